```python
import math
import jax, jax.numpy as jnp
from jax import lax
import numpy as np

D_MODEL = 1024
BATCH = 4
SEQ = 4096
DEPTH = 1
DEC_BATCH = 128
DEC_SEQ = 1
PAST_LEN = 8192
PAGE_SIZE = 128

HEAD_DIM = 64
N_HEADS = (D_MODEL // 2) // HEAD_DIM
N_KV = 2
Q_PER_KV = N_HEADS // N_KV
ATT_WIDTH = N_HEADS * HEAD_DIM
KV_WIDTH = N_KV * HEAD_DIM
CMP_LEN = 32
CMP_STRIDE = 16
CMP_HIDDEN = HEAD_DIM
SEL_BLOCK = 64
SEL_TOPN = 16
WINDOW = 512
ROPE_THETA = 10000.0
SSM_GROUP = 16
SSM_WIDTH = D_MODEL // 2
SSM_GROUPS = SSM_WIDTH // SSM_GROUP
SSM_STATE = 64
PEER_HEADS = 8
PEER_NKEYS = 128
PEER_EXPERTS = PEER_NKEYS * PEER_NKEYS
PEER_QDIM = 256
PEER_TOPK = 16
PEER_CHUNK = 128
Q_BLOCK = 64
IN_COLS = ATT_WIDTH + 6 * KV_WIDTH + 3 * N_HEADS + SSM_WIDTH + 2 * D_MODEL
RMS_EPS = 1e-6
NEG = -1e30
BIG = 1e9
TINY = 1e-30

kernel_name = 'nsa_s5_peer_hybrid_step'


def _split_points():
    sizes = [ATT_WIDTH] + [KV_WIDTH] * 6 + [3 * N_HEADS, SSM_WIDTH]
    pts, acc = [], 0
    for s in sizes:
        acc += s
        pts.append(acc)
    return pts


def _rmsnorm(x, g):
    xf = x.astype(jnp.float32)
    y = xf * lax.rsqrt(jnp.mean(xf * xf, axis=-1, keepdims=True) + RMS_EPS) * g.astype(jnp.float32)
    return y.astype(x.dtype)


def _rope(x, pos):
    half = HEAD_DIM // 2
    inv = ROPE_THETA ** (-jnp.arange(half, dtype=jnp.float32) / half)
    ang = pos.astype(jnp.float32)[:, None] * inv[None, :]
    cos = jnp.cos(ang)[:, None, :]
    sin = jnp.sin(ang)[:, None, :]
    xf = x.astype(jnp.float32)
    x1, x2 = xf[..., :half], xf[..., half:]
    return jnp.concatenate([x1 * cos - x2 * sin, x2 * cos + x1 * sin], axis=-1).astype(x.dtype)


def _masked_softmax(s, mask):
    s = jnp.where(mask, s, NEG)
    e = jnp.where(mask, jnp.exp(s - jnp.max(s, axis=-1, keepdims=True)), 0.0)
    return e / jnp.maximum(jnp.sum(e, axis=-1, keepdims=True), TINY)


def _compress(k, pe, w1, w2):
    B, L = k.shape[:2]
    lp = -(-L // CMP_STRIDE) * CMP_STRIDE
    k = jnp.pad(k, ((0, 0), (0, lp - L), (0, 0), (0, 0)))
    n_chunk = lp // CMP_STRIDE
    chunks = k.reshape(B, n_chunk, CMP_STRIDE, N_KV, HEAD_DIM)
    r = CMP_LEN // CMP_STRIDE
    nb = n_chunk - r + 1
    hid = None
    for j in range(r):
        wj = w1[j * CMP_STRIDE:(j + 1) * CMP_STRIDE]
        pj = pe[j * CMP_STRIDE:(j + 1) * CMP_STRIDE]
        part = jnp.einsum('bcsgd,sdh->bcgh', chunks, wj)[:, j:j + nb] + jnp.einsum('sd,sdh->h', pj, wj)
        hid = part if hid is None else hid + part
    out = jax.nn.gelu(hid) @ w2
    cend = jnp.arange(nb, dtype=jnp.int32) * CMP_STRIDE + CMP_LEN - 1
    return out, cend


def _cmp_to_sel(nc, ns):
    i = jnp.arange(nc)[:, None]
    j = jnp.arange(ns)[None, :]
    st = i * CMP_STRIDE
    en = st + CMP_LEN - 1
    ss = j * SEL_BLOCK
    se = ss + SEL_BLOCK - 1
    return ((st <= se) & (en >= ss)).astype(jnp.float32)


def _nsa(q, qpos, gates, ck, cv, cend, n_sel_blocks, gather_sel, kwin, vwin, kpos):
    f32 = jnp.float32
    dt = q.dtype
    scale = HEAD_DIM ** -0.5
    s_c = jnp.einsum('bqgrd,bngd->bqgrn', q, ck).astype(f32) * scale
    m_c = (cend[None, :] <= qpos[:, None])[None, :, None, None, :]
    p_c = _masked_softmax(s_c, m_c)
    o_c = jnp.einsum('bqgrn,bngd->bqgrd', p_c.astype(dt), cv)
    imp = jnp.einsum('bqgn,ns->bqgs', jnp.sum(p_c, axis=3), _cmp_to_sel(ck.shape[1], n_sel_blocks))
    j = jnp.arange(n_sel_blocks)[None, :]
    cur = (qpos // SEL_BLOCK)[:, None]
    valid = j * SEL_BLOCK <= qpos[:, None]
    force = (j == 0) | (j == cur) | (j == cur - 1)
    score = jnp.where((valid & force)[None, :, None, :], BIG, jnp.where(valid[None, :, None, :], imp, -BIG))
    _, idx = lax.top_k(score, min(SEL_TOPN, n_sel_blocks))
    k_s, v_s, p_s = gather_sel(idx)
    s_s = jnp.einsum('bqgrd,bqgkd->bqgrk', q, k_s).astype(f32) * scale
    m_s = (p_s <= qpos[None, :, None, None])[:, :, :, None, :]
    o_s = jnp.einsum('bqgrk,bqgkd->bqgrd', _masked_softmax(s_s, m_s).astype(dt), v_s)
    s_w = jnp.einsum('bqgrd,bkgd->bqgrk', q, kwin).astype(f32) * scale
    dlt = qpos[:, None] - kpos[None, :]
    m_w = ((kpos[None, :] >= 0) & (dlt >= 0) & (dlt < WINDOW))[None, :, None, None, :]
    o_w = jnp.einsum('bqgrk,bkgd->bqgrd', _masked_softmax(s_w, m_w).astype(dt), vwin)
    g = jax.nn.sigmoid(gates.astype(f32))
    o = g[..., 0:1] * o_c + g[..., 1:2] * o_s + g[..., 2:3] * o_w
    B, Q = q.shape[:2]
    return o.reshape(B, Q, ATT_WIDTH).astype(dt)


def _scan_combine(e1, e2):
    a1, b1 = e1
    a2, b2 = e2
    return a1 * a2, a2 * b1 + b2


def _ssm(u, h0, lp):
    f32 = jnp.float32
    B, T = u.shape[:2]
    uf = u.reshape(B, T, SSM_GROUPS, SSM_GROUP).astype(f32)
    lam = lax.complex(lp['ssm_a_re'].astype(f32), lp['ssm_a_im'].astype(f32))
    step = jnp.exp(lp['ssm_log_dt'].astype(f32))[:, None]
    a_bar = jnp.exp(lam * step)
    b_bar = ((a_bar - 1.0) / lam)[..., None] * lax.complex(lp['ssm_b_re'].astype(f32), lp['ssm_b_im'].astype(f32))
    c = lax.complex(lp['ssm_c_re'].astype(f32), lp['ssm_c_im'].astype(f32))
    bu = jnp.einsum('gpc,btgc->btgp', b_bar, uf.astype(jnp.complex64))
    bu = bu.at[:, 0].add(a_bar * h0)
    a_seq = jnp.broadcast_to(a_bar, bu.shape)
    _, hs = lax.associative_scan(_scan_combine, (a_seq, bu), axis=1)
    y = jnp.real(jnp.einsum('gcp,btgp->btgc', c, hs)) + lp['ssm_d'].astype(f32) * uf
    return y.reshape(B, T, SSM_WIDTH), hs[:, -1]


def _merge(o_att, y_ssm, g_mrg, lp):
    dt = o_att.dtype
    a = o_att @ lp['w_att_proj']
    gl = jax.nn.gelu(y_ssm).astype(dt) @ lp['w_ssm_glu']
    s = gl[..., :D_MODEL] * jax.nn.sigmoid(gl[..., D_MODEL:])
    m = jax.nn.sigmoid(g_mrg[..., :D_MODEL]) * a + jax.nn.sigmoid(g_mrg[..., D_MODEL:]) * s
    return m @ lp['w_out']


def _project(h, lp, pos):
    B, T, _ = h.shape
    z = h @ lp['w_in']
    q, kc, vc, ks, vs, kw, vw, g_nsa, u, g_mrg = jnp.split(z, _split_points(), axis=-1)
    kv = lambda t: t.reshape(B, T, N_KV, HEAD_DIM)
    q = _rope(q.reshape(B, T, N_HEADS, HEAD_DIM), pos).reshape(B, T, N_KV, Q_PER_KV, HEAD_DIM)
    return (q, _rope(kv(kc), pos), kv(vc), _rope(kv(ks), pos), kv(vs), _rope(kv(kw), pos), kv(vw),
            g_nsa.reshape(B, T, N_KV, Q_PER_KV, 3), u, g_mrg)


def _mix_prompt(h, lp):
    B, T, _ = h.shape
    pos = jnp.arange(T, dtype=jnp.int32)
    q, kc, vc, ks, vs, kw, vw, gates, u, g_mrg = _project(h, lp, pos)
    ck, cend = _compress(kc, lp['cmp_pe_k'], lp['cmp_w1_k'], lp['cmp_w2_k'])
    cv, _ = _compress(vc, lp['cmp_pe_v'], lp['cmp_w1_v'], lp['cmp_w2_v'])
    ns = -(-T // SEL_BLOCK)
    bi = jnp.arange(B)[:, None, None, None, None]
    gi = jnp.arange(N_KV)[None, None, :, None, None]

    def gather_sel(idx):
        p = idx[..., None] * SEL_BLOCK + jnp.arange(SEL_BLOCK)
        pc = jnp.minimum(p, T - 1)
        Q, K = idx.shape[1], idx.shape[3] * SEL_BLOCK
        return (ks[bi, pc, gi].reshape(B, Q, N_KV, K, HEAD_DIM),
                vs[bi, pc, gi].reshape(B, Q, N_KV, K, HEAD_DIM),
                p.reshape(B, Q, N_KV, K))

    kw_pad = jnp.pad(kw, ((0, 0), (WINDOW, 0), (0, 0), (0, 0)))
    vw_pad = jnp.pad(vw, ((0, 0), (WINDOW, 0), (0, 0), (0, 0)))
    nqb = T // Q_BLOCK
    q_blk = jnp.swapaxes(q.reshape(B, nqb, Q_BLOCK, N_KV, Q_PER_KV, HEAD_DIM), 0, 1)
    g_blk = jnp.swapaxes(gates.reshape(B, nqb, Q_BLOCK, N_KV, Q_PER_KV, 3), 0, 1)
    starts = jnp.arange(nqb, dtype=jnp.int32) * Q_BLOCK

    def one_block(args):
        qb, gb, s = args
        qpos = s + jnp.arange(Q_BLOCK, dtype=jnp.int32)
        kwin = lax.dynamic_slice_in_dim(kw_pad, s, Q_BLOCK + WINDOW, axis=1)
        vwin = lax.dynamic_slice_in_dim(vw_pad, s, Q_BLOCK + WINDOW, axis=1)
        kpos = s - WINDOW + jnp.arange(Q_BLOCK + WINDOW, dtype=jnp.int32)
        return _nsa(qb, qpos, gb, ck, cv, cend, ns, gather_sel, kwin, vwin, kpos)

    o = lax.map(one_block, (q_blk, g_blk, starts))
    o = jnp.swapaxes(o, 0, 1).reshape(B, T, ATT_WIDTH)
    h0 = jnp.zeros((B, SSM_GROUPS, SSM_STATE), jnp.complex64)
    y_ssm, h_last = _ssm(u, h0, lp)
    out = _merge(o, y_ssm, g_mrg, lp)
    wb = min(WINDOW, T)
    new = (kc, vc, ks, vs, kw[:, T - wb:], vw[:, T - wb:], jnp.real(h_last), jnp.imag(h_last))
    return out, new


def _mix_sample(h, ckp, cvp, ksp, vsp, kwb, vwb, s_re, s_im, page_table, lp):
    DB, S, _ = h.shape
    n_pages = page_table.shape[1]
    past = n_pages * PAGE_SIZE
    pos = past + jnp.arange(S, dtype=jnp.int32)
    q, kc, vc, ks, vs, kw, vw, gates, u, g_mrg = _project(h, lp, pos)
    full_kc = jnp.concatenate([ckp[page_table].reshape(DB, past, N_KV, HEAD_DIM), kc], axis=1)
    full_vc = jnp.concatenate([cvp[page_table].reshape(DB, past, N_KV, HEAD_DIM), vc], axis=1)
    ck, cend = _compress(full_kc, lp['cmp_pe_k'], lp['cmp_w1_k'], lp['cmp_w2_k'])
    cv, _ = _compress(full_vc, lp['cmp_pe_v'], lp['cmp_w1_v'], lp['cmp_w2_v'])
    ns = -(-(past + S) // SEL_BLOCK)
    bi = jnp.arange(DB)[:, None, None, None, None]
    gi = jnp.arange(N_KV)[None, None, :, None, None]

    def gather_sel(idx):
        p = idx[..., None] * SEL_BLOCK + jnp.arange(SEL_BLOCK)
        is_past = (p < past)[..., None]
        pp = jnp.clip(p, 0, past - 1)
        page = page_table[bi, pp // PAGE_SIZE]
        off = pp % PAGE_SIZE
        pn = jnp.clip(p - past, 0, S - 1)
        Q, K = idx.shape[1], idx.shape[3] * SEL_BLOCK
        kk = jnp.where(is_past, ksp[page, off, gi], ks[bi, pn, gi])
        vv = jnp.where(is_past, vsp[page, off, gi], vs[bi, pn, gi])
        return (kk.reshape(DB, Q, N_KV, K, HEAD_DIM), vv.reshape(DB, Q, N_KV, K, HEAD_DIM),
                p.reshape(DB, Q, N_KV, K))

    wbuf = kwb.shape[1]
    kwin = jnp.concatenate([kwb, kw], axis=1)
    vwin = jnp.concatenate([vwb, vw], axis=1)
    kpos = past - wbuf + jnp.arange(wbuf + S, dtype=jnp.int32)
    o = _nsa(q, pos, gates, ck, cv, cend, ns, gather_sel, kwin, vwin, kpos)
    h0 = lax.complex(s_re.astype(jnp.float32), s_im.astype(jnp.float32))
    y_ssm, h_last = _ssm(u, h0, lp)
    out = _merge(o, y_ssm, g_mrg, lp)
    nw = min(WINDOW, wbuf + S)
    new = (kc, vc, ks, vs, kwin[:, wbuf + S - nw:], vwin[:, wbuf + S - nw:], jnp.real(h_last), jnp.imag(h_last))
    return out, new


def _peer(h, w_q, k1, k2, u_tab, v_tab):
    shp = h.shape
    hf = h.reshape(-1, D_MODEL)
    n = hf.shape[0]
    n_pad = -(-n // PEER_CHUNK) * PEER_CHUNK
    hf = jnp.pad(hf, ((0, n_pad - n), (0, 0))).reshape(n_pad // PEER_CHUNK, PEER_CHUNK, D_MODEL)
    half = PEER_QDIM // 2

    def one(hc):
        q = (hc @ w_q).reshape(PEER_CHUNK, PEER_HEADS, PEER_QDIM)
        s1 = jnp.einsum('nhd,kd->nhk', q[..., :half], k1).astype(jnp.float32)
        s2 = jnp.einsum('nhd,kd->nhk', q[..., half:], k2).astype(jnp.float32)
        v1, i1 = lax.top_k(s1, PEER_TOPK)
        v2, i2 = lax.top_k(s2, PEER_TOPK)
        cand = (v1[..., :, None] + v2[..., None, :]).reshape(PEER_CHUNK, PEER_HEADS, PEER_TOPK * PEER_TOPK)
        sc, ci = lax.top_k(cand, PEER_TOPK)
        e = (jnp.take_along_axis(i1, ci // PEER_TOPK, axis=-1) * PEER_NKEYS
             + jnp.take_along_axis(i2, ci % PEER_TOPK, axis=-1))
        g = jax.nn.softmax(sc, axis=-1)
        act = jax.nn.gelu(jnp.einsum('nhkd,nd->nhk', u_tab[e], hc).astype(jnp.float32))
        return jnp.einsum('nhk,nhkd->nd', (g * act).astype(hc.dtype), v_tab[e])

    out = lax.map(one, hf).reshape(n_pad, D_MODEL)[:n]
    return out.reshape(shp)


def setup_inputs(seed: int = 0) -> dict:
    key = jax.random.key(seed)
    ks = jax.random.split(key, 40)
    f32 = jnp.float32
    nrm = lambda k, shape, s: jax.random.normal(k, shape, f32) * s
    n_pages = PAST_LEN // PAGE_SIZE
    n_pool = (5 * DEC_BATCH * n_pages) // 4
    wbuf = min(WINDOW, PAST_LEN)
    pool = (DEPTH, n_pool, PAGE_SIZE, N_KV, HEAD_DIM)
    win = (DEPTH, DEC_BATCH, wbuf, N_KV, HEAD_DIM)
    st = (DEPTH, DEC_BATCH, SSM_GROUPS, SSM_STATE)
    page_table = jax.random.permutation(ks[10], n_pool)[:DEC_BATCH * n_pages].reshape(DEC_BATCH, n_pages).astype(jnp.int32)
    a_im = jnp.broadcast_to(math.pi * jnp.arange(SSM_STATE, dtype=f32), (DEPTH, SSM_GROUPS, SSM_STATE))
    return {
        'x_prompt': nrm(ks[0], (BATCH, SEQ, D_MODEL), 1.0),
        'x_sample': nrm(ks[1], (DEC_BATCH, DEC_SEQ, D_MODEL), 1.0),
        'cache_k_cmp': nrm(ks[2], pool, 1.0),
        'cache_v_cmp': nrm(ks[3], pool, 1.0),
        'cache_k_sel': nrm(ks[4], pool, 1.0),
        'cache_v_sel': nrm(ks[5], pool, 1.0),
        'cache_k_win': nrm(ks[6], win, 1.0),
        'cache_v_win': nrm(ks[7], win, 1.0),
        'state_ssm_re': nrm(ks[8], st, 0.1),
        'state_ssm_im': nrm(ks[9], st, 0.1),
        'page_table': page_table,
        'norm_mix': 1.0 + nrm(ks[11], (DEPTH, D_MODEL), 0.01),
        'w_in': nrm(ks[12], (DEPTH, D_MODEL, IN_COLS), D_MODEL ** -0.5),
        'cmp_pe_k': nrm(ks[13], (DEPTH, CMP_LEN, HEAD_DIM), 0.1),
        'cmp_w1_k': nrm(ks[14], (DEPTH, CMP_LEN, HEAD_DIM, CMP_HIDDEN), (CMP_LEN * HEAD_DIM) ** -0.5),
        'cmp_w2_k': nrm(ks[15], (DEPTH, CMP_HIDDEN, HEAD_DIM), CMP_HIDDEN ** -0.5),
        'cmp_pe_v': nrm(ks[16], (DEPTH, CMP_LEN, HEAD_DIM), 0.1),
        'cmp_w1_v': nrm(ks[17], (DEPTH, CMP_LEN, HEAD_DIM, CMP_HIDDEN), (CMP_LEN * HEAD_DIM) ** -0.5),
        'cmp_w2_v': nrm(ks[18], (DEPTH, CMP_HIDDEN, HEAD_DIM), CMP_HIDDEN ** -0.5),
        'ssm_a_re': -0.5 + nrm(ks[19], (DEPTH, SSM_GROUPS, SSM_STATE), 0.01),
        'ssm_a_im': a_im + nrm(ks[20], (DEPTH, SSM_GROUPS, SSM_STATE), 0.01),
        'ssm_log_dt': jax.random.uniform(ks[21], (DEPTH, SSM_GROUPS), f32, math.log(0.001), math.log(0.1)),
        'ssm_b_re': nrm(ks[22], (DEPTH, SSM_GROUPS, SSM_STATE, SSM_GROUP), (2 * SSM_GROUP) ** -0.5),
        'ssm_b_im': nrm(ks[23], (DEPTH, SSM_GROUPS, SSM_STATE, SSM_GROUP), (2 * SSM_GROUP) ** -0.5),
        'ssm_c_re': nrm(ks[24], (DEPTH, SSM_GROUPS, SSM_GROUP, SSM_STATE), (2 * SSM_STATE) ** -0.5),
        'ssm_c_im': nrm(ks[25], (DEPTH, SSM_GROUPS, SSM_GROUP, SSM_STATE), (2 * SSM_STATE) ** -0.5),
        'ssm_d': nrm(ks[26], (DEPTH, SSM_GROUPS, SSM_GROUP), 1.0),
        'w_att_proj': nrm(ks[27], (DEPTH, ATT_WIDTH, D_MODEL), ATT_WIDTH ** -0.5),
        'w_ssm_glu': nrm(ks[28], (DEPTH, SSM_WIDTH, 2 * D_MODEL), SSM_WIDTH ** -0.5),
        'w_out': nrm(ks[29], (DEPTH, D_MODEL, D_MODEL), D_MODEL ** -0.5),
        'norm_ffn': 1.0 + nrm(ks[30], (DEPTH, D_MODEL), 0.01),
        'peer_w_q': nrm(ks[31], (DEPTH, D_MODEL, PEER_HEADS * PEER_QDIM), D_MODEL ** -0.5),
        'peer_keys1': nrm(ks[32], (DEPTH, PEER_NKEYS, PEER_QDIM // 2), (PEER_QDIM // 2) ** -0.5),
        'peer_keys2': nrm(ks[33], (DEPTH, PEER_NKEYS, PEER_QDIM // 2), (PEER_QDIM // 2) ** -0.5),
        'peer_u': nrm(ks[34], (DEPTH, PEER_EXPERTS, D_MODEL), D_MODEL ** -0.5),
        'peer_v': nrm(ks[35], (DEPTH, PEER_EXPERTS, D_MODEL), PEER_HEADS ** -0.5),
        'norm_final': 1.0 + nrm(ks[36], (D_MODEL,), 0.01),
    }


def reference(x_prompt, x_sample, cache_k_cmp, cache_v_cmp, cache_k_sel, cache_v_sel, cache_k_win, cache_v_win,
              state_ssm_re, state_ssm_im, page_table, norm_mix, w_in, cmp_pe_k, cmp_w1_k, cmp_w2_k,
              cmp_pe_v, cmp_w1_v, cmp_w2_v, ssm_a_re, ssm_a_im, ssm_log_dt, ssm_b_re, ssm_b_im, ssm_c_re,
              ssm_c_im, ssm_d, w_att_proj, w_ssm_glu, w_out, norm_ffn, peer_w_q, peer_keys1, peer_keys2,
              peer_u, peer_v, norm_final):
    yp, ys = x_prompt, x_sample
    sp, ss = [], []
    for l in range(DEPTH):
        lp = {'w_in': w_in[l], 'cmp_pe_k': cmp_pe_k[l], 'cmp_w1_k': cmp_w1_k[l], 'cmp_w2_k': cmp_w2_k[l],
              'cmp_pe_v': cmp_pe_v[l], 'cmp_w1_v': cmp_w1_v[l], 'cmp_w2_v': cmp_w2_v[l],
              'ssm_a_re': ssm_a_re[l], 'ssm_a_im': ssm_a_im[l], 'ssm_log_dt': ssm_log_dt[l],
              'ssm_b_re': ssm_b_re[l], 'ssm_b_im': ssm_b_im[l], 'ssm_c_re': ssm_c_re[l], 'ssm_c_im': ssm_c_im[l],
              'ssm_d': ssm_d[l], 'w_att_proj': w_att_proj[l], 'w_ssm_glu': w_ssm_glu[l], 'w_out': w_out[l]}
        mp, st_p = _mix_prompt(_rmsnorm(yp, norm_mix[l]), lp)
        ms, st_s = _mix_sample(_rmsnorm(ys, norm_mix[l]), cache_k_cmp[l], cache_v_cmp[l], cache_k_sel[l],
                               cache_v_sel[l], cache_k_win[l], cache_v_win[l], state_ssm_re[l], state_ssm_im[l],
                               page_table, lp)
        yp = yp + mp
        ys = ys + ms
        yp = yp + _peer(_rmsnorm(yp, norm_ffn[l]), peer_w_q[l], peer_keys1[l], peer_keys2[l], peer_u[l], peer_v[l])
        ys = ys + _peer(_rmsnorm(ys, norm_ffn[l]), peer_w_q[l], peer_keys1[l], peer_keys2[l], peer_u[l], peer_v[l])
        sp.append(st_p)
        ss.append(st_s)
    y_prompt = _rmsnorm(yp, norm_final)
    y_sample = _rmsnorm(ys, norm_final)
    stk = lambda lst, i: jnp.stack([t[i] for t in lst])
    p_k_cmp, p_v_cmp, p_k_sel, p_v_sel = stk(sp, 0), stk(sp, 1), stk(sp, 2), stk(sp, 3)
    p_k_win, p_v_win, p_ssm_re, p_ssm_im = stk(sp, 4), stk(sp, 5), stk(sp, 6), stk(sp, 7)
    s_k_cmp, s_v_cmp, s_k_sel, s_v_sel = stk(ss, 0), stk(ss, 1), stk(ss, 2), stk(ss, 3)
    s_k_win, s_v_win, s_ssm_re, s_ssm_im = stk(ss, 4), stk(ss, 5), stk(ss, 6), stk(ss, 7)
    return (y_prompt, y_sample, p_k_cmp, p_v_cmp, p_k_sel, p_v_sel, p_k_win, p_v_win, p_ssm_re, p_ssm_im,
            s_k_cmp, s_v_cmp, s_k_sel, s_v_sel, s_k_win, s_v_win, s_ssm_re, s_ssm_im)
```

```python
import functools
import math

import jax
import jax.numpy as jnp
from jax import lax
from jax.experimental import pallas as pl
from jax.experimental.pallas import tpu as pltpu

F32 = jnp.float32
BF16 = jnp.bfloat16

D_MODEL = 1024
HEAD_DIM = 64
N_HEADS = 8
N_KV = 2
Q_PER_KV = 4
ATT_WIDTH = 512
KV_WIDTH = 128
CMP_LEN = 32
CMP_STRIDE = 16
SEL_BLOCK = 64
SEL_TOPN = 16
WINDOW = 512
ROPE_THETA = 10000.0
PAGE_SIZE = 128
SSM_GROUP = 16
SSM_WIDTH = 512
SSM_GROUPS = 32
SSM_STATE = 64
PEER_HEADS = 8
PEER_NKEYS = 128
PEER_QDIM = 256
PEER_TOPK = 16
RMS_EPS = 1e-6
NEG = -1e30
BIG = 1e9
TINY = 1e-30
LOWEST = -3.0e38

LANE = 128
VMEM_LIMIT = 56 * 1024 * 1024

_NT = (((1,), (1,)), ((), ()))


def _params(sem, vmem=VMEM_LIMIT):
    return pltpu.CompilerParams(dimension_semantics=sem, vmem_limit_bytes=vmem)


def _split(x):
    hi = x.astype(BF16)
    lo = (x - hi.astype(F32)).astype(BF16)
    return hi, lo


def _dot(a, b):
    return jnp.dot(a, b, preferred_element_type=F32)


def _dot_nt(a, b):
    return lax.dot_general(a, b, _NT, preferred_element_type=F32)


def _dot3(a_hi, a_lo, b_hi, b_lo):
    return _dot(a_hi, b_hi) + (_dot(a_hi, b_lo) + _dot(a_lo, b_hi))


def _dot3_nt(a_hi, a_lo, b_hi, b_lo):
    return _dot_nt(a_hi, b_hi) + (_dot_nt(a_hi, b_lo) + _dot_nt(a_lo, b_hi))


def _rms(x, g):
    return x * lax.rsqrt(jnp.mean(x * x, axis=-1, keepdims=True) + RMS_EPS) * g


def _softmax_rows(s, mask):
    s = jnp.where(mask, s, NEG)
    e = jnp.where(mask, jnp.exp(s - jnp.max(s, axis=-1, keepdims=True)), 0.0)
    return e / jnp.maximum(jnp.sum(e, axis=-1, keepdims=True), TINY)


_SEG = {'q': (0, 512), 'kc': (512, 640), 'vc': (640, 768), 'ks': (768, 896), 'vs': (896, 1024),
        'kw': (1024, 1152), 'vw': (1152, 1280), 'gt': (1280, 1408), 'u': (1408, 1920), 'gm': (1920, 3968)}
_PROJ_COLS = 3968


def _proj_kernel(*refs, precise):
    x_ref, g_ref, cos_ref, sin_ref = refs[:4]
    n_w = 2 if precise else 1
    w_refs = refs[4:4 + n_w]
    q_ref, kc_ref, vc_ref, ks_ref, vs_ref, kw_ref, vw_ref, gt_ref, u_ref, gm_ref = refs[4 + n_w:]
    h = _rms(x_ref[0], g_ref[...])
    h_hi = h.astype(BF16)
    h_lo = (h - h_hi.astype(F32)).astype(BF16) if precise else None

    def mm(c0, c1):
        z = _dot(h_hi, w_refs[0][:, c0:c1])
        if precise:
            z = z + (_dot(h_hi, w_refs[1][:, c0:c1]) + _dot(h_lo, w_refs[0][:, c0:c1]))
        return z

    cos = cos_ref[...]
    sin = sin_ref[...]
    first = (lax.broadcasted_iota(jnp.int32, (1, LANE), 1) % HEAD_DIM) < (HEAD_DIM // 2)

    def rope(z):
        rot = jnp.where(first, pltpu.roll(z, LANE - HEAD_DIM // 2, 1), pltpu.roll(z, HEAD_DIM // 2, 1))
        return z * cos + rot * sin

    for i in range(4):
        q_ref[0, :, i * LANE:(i + 1) * LANE] = rope(mm(i * LANE, (i + 1) * LANE))
    kc_ref[0] = rope(mm(*_SEG['kc']))
    vc_ref[0] = mm(*_SEG['vc'])
    ks_ref[0] = rope(mm(*_SEG['ks']))
    vs_ref[0] = mm(*_SEG['vs'])
    kw_ref[0] = rope(mm(*_SEG['kw']))
    vw_ref[0] = mm(*_SEG['vw'])
    gt_ref[0] = mm(*_SEG['gt'])
    u_ref[0] = mm(*_SEG['u'])
    for i in range(4):
        c0 = _SEG['gm'][0] + i * 512
        gm_ref[0, :, i * 512:(i + 1) * 512] = mm(c0, c0 + 512)


def _project(x, gain, cos, sin, w_list, tm, precise):
    B, T, D = x.shape
    widths = [512, 128, 128, 128, 128, 128, 128, 128, 512, 2048]
    tok = lambda w: pl.BlockSpec((1, tm, w), lambda b, t: (b, t, 0))
    const = lambda a: pl.BlockSpec(a.shape, lambda b, t: (0,) * a.ndim)
    return pl.pallas_call(
        functools.partial(_proj_kernel, precise=precise),
        grid=(B, T // tm),
        in_specs=[tok(D), const(gain), pl.BlockSpec((tm, LANE), lambda b, t: (t, 0)),
                  pl.BlockSpec((tm, LANE), lambda b, t: (t, 0))] + [const(w) for w in w_list],
        out_specs=[tok(w) for w in widths],
        out_shape=[jax.ShapeDtypeStruct((B, T, w), F32) for w in widths],
        compiler_params=_params(("parallel", "arbitrary")),
        name="proj",
    )(x, gain, cos, sin, *w_list)


def _compress_rows(x, last_p1, wc_ref, pe_ref, w2_ref):
    C = x.shape[0]
    p = _dot(x.astype(BF16), wc_ref[...])
    pb = _dot(pe_ref[...].astype(BF16), wc_ref[...])
    bias = pb[0:1, :LANE] + pb[1:2, LANE:]
    p1 = pltpu.roll(p[:, LANE:], C - 1, 0)
    if last_p1 is not None:
        row = lax.broadcasted_iota(jnp.int32, (C, 1), 0)
        p1 = jnp.where(row == C - 1, last_p1, p1)
    hid = p[:, :LANE] + p1 + bias
    return _dot(jax.nn.gelu(hid).astype(BF16), w2_ref[...])


def _compress_prompt_kernel(xk_ref, xv_ref, wck_ref, pek_ref, w2k_ref, wcv_ref, pev_ref, w2v_ref, ck_ref, cv_ref):
    ck_ref[0] = _compress_rows(xk_ref[0], None, wck_ref, pek_ref, w2k_ref)
    cv_ref[0] = _compress_rows(xv_ref[0], None, wcv_ref, pev_ref, w2v_ref)


def _compress_prompt(xk, xv, cw):
    B, C, W = xk.shape
    seq = pl.BlockSpec((1, C, W), lambda b: (b, 0, 0))
    const = lambda a: pl.BlockSpec(a.shape, lambda b: (0,) * a.ndim)
    out = pl.BlockSpec((1, C, LANE), lambda b: (b, 0, 0))
    return pl.pallas_call(
        _compress_prompt_kernel,
        grid=(B,),
        in_specs=[seq, seq] + [const(a) for a in cw],
        out_specs=[out, out],
        out_shape=[jax.ShapeDtypeStruct((B, C, LANE), F32)] * 2,
        compiler_params=_params(("arbitrary",)),
        name="compress_prompt",
    )(xk, xv, *cw)


def _page_copies(pt_ref, b, slot, pool_ref, buf_ref, sem_ref, n_pages, rows):
    return [pltpu.make_async_copy(pool_ref.at[pt_ref[b, p]], buf_ref.at[slot, pl.ds(p * rows, rows), :],
                                  sem_ref.at[slot]) for p in range(n_pages)]


def _paged_fetch(pt_ref, pools, bufs, sems, n_pages, rows):
    b = pl.program_id(0)
    nb = pl.num_programs(0)
    slot = b % 2

    def start(bb, sl):
        for pool, buf, sem in zip(pools, bufs, sems):
            for cp in _page_copies(pt_ref, bb, sl, pool, buf, sem, n_pages, rows):
                cp.start()

    @pl.when(b == 0)
    def _():
        start(b, slot)

    @pl.when(b + 1 < nb)
    def _():
        start(b + 1, 1 - slot)

    for pool, buf, sem in zip(pools, bufs, sems):
        for cp in _page_copies(pt_ref, b, slot, pool, buf, sem, n_pages, rows):
            cp.wait()
    return slot


def _compress_sample_kernel(pt_ref, kn_ref, vn_ref, wck_ref, pek_ref, w2k_ref, wcv_ref, pev_ref, w2v_ref,
                            kpool_ref, vpool_ref, ck_ref, cv_ref, kbuf, vbuf, ksem, vsem, *, n_pages):
    slot = _paged_fetch(pt_ref, (kpool_ref, vpool_ref), (kbuf, vbuf), (ksem, vsem), n_pages, PAGE_SIZE // CMP_STRIDE)

    def one(buf, new_ref, wc_ref, pe_ref, w2_ref, out_ref):
        new = jnp.broadcast_to(new_ref[0], (8, LANE)).astype(BF16)
        last_p1 = _dot(new, wc_ref[0:LANE, LANE:])[0:1]
        out_ref[0] = _compress_rows(buf[slot], last_p1, wc_ref, pe_ref, w2_ref)

    one(kbuf, kn_ref, wck_ref, pek_ref, w2k_ref, ck_ref)
    one(vbuf, vn_ref, wcv_ref, pev_ref, w2v_ref, cv_ref)


def _compress_sample(page_table, kpool, vpool, k_new, v_new, cw):
    DB, n_pages = page_table.shape
    rows = PAGE_SIZE // CMP_STRIDE
    C = n_pages * rows
    W = kpool.shape[-1]
    new = pl.BlockSpec((1, 1, LANE), lambda b, pt: (b, 0, 0))
    const = lambda a: pl.BlockSpec(a.shape, lambda b, pt: (0,) * a.ndim)
    hbm = pl.BlockSpec(memory_space=pl.ANY)
    out = pl.BlockSpec((1, C, LANE), lambda b, pt: (b, 0, 0))
    return pl.pallas_call(
        functools.partial(_compress_sample_kernel, n_pages=n_pages),
        grid_spec=pltpu.PrefetchScalarGridSpec(
            num_scalar_prefetch=1, grid=(DB,),
            in_specs=[new, new] + [const(a) for a in cw] + [hbm, hbm],
            out_specs=[out, out],
            scratch_shapes=[pltpu.VMEM((2, C, W), F32), pltpu.VMEM((2, C, W), F32),
                            pltpu.SemaphoreType.DMA((2,)), pltpu.SemaphoreType.DMA((2,))]),
        out_shape=[jax.ShapeDtypeStruct((DB, C, LANE), F32)] * 2,
        compiler_params=_params(("arbitrary",)),
        name="compress_sample",
    )(page_table, k_new, v_new, *cw, kpool, vpool)


def _select_blocks(imp, qpos, n_real):
    R, NS = imp.shape
    j = lax.broadcasted_iota(jnp.int32, (1, NS), 1)
    valid = j * SEL_BLOCK <= qpos
    cur = qpos // SEL_BLOCK
    force = (j == 0) | (j == cur) | (j == cur - 1)
    score = jnp.where(valid & force, BIG, jnp.where(valid, imp, -BIG))
    rank = jnp.zeros((R, NS), F32)
    for jp in range(n_real):
        col = score[:, jp:jp + 1]
        earlier = jnp.where(j > jp, 1.0, 0.0)
        rank = rank + jnp.where(col > score, 1.0, jnp.where(col == score, earlier, 0.0))
    return jnp.where(rank < SEL_TOPN, 1.0, 0.0)


def _attn_prompt_kernel(q_ref, gt_ref, ck_ref, cv_ref, ks_ref, vs_ref, kw_ref, vw_ref, msel_ref, exp_ref, o_ref,
                        m_sc, l_sc, acc_sc, *, tq, kc):
    s0 = pl.program_id(1) * tq
    R = Q_PER_KV * tq
    ncb = ck_ref.shape[1]
    qpos_r = s0 + lax.broadcasted_iota(jnp.int32, (R, 1), 0) % tq
    qpos_t = s0 + lax.broadcasted_iota(jnp.int32, (tq, 1), 0)
    sig = jax.nn.sigmoid(gt_ref[0])
    stack = lambda f: jnp.concatenate([f(r) for r in range(Q_PER_KV)], axis=0)
    tile4 = lambda a: jnp.concatenate([a] * Q_PER_KV, axis=0)
    heads = []
    for g in range(N_KV):
        gl = slice(g * HEAD_DIM, (g + 1) * HEAD_DIM)
        qg = stack(lambda r: q_ref[0, :, (g * Q_PER_KV + r) * HEAD_DIM:(g * Q_PER_KV + r + 1) * HEAD_DIM])
        qg = qg * (HEAD_DIM ** -0.5)
        q_hi, q_lo = _split(qg)

        k_hi, k_lo = _split(ck_ref[0, :, gl])
        s_c = _dot3_nt(q_hi, q_lo, k_hi, k_lo)
        cend = lax.broadcasted_iota(jnp.int32, (1, ncb), 1) * CMP_STRIDE + (CMP_LEN - 1)
        p_c = _softmax_rows(s_c, cend <= qpos_r)
        o_c = _dot(p_c.astype(BF16), cv_ref[0, :, gl].astype(BF16))
        p_sum = p_c[0:tq]
        for r in range(1, Q_PER_KV):
            p_sum = p_sum + p_c[r * tq:(r + 1) * tq]
        ps_hi, ps_lo = _split(p_sum)
        msel = msel_ref[...]
        imp = _dot(ps_hi, msel) + _dot(ps_lo, msel)
        sel = _select_blocks(imp, qpos_t, imp.shape[1]).astype(BF16)

        def attend(k_ref, v_ref, c_lo, c_hi, mask_fn):
            m_sc[...] = jnp.full((R, 1), NEG, F32)
            l_sc[...] = jnp.zeros((R, 1), F32)
            acc_sc[...] = jnp.zeros((R, HEAD_DIM), F32)

            def body(c, carry):
                off = pl.multiple_of(c * kc, kc)
                k = k_ref[0, pl.ds(off, kc), gl].astype(BF16)
                v = v_ref[0, pl.ds(off, kc), gl].astype(BF16)
                kpos = off + lax.broadcasted_iota(jnp.int32, (1, kc), 1)
                mask = mask_fn(c, kpos)
                s = jnp.where(mask, _dot_nt(q_hi, k), NEG)
                m_old = m_sc[...]
                m_new = jnp.maximum(m_old, jnp.max(s, axis=-1, keepdims=True))
                p = jnp.where(mask, jnp.exp(s - m_new), 0.0)
                alpha = jnp.exp(m_old - m_new)
                l_sc[...] = alpha * l_sc[...] + jnp.sum(p, axis=-1, keepdims=True)
                acc_sc[...] = alpha * acc_sc[...] + _dot(p.astype(BF16), v)
                m_sc[...] = m_new
                return carry

            lax.fori_loop(c_lo, c_hi, body, 0)
            return acc_sc[...] / jnp.maximum(l_sc[...], TINY)

        def sel_mask(c, kpos):
            picked = tile4(_dot(sel, exp_ref[c]))
            return (picked > 0.5) & (kpos <= qpos_r)

        def win_mask(c, kpos):
            dlt = qpos_r - kpos
            return (dlt >= 0) & (dlt < WINDOW)

        o_s = attend(ks_ref, vs_ref, 0, (s0 + tq) // kc, sel_mask)
        o_w = attend(kw_ref, vw_ref, jnp.maximum(s0 - WINDOW, 0) // kc, (s0 + tq) // kc, win_mask)

        gate = lambda i: stack(lambda r: sig[:, (g * Q_PER_KV + r) * 3 + i:(g * Q_PER_KV + r) * 3 + i + 1])
        og = gate(0) * o_c + gate(1) * o_s + gate(2) * o_w
        heads += [og[r * tq:(r + 1) * tq] for r in range(Q_PER_KV)]
    o_ref[0] = jnp.concatenate(heads, axis=1)


def _attn_prompt(q, gt, ck, cv, ks, vs, kw, vw, tq=128, kc=128):
    B, T, _ = q.shape
    ncb = ck.shape[1]
    ns = T // SEL_BLOCK
    i = jnp.arange(ncb)[:, None]
    j = jnp.arange(ns)[None, :]
    msel = ((i * CMP_STRIDE <= j * SEL_BLOCK + SEL_BLOCK - 1)
            & (i * CMP_STRIDE + CMP_LEN - 1 >= j * SEL_BLOCK)).astype(BF16)
    key_blk = (jnp.arange(T) // SEL_BLOCK).reshape(T // kc, 1, kc)
    expand = (key_blk == jnp.arange(ns)[None, :, None]).astype(BF16)
    tok = lambda w: pl.BlockSpec((1, tq, w), lambda b, t: (b, t, 0))
    seq = lambda a: pl.BlockSpec((1,) + a.shape[1:], lambda b, t: (b, 0, 0))
    const = lambda a: pl.BlockSpec(a.shape, lambda b, t: (0,) * a.ndim)
    R = Q_PER_KV * tq
    return pl.pallas_call(
        functools.partial(_attn_prompt_kernel, tq=tq, kc=kc),
        grid=(B, T // tq),
        in_specs=[tok(ATT_WIDTH), tok(LANE), seq(ck), seq(cv), seq(ks), seq(vs), seq(kw), seq(vw),
                  const(msel), const(expand)],
        out_specs=tok(ATT_WIDTH),
        out_shape=jax.ShapeDtypeStruct((B, T, ATT_WIDTH), F32),
        scratch_shapes=[pltpu.VMEM((R, 1), F32), pltpu.VMEM((R, 1), F32), pltpu.VMEM((R, HEAD_DIM), F32)],
        compiler_params=_params(("parallel", "arbitrary")),
        name="attn_prompt",
    )(q, gt, ck, cv, ks, vs, kw, vw, msel, expand)


def _group_lanes():
    row = lax.broadcasted_iota(jnp.int32, (N_HEADS, LANE), 0) // Q_PER_KV
    lane = lax.broadcasted_iota(jnp.int32, (N_HEADS, LANE), 1) // HEAD_DIM
    return row == lane


def _attn_sample_cmp_kernel(q_ref, ck_ref, cv_ref, msel_ref, oc_ref, sel_ref, *, qpos, n_blocks):
    q_hi, q_lo = _split(q_ref[0] * (HEAD_DIM ** -0.5))
    k_hi, k_lo = _split(ck_ref[0])
    s_c = _dot3_nt(q_hi, q_lo, k_hi, k_lo)
    ncb = s_c.shape[1]
    cend = lax.broadcasted_iota(jnp.int32, (1, ncb), 1) * CMP_STRIDE + (CMP_LEN - 1)
    p_c = _softmax_rows(s_c, cend <= qpos)
    o_c = _dot(p_c.astype(BF16), cv_ref[0].astype(BF16))
    oc_ref[0] = jnp.where(_group_lanes(), o_c, 0.0)
    top = jnp.sum(p_c[0:Q_PER_KV], axis=0, keepdims=True)
    bot = jnp.sum(p_c[Q_PER_KV:], axis=0, keepdims=True)
    row = lax.broadcasted_iota(jnp.int32, (N_HEADS, 1), 0)
    p_sum = jnp.where(row < Q_PER_KV, top, bot)
    ps_hi, ps_lo = _split(p_sum)
    msel = msel_ref[...]
    imp = _dot(ps_hi, msel) + _dot(ps_lo, msel)
    sel_ref[0] = _select_blocks(imp, jnp.full((N_HEADS, 1), qpos, jnp.int32), n_blocks)


def _attn_sample_cmp(qbd, ck, cv, qpos, n_blocks, ns_pad):
    DB, ncb, _ = ck.shape
    i = jnp.arange(ncb)[:, None]
    j = jnp.arange(ns_pad)[None, :]
    msel = ((i * CMP_STRIDE <= j * SEL_BLOCK + SEL_BLOCK - 1) & (i * CMP_STRIDE + CMP_LEN - 1 >= j * SEL_BLOCK)
            & (j < n_blocks)).astype(BF16)
    row = lambda a: pl.BlockSpec((1,) + a.shape[1:], lambda b: (b, 0, 0))
    return pl.pallas_call(
        functools.partial(_attn_sample_cmp_kernel, qpos=qpos, n_blocks=n_blocks),
        grid=(DB,),
        in_specs=[row(qbd), row(ck), row(cv), pl.BlockSpec(msel.shape, lambda b: (0, 0))],
        out_specs=[pl.BlockSpec((1, N_HEADS, LANE), lambda b: (b, 0, 0)),
                   pl.BlockSpec((1, N_HEADS, ns_pad), lambda b: (b, 0, 0))],
        out_shape=[jax.ShapeDtypeStruct((DB, N_HEADS, LANE), F32), jax.ShapeDtypeStruct((DB, N_HEADS, ns_pad), F32)],
        compiler_params=_params(("arbitrary",)),
        name="attn_sample_cmp",
    )(qbd, ck, cv, msel)


def _attn_sample_kernel(pt_ref, q_ref, sel_ref, oc_ref, gt_ref, ksn_ref, vsn_ref, kwb_ref, vwb_ref, kwn_ref, vwn_ref,
                        exp_ref, kpool_ref, vpool_ref, o_ref, kbuf, vbuf, ksem, vsem, *, n_pages, qpos):
    slot = _paged_fetch(pt_ref, (kpool_ref, vpool_ref), (kbuf, vbuf), (ksem, vsem), n_pages, PAGE_SIZE)
    past = n_pages * PAGE_SIZE
    q = q_ref[0] * (HEAD_DIM ** -0.5)
    q_b = q.astype(BF16)
    sel = sel_ref[0]

    def attend(k_old, v_old, mask_old, k_new, v_new, mask_new):
        s = jnp.where(mask_old, _dot_nt(q_b, k_old.astype(BF16)), NEG)
        s_n = jnp.where(mask_new, jnp.sum(q * k_new, axis=-1, keepdims=True), NEG)
        m = jnp.maximum(jnp.max(s, axis=-1, keepdims=True), s_n)
        e = jnp.where(mask_old, jnp.exp(s - m), 0.0)
        e_n = jnp.where(mask_new, jnp.exp(s_n - m), 0.0)
        den = jnp.maximum(jnp.sum(e, axis=-1, keepdims=True) + e_n, TINY)
        return (_dot(e.astype(BF16), v_old.astype(BF16)) + e_n * v_new) / den

    picked = _dot(sel.astype(BF16), exp_ref[...])
    kpos = lax.broadcasted_iota(jnp.int32, (1, past), 1)
    nb_new = past // SEL_BLOCK
    o_s = attend(kbuf[slot], vbuf[slot], (picked > 0.5) & (kpos <= qpos), ksn_ref[0], vsn_ref[0],
                 (sel[:, nb_new:nb_new + 1] > 0.5) & (past <= qpos))
    wbuf = kwb_ref.shape[1]
    dlt = qpos - (past - wbuf + lax.broadcasted_iota(jnp.int32, (1, wbuf), 1))
    o_w = attend(kwb_ref[0], vwb_ref[0], (dlt >= 0) & (dlt < WINDOW), kwn_ref[0], vwn_ref[0],
                 jnp.full((N_HEADS, 1), (qpos - past >= 0) & (qpos - past < WINDOW)))
    sig = jax.nn.sigmoid(gt_ref[0])
    o = sig[:, 0:1] * oc_ref[0] + sig[:, 1:2] * o_s + sig[:, 2:3] * o_w
    o = jnp.where(_group_lanes(), o, 0.0)
    o_ref[0] = o[:, :HEAD_DIM] + o[:, HEAD_DIM:]


def _attn_sample(page_table, qbd, sel, oc, gt3, ksn, vsn, kwb, vwb, kwn, vwn, kpool, vpool, qpos):
    DB, n_pages = page_table.shape
    past = n_pages * PAGE_SIZE
    nsp = sel.shape[-1]
    expand = ((jnp.arange(past) // SEL_BLOCK)[None, :] == jnp.arange(nsp)[:, None]).astype(BF16)
    row = lambda a: pl.BlockSpec((1,) + a.shape[1:], lambda b, pt: (b, 0, 0))
    hbm = pl.BlockSpec(memory_space=pl.ANY)
    ins = [qbd, sel, oc, gt3, ksn, vsn, kwb, vwb, kwn, vwn]
    return pl.pallas_call(
        functools.partial(_attn_sample_kernel, n_pages=n_pages, qpos=qpos),
        grid_spec=pltpu.PrefetchScalarGridSpec(
            num_scalar_prefetch=1, grid=(DB,),
            in_specs=[row(a) for a in ins] + [pl.BlockSpec(expand.shape, lambda b, pt: (0, 0)), hbm, hbm],
            out_specs=pl.BlockSpec((1, N_HEADS, HEAD_DIM), lambda b, pt: (b, 0, 0)),
            scratch_shapes=[pltpu.VMEM((2, past, LANE), F32), pltpu.VMEM((2, past, LANE), F32),
                            pltpu.SemaphoreType.DMA((2,)), pltpu.SemaphoreType.DMA((2,))]),
        out_shape=jax.ShapeDtypeStruct((DB, N_HEADS, HEAD_DIM), F32),
        compiler_params=_params(("arbitrary",)),
        name="attn_sample",
    )(page_table, *ins, expand, kpool, vpool)


_SSM_HALF = SSM_WIDTH // 2
_SSM_ROW = SSM_GROUPS // 2 * SSM_STATE


def _ssm_params(a_re, a_im, log_dt, b_re, b_im, c_re, c_im, d):
    lam = lax.complex(a_re, a_im)
    step = jnp.exp(log_dt)[:, None]
    a_bar = jnp.exp(lam * step)
    b_bar = ((a_bar - 1.0) / lam)[..., None] * lax.complex(b_re, b_im)
    eye = jnp.eye(SSM_GROUPS // 2, dtype=F32)

    def b_mat(x):
        x = x.reshape(2, SSM_GROUPS // 2, SSM_STATE, SSM_GROUP)
        return jnp.einsum('hgpc,gk->hgckp', x, eye).reshape(2, _SSM_HALF, _SSM_ROW)

    def c_mat(x):
        x = x.reshape(2, SSM_GROUPS // 2, SSM_GROUP, SSM_STATE)
        return jnp.einsum('hgcp,gk->hgpkc', x, eye).reshape(2, _SSM_ROW, _SSM_HALF)

    bm = jnp.concatenate([b_mat(jnp.real(b_bar)), b_mat(jnp.imag(b_bar))], axis=2)
    cm = jnp.concatenate([c_mat(c_re), -c_mat(c_im)], axis=1)
    ar = jnp.real(a_bar).reshape(2, _SSM_ROW)
    ai = jnp.imag(a_bar).reshape(2, _SSM_ROW)
    bm_hi, bm_lo = _split(bm)
    cm_hi, cm_lo = _split(cm)
    return ar, ai, bm_hi, bm_lo, cm_hi, cm_lo, d.reshape(1, SSM_WIDTH)


def _ssm_prompt_kernel(u_ref, ar_ref, ai_ref, bh_ref, bl_ref, ch_ref, cl_ref, d_ref, y_ref, hr_ref, hi_ref,
                       sr_sc, si_sc, st_re, st_im, *, tc, nb):
    t = pl.program_id(0)
    rows = 2 * nb
    nlb = _SSM_ROW // LANE

    @pl.when(t == 0)
    def _():
        st_re[...] = jnp.zeros_like(st_re)
        st_im[...] = jnp.zeros_like(st_im)

    for b in range(nb):
        for hf in range(2):
            u_hi, u_lo = _split(u_ref[b, :, hf * _SSM_HALF:(hf + 1) * _SSM_HALF])
            bu = _dot3(u_hi, u_lo, bh_ref[hf], bl_ref[hf])
            for k in range(nlb):
                sr_sc[k, pl.ds(b * 2 + hf, tc, stride=rows), :] = bu[:, k * LANE:(k + 1) * LANE]
                si_sc[k, pl.ds(b * 2 + hf, tc, stride=rows), :] = bu[:, _SSM_ROW + k * LANE:_SSM_ROW + (k + 1) * LANE]

    ar = ar_ref[...]
    ai = ai_ref[...]

    def step(i, carry):
        h_re, h_im = carry
        r0 = pl.multiple_of(i * rows, rows)
        n_re = ar * h_re - ai * h_im + sr_sc[:, pl.ds(r0, rows), :]
        n_im = ar * h_im + ai * h_re + si_sc[:, pl.ds(r0, rows), :]
        sr_sc[:, pl.ds(r0, rows), :] = n_re
        si_sc[:, pl.ds(r0, rows), :] = n_im
        return n_re, n_im

    h_re, h_im = lax.fori_loop(0, tc, step, (st_re[...], st_im[...]), unroll=4)
    st_re[...] = h_re
    st_im[...] = h_im
    hr_ref[...] = h_re
    hi_ref[...] = h_im

    for b in range(nb):
        for hf in range(2):
            gather = lambda sc: jnp.concatenate(
                [sc[k, pl.ds(b * 2 + hf, tc, stride=rows), :] for k in range(nlb)], axis=1)
            r_hi, r_lo = _split(gather(sr_sc))
            i_hi, i_lo = _split(gather(si_sc))
            y = (_dot3(r_hi, r_lo, ch_ref[hf, :_SSM_ROW], cl_ref[hf, :_SSM_ROW])
                 + _dot3(i_hi, i_lo, ch_ref[hf, _SSM_ROW:], cl_ref[hf, _SSM_ROW:]))
            cols = slice(hf * _SSM_HALF, (hf + 1) * _SSM_HALF)
            y_ref[b, :, cols] = y + d_ref[:, cols] * u_ref[b, :, cols]


def _ssm_prompt(u, sp, tc=128):
    B, T, W = u.shape
    ar, ai, bh, bl, ch, cl, d = sp
    rows = 2 * B
    nlb = _SSM_ROW // LANE
    tiles = lambda a: jnp.tile(a, (B, 1)).reshape(rows, nlb, LANE).transpose(1, 0, 2)
    const = lambda a: pl.BlockSpec(a.shape, lambda t: (0,) * a.ndim)
    blk = pl.BlockSpec((B, tc, W), lambda t: (0, t, 0))
    st = pl.BlockSpec((nlb, rows, LANE), lambda t: (0, 0, 0))
    ins = [tiles(ar), tiles(ai), bh, bl, ch, cl, d]
    st_shape = jax.ShapeDtypeStruct((nlb, rows, LANE), F32)
    y, h_re, h_im = pl.pallas_call(
        functools.partial(_ssm_prompt_kernel, tc=tc, nb=B),
        grid=(T // tc,),
        in_specs=[blk] + [const(a) for a in ins],
        out_specs=[blk, st, st],
        out_shape=[jax.ShapeDtypeStruct((B, T, W), F32), st_shape, st_shape],
        scratch_shapes=[pltpu.VMEM((nlb, tc * rows, LANE), F32), pltpu.VMEM((nlb, tc * rows, LANE), F32),
                        pltpu.VMEM((nlb, rows, LANE), F32), pltpu.VMEM((nlb, rows, LANE), F32)],
        compiler_params=_params(("arbitrary",)),
        name="ssm_prompt",
    )(u, *ins)
    rows_major = lambda a: a.transpose(1, 0, 2).reshape(rows, _SSM_ROW)
    return y, rows_major(h_re), rows_major(h_im)


def _ssm_sample_kernel(u_ref, h0r_ref, h0i_ref, ar_ref, ai_ref, bh_ref, bl_ref, ch_ref, cl_ref, d_ref,
                       y_ref, hr_ref, hi_ref):
    for hf in range(2):
        cols = slice(hf * _SSM_HALF, (hf + 1) * _SSM_HALF)
        lanes = slice(hf * _SSM_ROW, (hf + 1) * _SSM_ROW)
        u = u_ref[:, cols]
        u_hi, u_lo = _split(u)
        bu = _dot3(u_hi, u_lo, bh_ref[hf], bl_ref[hf])
        ar = ar_ref[hf:hf + 1, :]
        ai = ai_ref[hf:hf + 1, :]
        h_re = ar * h0r_ref[:, lanes] - ai * h0i_ref[:, lanes] + bu[:, :_SSM_ROW]
        h_im = ar * h0i_ref[:, lanes] + ai * h0r_ref[:, lanes] + bu[:, _SSM_ROW:]
        hr_ref[:, lanes] = h_re
        hi_ref[:, lanes] = h_im
        r_hi, r_lo = _split(h_re)
        i_hi, i_lo = _split(h_im)
        y = (_dot3(r_hi, r_lo, ch_ref[hf, :_SSM_ROW], cl_ref[hf, :_SSM_ROW])
             + _dot3(i_hi, i_lo, ch_ref[hf, _SSM_ROW:], cl_ref[hf, _SSM_ROW:]))
        y_ref[:, cols] = y + d_ref[:, cols] * u


def _ssm_sample(u, h0r, h0i, sp):
    n = u.shape[0]
    ins = [u, h0r, h0i, *sp]
    full = lambda a: pl.BlockSpec(a.shape, lambda i: (0,) * a.ndim)
    outs = [jax.ShapeDtypeStruct((n, SSM_WIDTH), F32), jax.ShapeDtypeStruct(h0r.shape, F32),
            jax.ShapeDtypeStruct(h0r.shape, F32)]
    return pl.pallas_call(
        _ssm_sample_kernel, grid=(1,),
        in_specs=[full(a) for a in ins], out_specs=[full(a) for a in outs], out_shape=outs,
        compiler_params=_params(("arbitrary",)),
        name="ssm_sample",
    )(*ins)


def _merge_kernel(x_ref, o_ref, y_ref, gm_ref, wa_ref, ws_ref, wo_ref, gf_ref, y2_ref, hf_ref):
    a = _dot(o_ref[...].astype(BF16), wa_ref[...])
    gl = _dot(jax.nn.gelu(y_ref[...]).astype(BF16), ws_ref[...])
    s = gl[:, :D_MODEL] * jax.nn.sigmoid(gl[:, D_MODEL:])
    gm = gm_ref[...]
    m = jax.nn.sigmoid(gm[:, :D_MODEL]) * a + jax.nn.sigmoid(gm[:, D_MODEL:]) * s
    y2 = x_ref[...] + _dot(m.astype(BF16), wo_ref[...])
    y2_ref[...] = y2
    hf_ref[...] = _rms(y2, gf_ref[...])


def _merge(x, o, y, gm, wa, ws, wo, gf, tm):
    n = x.shape[0]
    tok = lambda w: pl.BlockSpec((tm, w), lambda i: (i, 0))
    const = lambda a: pl.BlockSpec(a.shape, lambda i: (0,) * a.ndim)
    return pl.pallas_call(
        _merge_kernel, grid=(n // tm,),
        in_specs=[tok(D_MODEL), tok(ATT_WIDTH), tok(SSM_WIDTH), tok(2 * D_MODEL), const(wa), const(ws), const(wo),
                  const(gf)],
        out_specs=[tok(D_MODEL), tok(D_MODEL)],
        out_shape=[jax.ShapeDtypeStruct((n, D_MODEL), F32)] * 2,
        compiler_params=_params(("parallel",)),
        name="merge",
    )(x, o, y, gm, wa, ws, wo, gf)


def _cand_pairs():
    return [(a, b) for a in range(PEER_TOPK) for b in range(PEER_TOPK) if (a + 1) * (b + 1) <= PEER_TOPK]


def _top_values(s, k):
    vals = []
    for _ in range(k):
        mx = jnp.max(s, axis=0, keepdims=True)
        vals.append(mx)
        s = jnp.where(s == mx, LOWEST, s)
    return vals


def _peer_route_kernel(hf_ref, wqh_ref, wql_ref, k1h_ref, k1l_ref, k2h_ref, k2l_ref,
                       s1_ref, e1_ref, s2_ref, e2_ref, thr_ref):
    half = PEER_QDIM // 2
    h_hi, h_lo = _split(hf_ref[...])
    q = _dot3(h_hi, h_lo, wqh_ref[...], wql_ref[...])
    q1h, q1l = _split(q[:, :half])
    q2h, q2l = _split(q[:, half:])
    s1 = _dot3_nt(k1h_ref[...], k1l_ref[...], q1h, q1l)
    s2 = _dot3_nt(k2h_ref[...], k2l_ref[...], q2h, q2l)
    v1 = _top_values(s1, PEER_TOPK)
    v2 = _top_values(s2, PEER_TOPK)
    cand = jnp.concatenate([v1[a] + v2[b] for a, b in _cand_pairs()], axis=0)
    thr = _top_values(cand, PEER_TOPK)[-1]
    top = v1[0] + v2[0]
    z = jnp.sum(jnp.where(cand >= thr, jnp.exp(cand - top), 0.0), axis=0, keepdims=True)
    s1_ref[0] = s1
    s2_ref[0] = s2
    e1_ref[0] = jnp.exp(s1 - v1[0])
    e2_ref[0] = jnp.exp(s2 - v2[0]) / z
    thr_ref[0] = thr


def _peer_route(hf, wq_hi, wq_lo, k1, k2, tn):
    n = hf.shape[0]
    k1h, k1l = _split(k1)
    k2h, k2l = _split(k2)
    const = lambda a: pl.BlockSpec(a.shape, lambda i, h: (0, 0))
    wq = pl.BlockSpec((D_MODEL, PEER_QDIM), lambda i, h: (0, h))
    keyed = pl.BlockSpec((1, PEER_NKEYS, tn), lambda i, h: (h, 0, i))
    shp = jax.ShapeDtypeStruct((PEER_HEADS, PEER_NKEYS, n), F32)
    return pl.pallas_call(
        _peer_route_kernel, grid=(n // tn, PEER_HEADS),
        in_specs=[pl.BlockSpec((tn, D_MODEL), lambda i, h: (i, 0)), wq, wq, const(k1h), const(k1l), const(k2h),
                  const(k2l)],
        out_specs=[keyed, keyed, keyed, keyed, pl.BlockSpec((1, 1, tn), lambda i, h: (h, 0, i))],
        out_shape=[shp, shp, shp, shp, jax.ShapeDtypeStruct((PEER_HEADS, 1, n), F32)],
        compiler_params=_params(("parallel", "arbitrary")),
        name="peer_route",
    )(hf, wq_hi, wq_lo, k1h, k1l, k2h, k2l)


def _peer_main_kernel(hf_ref, y2_ref, u_ref, vt_ref, s1_ref, e1_ref, s2_ref, e2_ref, thr_ref, gn_ref, out_ref,
                      acc_sc, w_sc, act_sc, *, tn, n_slab):
    e = pl.program_id(1)

    @pl.when(e == 0)
    def _():
        acc_sc[...] = jnp.zeros_like(acc_sc)

    act_sc[...] = jax.nn.gelu(_dot_nt(u_ref[...], hf_ref[...].astype(BF16)))

    for lc in range(tn // LANE):
        ln = slice(lc * LANE, (lc + 1) * LANE)

        for ii in range(n_slab):
            w = jnp.zeros((PEER_NKEYS, LANE), F32)
            for h in range(PEER_HEADS):
                pair = s1_ref[h, ii:ii + 1, ln] + s2_ref[h, :, ln]
                w = w + e1_ref[h, ii:ii + 1, ln] * jnp.where(pair >= thr_ref[h, :, ln], e2_ref[h, :, ln], 0.0)
            rs = slice(ii * PEER_NKEYS, (ii + 1) * PEER_NKEYS)
            w_sc[rs, ln] = (w * act_sc[rs, ln]).astype(BF16)

    acc_sc[...] += _dot(vt_ref[...], w_sc[...])

    @pl.when(e == pl.num_programs(1) - 1)
    def _():
        out_ref[...] = _rms(y2_ref[...] + acc_sc[...].T, gn_ref[...])


def _peer_main(hf, y2, u_tab, vt_tab, s1, e1, s2, e2, thr, gn, tn, n_slab=8):
    n = hf.shape[0]
    ec = n_slab * PEER_NKEYS
    n_exp = u_tab.shape[0]
    tok = pl.BlockSpec((tn, D_MODEL), lambda i, e: (i, 0))
    slab = pl.BlockSpec((PEER_HEADS, n_slab, tn), lambda i, e: (0, e, i))
    keyed = pl.BlockSpec((PEER_HEADS, PEER_NKEYS, tn), lambda i, e: (0, 0, i))
    return pl.pallas_call(
        functools.partial(_peer_main_kernel, tn=tn, n_slab=n_slab),
        grid=(n // tn, n_exp // ec),
        in_specs=[tok, tok, pl.BlockSpec((ec, D_MODEL), lambda i, e: (e, 0)),
                  pl.BlockSpec((D_MODEL, ec), lambda i, e: (0, e)), slab, slab, keyed, keyed,
                  pl.BlockSpec((PEER_HEADS, 1, tn), lambda i, e: (0, 0, i)),
                  pl.BlockSpec(gn.shape, lambda i, e: (0, 0))],
        out_specs=tok,
        out_shape=jax.ShapeDtypeStruct((n, D_MODEL), F32),
        scratch_shapes=[pltpu.VMEM((D_MODEL, tn), F32), pltpu.VMEM((ec, tn), BF16), pltpu.VMEM((ec, tn), F32)],
        compiler_params=_params(("parallel", "arbitrary")),
        name="peer_main",
    )(hf, y2, u_tab, vt_tab, s1, e1, s2, e2, thr, gn)


def _peer(hf, y2, wq_hi, wq_lo, k1, k2, u_tab, vt_tab, gn, tn):
    s1, e1, s2, e2, thr = _peer_route(hf, wq_hi, wq_lo, k1, k2, tn)
    return _peer_main(hf, y2, u_tab, vt_tab, s1, e1, s2, e2, thr, gn, tn)


def _rope_tables(pos):
    half = HEAD_DIM // 2
    inv = ROPE_THETA ** (-jnp.arange(half, dtype=F32) / half)
    ang = pos.astype(F32)[:, None] * inv[None, :]
    cos, sin = jnp.cos(ang), jnp.sin(ang)
    return jnp.tile(jnp.concatenate([cos, cos], -1), (1, 2)), jnp.tile(jnp.concatenate([-sin, sin], -1), (1, 2))


def _proj_weight(w):
    gates = jnp.pad(w[:, 1280:1304], ((0, 0), (0, LANE - 3 * N_HEADS)))
    return jnp.concatenate([w[:, :1280], gates, w[:, 1304:1816], w[:, 1816:]], axis=1)


def _compress_weights(pe, w1, w2):
    r = CMP_LEN // CMP_STRIDE
    eye = jnp.eye(N_KV, dtype=F32)
    w1r = w1.reshape(r, CMP_STRIDE, HEAD_DIM, w1.shape[-1])
    wc = jnp.einsum('jsdh,gk->sgdjkh', w1r, eye).reshape(CMP_STRIDE * KV_WIDTH, r * N_KV * w1.shape[-1])
    pe2 = jnp.broadcast_to(pe.reshape(r, CMP_STRIDE, 1, HEAD_DIM), (r, CMP_STRIDE, N_KV, HEAD_DIM))
    pe2 = jnp.pad(pe2.reshape(r, CMP_STRIDE * KV_WIDTH), ((0, 8 - r), (0, 0)))
    w2bd = jnp.einsum('hd,gk->ghkd', w2, eye).reshape(N_KV * w2.shape[0], KV_WIDTH)
    return wc.astype(BF16), pe2, w2bd.astype(BF16)


def kernel(x_prompt, x_sample, cache_k_cmp, cache_v_cmp, cache_k_sel, cache_v_sel, cache_k_win, cache_v_win,
           state_ssm_re, state_ssm_im, page_table, norm_mix, w_in, cmp_pe_k, cmp_w1_k, cmp_w2_k, cmp_pe_v, cmp_w1_v,
           cmp_w2_v, ssm_a_re, ssm_a_im, ssm_log_dt, ssm_b_re, ssm_b_im, ssm_c_re, ssm_c_im, ssm_d, w_att_proj,
           w_ssm_glu, w_out, norm_ffn, peer_w_q, peer_keys1, peer_keys2, peer_u, peer_v, norm_final):
    assert w_in.shape[0] == 1, "single layer"
    B, T, _ = x_prompt.shape
    DB = x_sample.shape[0]
    n_pages = page_table.shape[1]
    past = n_pages * PAGE_SIZE
    n_pool = cache_k_cmp.shape[1]

    w_hi, w_lo = _split(_proj_weight(w_in[0]))
    g_mix = norm_mix[0].reshape(1, D_MODEL)
    cw = _compress_weights(cmp_pe_k[0], cmp_w1_k[0], cmp_w2_k[0]) + _compress_weights(cmp_pe_v[0], cmp_w1_v[0],
                                                                                       cmp_w2_v[0])
    sp = _ssm_params(ssm_a_re[0], ssm_a_im[0], ssm_log_dt[0], ssm_b_re[0], ssm_b_im[0], ssm_c_re[0], ssm_c_im[0],
                     ssm_d[0])
    wa, ws, wo = w_att_proj[0].astype(BF16), w_ssm_glu[0].astype(BF16), w_out[0].astype(BF16)
    g_ffn = norm_ffn[0].reshape(1, D_MODEL)
    g_fin = norm_final.reshape(1, D_MODEL)
    wq_hi, wq_lo = _split(peer_w_q[0])
    u_tab = peer_u[0].astype(BF16)
    vt_tab = peer_v[0].T.astype(BF16)

    cos_p, sin_p = _rope_tables(jnp.arange(T, dtype=jnp.int32))
    q, kc, vc, ks, vs, kw, vw, gt, u, gm = _project(x_prompt, g_mix, cos_p, sin_p, [w_hi], 256, False)
    chunks = lambda a: a.reshape(B, T // CMP_STRIDE, CMP_STRIDE * KV_WIDTH)
    ck, cv = _compress_prompt(chunks(kc), chunks(vc), cw)
    o_att = _attn_prompt(q, gt, ck, cv, ks, vs, kw, vw)
    y_ssm, hp_re, hp_im = _ssm_prompt(u, sp)
    n_p = B * T
    flat = lambda a: a.reshape(n_p, a.shape[-1])
    y2_p, hf_p = _merge(flat(x_prompt), flat(o_att), flat(y_ssm), flat(gm), wa, ws, wo, g_ffn, 256)
    y_prompt = _peer(hf_p, y2_p, wq_hi, wq_lo, peer_keys1[0], peer_keys2[0], u_tab, vt_tab, g_fin, 512)

    cos_s, sin_s = _rope_tables(jnp.full((DB,), past, jnp.int32))
    xs = x_sample.reshape(1, DB, D_MODEL)
    qs, kcs, vcs, kss, vss, kws, vws, gts, us, gms = [a[0] for a in
                                                      _project(xs, g_mix, cos_s, sin_s, [w_hi, w_lo], DB, True)]
    row3 = lambda a: a.reshape(DB, 1, KV_WIDTH)
    cpool = lambda c: c[0].reshape(n_pool, PAGE_SIZE // CMP_STRIDE, CMP_STRIDE * KV_WIDTH)
    cks, cvs = _compress_sample(page_table, cpool(cache_k_cmp), cpool(cache_v_cmp), row3(kcs), row3(vcs), cw)
    q5 = qs.reshape(DB, N_KV, Q_PER_KV, 1, HEAD_DIM) * jnp.eye(N_KV, dtype=F32).reshape(1, N_KV, 1, N_KV, 1)
    qbd = q5.reshape(DB, N_HEADS, KV_WIDTH)
    n_blocks = -(-(past + 1) // SEL_BLOCK)
    ns_pad = -(-n_blocks // LANE) * LANE
    oc, sel = _attn_sample_cmp(qbd, cks, cvs, past, n_blocks, ns_pad)
    spool = lambda c: c[0].reshape(n_pool, PAGE_SIZE, KV_WIDTH)
    wrow = lambda c: c[0].reshape(DB, -1, KV_WIDTH)
    gt3 = gts[:, :3 * N_HEADS].reshape(DB, N_HEADS, 3)
    o_s = _attn_sample(page_table, qbd, sel, oc, gt3, row3(kss), row3(vss), wrow(cache_k_win), wrow(cache_v_win),
                       row3(kws), row3(vws), spool(cache_k_sel), spool(cache_v_sel), past)
    h0r = state_ssm_re[0].reshape(DB, SSM_GROUPS * SSM_STATE)
    h0i = state_ssm_im[0].reshape(DB, SSM_GROUPS * SSM_STATE)
    ys_ssm, hs_re, hs_im = _ssm_sample(us, h0r, h0i, sp)
    y2_s, hf_s = _merge(x_sample.reshape(DB, D_MODEL), o_s.reshape(DB, ATT_WIDTH), ys_ssm, gms, wa, ws, wo, g_ffn, DB)
    y_sample = _peer(hf_s, y2_s, wq_hi, wq_lo, peer_keys1[0], peer_keys2[0], u_tab, vt_tab, g_fin, DB)

    kv5 = lambda a, n: a.reshape(1, n, -1, N_KV, HEAD_DIM)
    wb = min(WINDOW, T)
    st = lambda a, n: a.reshape(1, n, SSM_GROUPS, SSM_STATE)
    wbuf = cache_k_win.shape[2]
    nw = min(WINDOW, wbuf + 1)
    win_s = lambda old, new: jnp.concatenate([old[0].reshape(DB, wbuf, KV_WIDTH), new.reshape(DB, 1, KV_WIDTH)],
                                             axis=1)[:, wbuf + 1 - nw:]
    return (y_prompt.reshape(B, T, D_MODEL), y_sample.reshape(DB, 1, D_MODEL),
            kv5(kc, B), kv5(vc, B), kv5(ks, B), kv5(vs, B), kv5(kw[:, T - wb:], B), kv5(vw[:, T - wb:], B),
            st(hp_re, B), st(hp_im, B),
            kv5(kcs, DB), kv5(vcs, DB), kv5(kss, DB), kv5(vss, DB),
            kv5(win_s(cache_k_win, kws), DB), kv5(win_s(cache_v_win, vws), DB),
            st(hs_re, DB), st(hs_im, DB))
```

```python
import functools
import math

import jax
import jax.numpy as jnp
from jax import lax
from jax.experimental import pallas as pl
from jax.experimental.pallas import tpu as pltpu

F32 = jnp.float32
BF16 = jnp.bfloat16

D_MODEL = 1024
HEAD_DIM = 64
N_HEADS = 8
N_KV = 2
Q_PER_KV = 4
ATT_WIDTH = 512
KV_WIDTH = 128
CMP_LEN = 32
CMP_STRIDE = 16
SEL_BLOCK = 64
SEL_TOPN = 16
WINDOW = 512
ROPE_THETA = 10000.0
PAGE_SIZE = 128
SSM_GROUP = 16
SSM_WIDTH = 512
SSM_GROUPS = 32
SSM_STATE = 64
PEER_HEADS = 8
PEER_NKEYS = 128
PEER_QDIM = 256
PEER_TOPK = 16
RMS_EPS = 1e-6
NEG = -1e30
BIG = 1e9
TINY = 1e-30
LOWEST = -3.0e38

LANE = 128
VMEM_LIMIT = 56 * 1024 * 1024

_NT = (((1,), (1,)), ((), ()))


def _params(sem, vmem=VMEM_LIMIT):
    return pltpu.CompilerParams(dimension_semantics=sem, vmem_limit_bytes=vmem)


def _split(x):
    hi = x.astype(BF16)
    lo = (x - hi.astype(F32)).astype(BF16)
    return hi, lo


def _dot(a, b):
    return jnp.dot(a, b, preferred_element_type=F32)


def _dot_nt(a, b):
    return lax.dot_general(a, b, _NT, preferred_element_type=F32)


def _dot3(a_hi, a_lo, b_hi, b_lo):
    return _dot(a_hi, b_hi) + (_dot(a_hi, b_lo) + _dot(a_lo, b_hi))


def _dot3_nt(a_hi, a_lo, b_hi, b_lo):
    return _dot_nt(a_hi, b_hi) + (_dot_nt(a_hi, b_lo) + _dot_nt(a_lo, b_hi))


def _rms(x, g):
    return x * lax.rsqrt(jnp.mean(x * x, axis=-1, keepdims=True) + RMS_EPS) * g


def _softmax_rows(s, mask):
    s = jnp.where(mask, s, NEG)
    e = jnp.where(mask, jnp.exp(s - jnp.max(s, axis=-1, keepdims=True)), 0.0)
    return e / jnp.maximum(jnp.sum(e, axis=-1, keepdims=True), TINY)


_SEG = {'q': (0, 512), 'kc': (512, 640), 'vc': (640, 768), 'ks': (768, 896), 'vs': (896, 1024),
        'kw': (1024, 1152), 'vw': (1152, 1280), 'gt': (1280, 1408), 'u': (1408, 1920), 'gm': (1920, 3968)}
_PROJ_COLS = 3968


def _proj_kernel(*refs, precise):
    x_ref, g_ref, cos_ref, sin_ref = refs[:4]
    n_w = 2 if precise else 1
    w_refs = refs[4:4 + n_w]
    q_ref, kc_ref, vc_ref, ks_ref, vs_ref, kw_ref, vw_ref, gt_ref, u_ref, gm_ref = refs[4 + n_w:]
    h = _rms(x_ref[0], g_ref[...])
    h_hi = h.astype(BF16)
    h_lo = (h - h_hi.astype(F32)).astype(BF16) if precise else None

    def mm(c0, c1):
        z = _dot(h_hi, w_refs[0][:, c0:c1])
        if precise:
            z = z + (_dot(h_hi, w_refs[1][:, c0:c1]) + _dot(h_lo, w_refs[0][:, c0:c1]))
        return z

    cos = cos_ref[...]
    sin = sin_ref[...]
    first = (lax.broadcasted_iota(jnp.int32, (1, LANE), 1) % HEAD_DIM) < (HEAD_DIM // 2)

    def rope(z):
        rot = jnp.where(first, pltpu.roll(z, LANE - HEAD_DIM // 2, 1), pltpu.roll(z, HEAD_DIM // 2, 1))
        return z * cos + rot * sin

    for i in range(4):
        q_ref[0, :, i * LANE:(i + 1) * LANE] = rope(mm(i * LANE, (i + 1) * LANE))
    kc_ref[0] = rope(mm(*_SEG['kc']))
    vc_ref[0] = mm(*_SEG['vc'])
    ks_ref[0] = rope(mm(*_SEG['ks']))
    vs_ref[0] = mm(*_SEG['vs'])
    kw_ref[0] = rope(mm(*_SEG['kw']))
    vw_ref[0] = mm(*_SEG['vw'])
    gt_ref[0] = mm(*_SEG['gt'])
    u_ref[0] = mm(*_SEG['u'])
    for i in range(4):
        c0 = _SEG['gm'][0] + i * 512
        gm_ref[0, :, i * 512:(i + 1) * 512] = mm(c0, c0 + 512)


def _project(x, gain, cos, sin, w_list, tm, precise):
    B, T, D = x.shape
    widths = [512, 128, 128, 128, 128, 128, 128, 128, 512, 2048]
    tok = lambda w: pl.BlockSpec((1, tm, w), lambda b, t: (b, t, 0))
    const = lambda a: pl.BlockSpec(a.shape, lambda b, t: (0,) * a.ndim)
    return pl.pallas_call(
        functools.partial(_proj_kernel, precise=precise),
        grid=(B, T // tm),
        in_specs=[tok(D), const(gain), pl.BlockSpec((tm, LANE), lambda b, t: (t, 0)),
                  pl.BlockSpec((tm, LANE), lambda b, t: (t, 0))] + [const(w) for w in w_list],
        out_specs=[tok(w) for w in widths],
        out_shape=[jax.ShapeDtypeStruct((B, T, w), F32) for w in widths],
        compiler_params=_params(("parallel", "arbitrary")),
        name="proj",
    )(x, gain, cos, sin, *w_list)


def _compress_rows(x, last_p1, wc_ref, pe_ref, w2_ref):
    C = x.shape[0]
    p = _dot(x.astype(BF16), wc_ref[...])
    pb = _dot(pe_ref[...].astype(BF16), wc_ref[...])
    bias = pb[0:1, :LANE] + pb[1:2, LANE:]
    p1 = pltpu.roll(p[:, LANE:], C - 1, 0)
    if last_p1 is not None:
        row = lax.broadcasted_iota(jnp.int32, (C, 1), 0)
        p1 = jnp.where(row == C - 1, last_p1, p1)
    hid = p[:, :LANE] + p1 + bias
    return _dot(jax.nn.gelu(hid).astype(BF16), w2_ref[...])


def _compress_prompt_kernel(xk_ref, xv_ref, wck_ref, pek_ref, w2k_ref, wcv_ref, pev_ref, w2v_ref, ck_ref, cv_ref):
    ck_ref[0] = _compress_rows(xk_ref[0], None, wck_ref, pek_ref, w2k_ref)
    cv_ref[0] = _compress_rows(xv_ref[0], None, wcv_ref, pev_ref, w2v_ref)


def _compress_prompt(xk, xv, cw):
    B, C, W = xk.shape
    seq = pl.BlockSpec((1, C, W), lambda b: (b, 0, 0))
    const = lambda a: pl.BlockSpec(a.shape, lambda b: (0,) * a.ndim)
    out = pl.BlockSpec((1, C, LANE), lambda b: (b, 0, 0))
    return pl.pallas_call(
        _compress_prompt_kernel,
        grid=(B,),
        in_specs=[seq, seq] + [const(a) for a in cw],
        out_specs=[out, out],
        out_shape=[jax.ShapeDtypeStruct((B, C, LANE), F32)] * 2,
        compiler_params=_params(("arbitrary",)),
        name="compress_prompt",
    )(xk, xv, *cw)


def _paged_fetch(pt_ref, pools, bufs, sems, n_pages, dst):
    b = pl.program_id(0)
    nb = pl.num_programs(0)
    slot = b % 2

    def copies(bb, sl):
        return [pltpu.make_async_copy(pool.at[pt_ref[bb, p]], dst(buf, sl, p), sem.at[sl])
                for pool, buf, sem in zip(pools, bufs, sems) for p in range(n_pages)]

    @pl.when(b == 0)
    def _():
        for cp in copies(b, slot):
            cp.start()

    @pl.when(b + 1 < nb)
    def _():
        for cp in copies(b + 1, 1 - slot):
            cp.start()

    for cp in copies(b, slot):
        cp.wait()
    return slot


def _compress_sample_kernel(pt_ref, kn_ref, vn_ref, wck_ref, pek_ref, w2k_ref, wcv_ref, pev_ref, w2v_ref,
                            kpool_ref, vpool_ref, ck_ref, cv_ref, kbuf, vbuf, ksem, vsem, rows_sc, *, n_pages):
    slot = _paged_fetch(pt_ref, (kpool_ref, vpool_ref), (kbuf, vbuf), (ksem, vsem), n_pages,
                        lambda buf, sl, p: buf.at[sl, p])
    n_chunks = n_pages * PAGE_SIZE // CMP_STRIDE

    def one(buf, new_ref, wc_ref, pe_ref, w2_ref, out_ref):
        def flip(p, carry):
            rows_sc[pl.ds(pl.multiple_of(p * PAGE_SIZE, PAGE_SIZE), PAGE_SIZE), :] = buf[slot, p].T
            return carry

        lax.fori_loop(0, n_pages, flip, 0)
        x = jnp.concatenate([rows_sc[pl.ds(s, n_chunks, stride=CMP_STRIDE), :] for s in range(CMP_STRIDE)], axis=1)
        new = jnp.broadcast_to(new_ref[0], (8, LANE)).astype(BF16)
        last_p1 = _dot(new, wc_ref[0:LANE, LANE:])[0:1]
        out_ref[0] = _compress_rows(x, last_p1, wc_ref, pe_ref, w2_ref)

    one(kbuf, kn_ref, wck_ref, pek_ref, w2k_ref, ck_ref)
    one(vbuf, vn_ref, wcv_ref, pev_ref, w2v_ref, cv_ref)


def _compress_sample(page_table, kpool, vpool, k_new, v_new, cw):
    DB, n_pages = page_table.shape
    C = n_pages * PAGE_SIZE // CMP_STRIDE
    new = pl.BlockSpec((1, 1, LANE), lambda b, pt: (b, 0, 0))
    const = lambda a: pl.BlockSpec(a.shape, lambda b, pt: (0,) * a.ndim)
    hbm = pl.BlockSpec(memory_space=pl.ANY)
    out = pl.BlockSpec((1, C, LANE), lambda b, pt: (b, 0, 0))
    page_buf = pltpu.VMEM((2, n_pages, KV_WIDTH, PAGE_SIZE), F32)
    return pl.pallas_call(
        functools.partial(_compress_sample_kernel, n_pages=n_pages),
        grid_spec=pltpu.PrefetchScalarGridSpec(
            num_scalar_prefetch=1, grid=(DB,),
            in_specs=[new, new] + [const(a) for a in cw] + [hbm, hbm],
            out_specs=[out, out],
            scratch_shapes=[page_buf, page_buf, pltpu.SemaphoreType.DMA((2,)), pltpu.SemaphoreType.DMA((2,)),
                            pltpu.VMEM((n_pages * PAGE_SIZE, KV_WIDTH), F32)]),
        out_shape=[jax.ShapeDtypeStruct((DB, C, LANE), F32)] * 2,
        compiler_params=_params(("arbitrary",)),
        name="compress_sample",
    )(page_table, k_new, v_new, *cw, kpool, vpool)


def _select_blocks(imp, qpos, n_real, axis=1):
    j_shape = (1, imp.shape[1]) if axis == 1 else (imp.shape[0], 1)
    j = lax.broadcasted_iota(jnp.int32, j_shape, axis)
    valid = j * SEL_BLOCK <= qpos
    cur = qpos // SEL_BLOCK
    force = (j == 0) | (j == cur) | (j == cur - 1)
    score = jnp.where(valid & force, BIG, jnp.where(valid, imp, -BIG))
    rank = jnp.zeros(imp.shape, F32)
    for jp in range(n_real):
        other = score[:, jp:jp + 1] if axis == 1 else score[jp:jp + 1, :]
        earlier = jnp.where(j > jp, 1.0, 0.0)
        rank = rank + jnp.where(other > score, 1.0, jnp.where(other == score, earlier, 0.0))
    return jnp.where(rank < SEL_TOPN, 1.0, 0.0)


def _attn_prompt_kernel(q_ref, gt_ref, ck_ref, cv_ref, ks_ref, vs_ref, kw_ref, vw_ref, msel_ref, exp_ref, o_ref,
                        m_sc, acc_sc, *, tq, kc, rb, wb):
    s0 = pl.program_id(1) * tq
    R = Q_PER_KV * tq
    ncb = ck_ref.shape[1]
    qpos_r = s0 + lax.broadcasted_iota(jnp.int32, (R, 1), 0) % tq
    qpos_t = s0 + lax.broadcasted_iota(jnp.int32, (tq, 1), 0)
    qpos_l = s0 + lax.broadcasted_iota(jnp.int32, (1, tq), 1)
    sig = jax.nn.sigmoid(gt_ref[0])
    stack = lambda f: jnp.concatenate([f(r) for r in range(Q_PER_KV)], axis=0)

    def rows_of(bias, r0, n):
        if n >= tq:
            return jnp.concatenate([bias] * (n // tq), axis=0)
        return bias[r0 % tq:r0 % tq + n]

    heads = []
    for g in range(N_KV):
        gl = slice(g * HEAD_DIM, (g + 1) * HEAD_DIM)
        qg = stack(lambda r: q_ref[0, :, (g * Q_PER_KV + r) * HEAD_DIM:(g * Q_PER_KV + r + 1) * HEAD_DIM])
        qg = qg * (HEAD_DIM ** -0.5)
        q_hi, q_lo = _split(qg)

        k_hi, k_lo = _split(ck_ref[0, :, gl])
        s_c = _dot3_nt(q_hi, q_lo, k_hi, k_lo)
        cend = lax.broadcasted_iota(jnp.int32, (1, ncb), 1) * CMP_STRIDE + (CMP_LEN - 1)
        p_c = _softmax_rows(s_c, cend <= qpos_r)
        o_c = _dot(p_c.astype(BF16), cv_ref[0, :, gl].astype(BF16))
        p_sum = p_c[0:tq]
        for r in range(1, Q_PER_KV):
            p_sum = p_sum + p_c[r * tq:(r + 1) * tq]
        ps_hi, ps_lo = _split(p_sum)
        msel_t = msel_ref[...]
        imp_t = _dot_nt(msel_t, ps_hi) + _dot_nt(msel_t, ps_lo)
        sel = _select_blocks(imp_t, qpos_l, imp_t.shape[0], axis=0).T

        unpicked = ((sel - 1.0) * -NEG).astype(BF16)
        m_sc[...] = jnp.full((R, 1), NEG, F32)
        acc_sc[...] = jnp.zeros((R, KV_WIDTH), F32)
        own = (lax.broadcasted_iota(jnp.int32, (1, KV_WIDTH), 1) // HEAD_DIM) == g
        ol = slice((1 - g) * HEAD_DIM, (1 - g) * HEAD_DIM + 1)
        with_ones = lambda v: jnp.where(own, v, 1.0).astype(BF16)

        def chunk(c, carry):
            off = pl.multiple_of(c * kc, kc)
            k = ks_ref[0, pl.ds(off, kc), gl].astype(BF16)
            v = with_ones(vs_ref[0, pl.ds(off, kc), :])
            kpos = off + lax.broadcasted_iota(jnp.int32, (1, kc), 1)
            bias = _dot(unpicked, exp_ref[c]) + jnp.where(kpos <= qpos_t, 0.0, NEG)
            for r0 in range(0, R, rb):
                rs = slice(r0, r0 + rb)
                s = _dot_nt(q_hi[rs], k) + rows_of(bias, r0, rb)
                m_old = m_sc[rs]
                m_new = jnp.maximum(m_old, jnp.max(s, axis=-1, keepdims=True))
                p = jnp.exp(s - m_new)
                acc_sc[rs] = jnp.exp(m_old - m_new) * acc_sc[rs] + _dot(p.astype(BF16), v)
                m_sc[rs] = m_new
            return carry

        lax.fori_loop(0, (s0 + tq + kc - 1) // kc, chunk, 0)
        o_s = acc_sc[:, gl] / jnp.maximum(acc_sc[:, ol], TINY)

        wl = WINDOW + tq
        w0 = pl.multiple_of(jnp.maximum(s0 - WINDOW, 0), tq)
        kpos = w0 + lax.broadcasted_iota(jnp.int32, (1, wl), 1)
        dlt = qpos_t - kpos
        bias_w = jnp.where((dlt >= 0) & (dlt < WINDOW), 0.0, NEG)
        k_w = kw_ref[0, pl.ds(w0, wl), gl].astype(BF16)
        v_w = with_ones(vw_ref[0, pl.ds(w0, wl), :])
        o_w = []
        for r0 in range(0, R, wb):
            s_w = _dot_nt(q_hi[r0:r0 + wb], k_w) + rows_of(bias_w, r0, wb)
            p_w = jnp.exp(s_w - jnp.max(s_w, axis=-1, keepdims=True))
            ov = _dot(p_w.astype(BF16), v_w)
            o_w.append(ov[:, gl] / jnp.maximum(ov[:, ol], TINY))
        o_w = jnp.concatenate(o_w, axis=0)

        gate = lambda i: stack(lambda r: sig[:, (g * Q_PER_KV + r) * 3 + i:(g * Q_PER_KV + r) * 3 + i + 1])
        og = gate(0) * o_c + gate(1) * o_s + gate(2) * o_w
        heads += [og[r * tq:(r + 1) * tq] for r in range(Q_PER_KV)]
    o_ref[0] = jnp.concatenate(heads, axis=1)


def _attn_prompt(q, gt, ck, cv, ks, vs, kw, vw, tq=128, kc=512, rb=512, wb=128):
    B, T, _ = q.shape
    assert T % kc == 0 and T % tq == 0 and T >= WINDOW + tq and (tq % rb == 0 or rb % tq == 0)
    ncb = ck.shape[1]
    ns = T // SEL_BLOCK
    i = jnp.arange(ncb)[None, :]
    j = jnp.arange(ns)[:, None]
    msel = ((i * CMP_STRIDE <= j * SEL_BLOCK + SEL_BLOCK - 1)
            & (i * CMP_STRIDE + CMP_LEN - 1 >= j * SEL_BLOCK)).astype(BF16)
    key_blk = (jnp.arange(T) // SEL_BLOCK).reshape(T // kc, 1, kc)
    expand = (key_blk == jnp.arange(ns)[None, :, None]).astype(BF16)
    tok = lambda w: pl.BlockSpec((1, tq, w), lambda b, t: (b, t, 0))
    seq = lambda a: pl.BlockSpec((1,) + a.shape[1:], lambda b, t: (b, 0, 0))
    const = lambda a: pl.BlockSpec(a.shape, lambda b, t: (0,) * a.ndim)
    R = Q_PER_KV * tq
    return pl.pallas_call(
        functools.partial(_attn_prompt_kernel, tq=tq, kc=kc, rb=rb, wb=wb),
        grid=(B, T // tq),
        in_specs=[tok(ATT_WIDTH), tok(LANE), seq(ck), seq(cv), seq(ks), seq(vs), seq(kw), seq(vw),
                  const(msel), const(expand)],
        out_specs=tok(ATT_WIDTH),
        out_shape=jax.ShapeDtypeStruct((B, T, ATT_WIDTH), F32),
        scratch_shapes=[pltpu.VMEM((R, 1), F32), pltpu.VMEM((R, KV_WIDTH), F32)],
        compiler_params=_params(("parallel", "arbitrary")),
        name="attn_prompt",
    )(q, gt, ck, cv, ks, vs, kw, vw, msel, expand)


def _group_lanes():
    row = lax.broadcasted_iota(jnp.int32, (N_HEADS, LANE), 0) // Q_PER_KV
    lane = lax.broadcasted_iota(jnp.int32, (N_HEADS, LANE), 1) // HEAD_DIM
    return row == lane


def _attn_sample_cmp_kernel(q_ref, ck_ref, cv_ref, msel_ref, oc_ref, sel_ref, *, qpos, n_blocks):
    q_hi, q_lo = _split(q_ref[0] * (HEAD_DIM ** -0.5))
    k_hi, k_lo = _split(ck_ref[0])
    s_c = _dot3_nt(q_hi, q_lo, k_hi, k_lo)
    ncb = s_c.shape[1]
    cend = lax.broadcasted_iota(jnp.int32, (1, ncb), 1) * CMP_STRIDE + (CMP_LEN - 1)
    p_c = _softmax_rows(s_c, cend <= qpos)
    o_c = _dot(p_c.astype(BF16), cv_ref[0].astype(BF16))
    oc_ref[0] = jnp.where(_group_lanes(), o_c, 0.0)
    top = jnp.sum(p_c[0:Q_PER_KV], axis=0, keepdims=True)
    bot = jnp.sum(p_c[Q_PER_KV:], axis=0, keepdims=True)
    row = lax.broadcasted_iota(jnp.int32, (N_HEADS, 1), 0)
    p_sum = jnp.where(row < Q_PER_KV, top, bot)
    ps_hi, ps_lo = _split(p_sum)
    msel = msel_ref[...]
    imp = _dot(ps_hi, msel) + _dot(ps_lo, msel)
    sel_ref[0] = _select_blocks(imp, jnp.full((N_HEADS, 1), qpos, jnp.int32), n_blocks)


def _attn_sample_cmp(qbd, ck, cv, qpos, n_blocks, ns_pad):
    DB, ncb, _ = ck.shape
    i = jnp.arange(ncb)[:, None]
    j = jnp.arange(ns_pad)[None, :]
    msel = ((i * CMP_STRIDE <= j * SEL_BLOCK + SEL_BLOCK - 1) & (i * CMP_STRIDE + CMP_LEN - 1 >= j * SEL_BLOCK)
            & (j < n_blocks)).astype(BF16)
    row = lambda a: pl.BlockSpec((1,) + a.shape[1:], lambda b: (b, 0, 0))
    return pl.pallas_call(
        functools.partial(_attn_sample_cmp_kernel, qpos=qpos, n_blocks=n_blocks),
        grid=(DB,),
        in_specs=[row(qbd), row(ck), row(cv), pl.BlockSpec(msel.shape, lambda b: (0, 0))],
        out_specs=[pl.BlockSpec((1, N_HEADS, LANE), lambda b: (b, 0, 0)),
                   pl.BlockSpec((1, N_HEADS, ns_pad), lambda b: (b, 0, 0))],
        out_shape=[jax.ShapeDtypeStruct((DB, N_HEADS, LANE), F32), jax.ShapeDtypeStruct((DB, N_HEADS, ns_pad), F32)],
        compiler_params=_params(("arbitrary",)),
        name="attn_sample_cmp",
    )(qbd, ck, cv, msel)


def _attn_sample_kernel(pt_ref, q_ref, sel_ref, oc_ref, gt_ref, ksn_ref, vsn_ref, kwb_ref, vwb_ref, kwn_ref, vwn_ref,
                        exp_ref, kpool_ref, vpool_ref, o_ref, kbuf, vbuf, ksem, vsem, *, n_pages, qpos):
    slot = _paged_fetch(pt_ref, (kpool_ref, vpool_ref), (kbuf, vbuf), (ksem, vsem), n_pages,
                        lambda buf, sl, p: buf.at[sl, :, pl.ds(p * PAGE_SIZE, PAGE_SIZE)])
    past = n_pages * PAGE_SIZE
    q = q_ref[0] * (HEAD_DIM ** -0.5)
    q_b = q.astype(BF16)
    sel = sel_ref[0]

    def attend(kt_old, vt_old, mask_old, k_new, v_new, mask_new):
        s = jnp.where(mask_old, _dot(q_b, kt_old.astype(BF16)), NEG)
        s_n = jnp.where(mask_new, jnp.sum(q * k_new, axis=-1, keepdims=True), NEG)
        m = jnp.maximum(jnp.max(s, axis=-1, keepdims=True), s_n)
        e = jnp.where(mask_old, jnp.exp(s - m), 0.0)
        e_n = jnp.where(mask_new, jnp.exp(s_n - m), 0.0)
        den = jnp.maximum(jnp.sum(e, axis=-1, keepdims=True) + e_n, TINY)
        return (_dot_nt(e.astype(BF16), vt_old.astype(BF16)) + e_n * v_new) / den

    picked = _dot(sel.astype(BF16), exp_ref[...])
    kpos = lax.broadcasted_iota(jnp.int32, (1, past), 1)
    nb_new = past // SEL_BLOCK
    o_s = attend(kbuf[slot], vbuf[slot], (picked > 0.5) & (kpos <= qpos), ksn_ref[0], vsn_ref[0],
                 (sel[:, nb_new:nb_new + 1] > 0.5) & (past <= qpos))
    wbuf = kwb_ref.shape[2]
    dlt = qpos - (past - wbuf + lax.broadcasted_iota(jnp.int32, (1, wbuf), 1))
    o_w = attend(kwb_ref[0], vwb_ref[0], (dlt >= 0) & (dlt < WINDOW), kwn_ref[0], vwn_ref[0],
                 jnp.full((N_HEADS, 1), (qpos - past >= 0) & (qpos - past < WINDOW)))
    sig = jax.nn.sigmoid(gt_ref[0])
    o = sig[:, 0:1] * oc_ref[0] + sig[:, 1:2] * o_s + sig[:, 2:3] * o_w
    o = jnp.where(_group_lanes(), o, 0.0)
    o_ref[0] = o[:, :HEAD_DIM] + o[:, HEAD_DIM:]


def _attn_sample(page_table, qbd, sel, oc, gt3, ksn, vsn, kwb, vwb, kwn, vwn, kpool, vpool, qpos):
    DB, n_pages = page_table.shape
    past = n_pages * PAGE_SIZE
    nsp = sel.shape[-1]
    expand = ((jnp.arange(past) // SEL_BLOCK)[None, :] == jnp.arange(nsp)[:, None]).astype(BF16)
    row = lambda a: pl.BlockSpec((1,) + a.shape[1:], lambda b, pt: (b, 0, 0))
    hbm = pl.BlockSpec(memory_space=pl.ANY)
    ins = [qbd, sel, oc, gt3, ksn, vsn, kwb, vwb, kwn, vwn]
    return pl.pallas_call(
        functools.partial(_attn_sample_kernel, n_pages=n_pages, qpos=qpos),
        grid_spec=pltpu.PrefetchScalarGridSpec(
            num_scalar_prefetch=1, grid=(DB,),
            in_specs=[row(a) for a in ins] + [pl.BlockSpec(expand.shape, lambda b, pt: (0, 0)), hbm, hbm],
            out_specs=pl.BlockSpec((1, N_HEADS, HEAD_DIM), lambda b, pt: (b, 0, 0)),
            scratch_shapes=[pltpu.VMEM((2, KV_WIDTH, past), F32), pltpu.VMEM((2, KV_WIDTH, past), F32),
                            pltpu.SemaphoreType.DMA((2,)), pltpu.SemaphoreType.DMA((2,))]),
        out_shape=jax.ShapeDtypeStruct((DB, N_HEADS, HEAD_DIM), F32),
        compiler_params=_params(("arbitrary",)),
        name="attn_sample",
    )(page_table, *ins, expand, kpool, vpool)


_SSM_HALF = SSM_WIDTH // 2
_SSM_ROW = SSM_GROUPS // 2 * SSM_STATE


def _ssm_params(a_re, a_im, log_dt, b_re, b_im, c_re, c_im, d):
    lam = lax.complex(a_re, a_im)
    step = jnp.exp(log_dt)[:, None]
    a_bar = jnp.exp(lam * step)
    b_bar = ((a_bar - 1.0) / lam)[..., None] * lax.complex(b_re, b_im)
    eye = jnp.eye(SSM_GROUPS // 2, dtype=F32)

    def b_mat(x):
        x = x.reshape(2, SSM_GROUPS // 2, SSM_STATE, SSM_GROUP)
        return jnp.einsum('hgpc,gk->hgckp', x, eye).reshape(2, _SSM_HALF, _SSM_ROW)

    def c_mat(x):
        x = x.reshape(2, SSM_GROUPS // 2, SSM_GROUP, SSM_STATE)
        return jnp.einsum('hgcp,gk->hgpkc', x, eye).reshape(2, _SSM_ROW, _SSM_HALF)

    bm = jnp.concatenate([b_mat(jnp.real(b_bar)), b_mat(jnp.imag(b_bar))], axis=2)
    cm = jnp.concatenate([c_mat(c_re), -c_mat(c_im)], axis=1)
    ar = jnp.real(a_bar).reshape(2, _SSM_ROW)
    ai = jnp.imag(a_bar).reshape(2, _SSM_ROW)
    bm_hi, bm_lo = _split(bm)
    cm_hi, cm_lo = _split(cm)
    return ar, ai, bm_hi, bm_lo, cm_hi, cm_lo, d.reshape(1, SSM_WIDTH)


def _ssm_prompt_kernel(u_ref, ar_ref, ai_ref, bh_ref, bl_ref, ch_ref, cl_ref, d_ref, y_ref, hr_ref, hi_ref,
                       sr_sc, si_sc, st_re, st_im, *, tc, nb):
    t = pl.program_id(0)
    rows = 2 * nb
    nlb = _SSM_ROW // LANE

    @pl.when(t == 0)
    def _():
        st_re[...] = jnp.zeros_like(st_re)
        st_im[...] = jnp.zeros_like(st_im)

    for b in range(nb):
        for hf in range(2):
            u_hi, u_lo = _split(u_ref[b, :, hf * _SSM_HALF:(hf + 1) * _SSM_HALF])
            bu = _dot3(u_hi, u_lo, bh_ref[hf], bl_ref[hf])
            for k in range(nlb):
                sr_sc[k, pl.ds(b * 2 + hf, tc, stride=rows), :] = bu[:, k * LANE:(k + 1) * LANE]
                si_sc[k, pl.ds(b * 2 + hf, tc, stride=rows), :] = bu[:, _SSM_ROW + k * LANE:_SSM_ROW + (k + 1) * LANE]

    ar = ar_ref[...]
    ai = ai_ref[...]

    def step(i, carry):
        h_re, h_im = carry
        r0 = pl.multiple_of(i * rows, rows)
        n_re = ar * h_re - ai * h_im + sr_sc[:, pl.ds(r0, rows), :]
        n_im = ar * h_im + ai * h_re + si_sc[:, pl.ds(r0, rows), :]
        sr_sc[:, pl.ds(r0, rows), :] = n_re
        si_sc[:, pl.ds(r0, rows), :] = n_im
        return n_re, n_im

    h_re, h_im = lax.fori_loop(0, tc, step, (st_re[...], st_im[...]), unroll=4)
    st_re[...] = h_re
    st_im[...] = h_im
    hr_ref[...] = h_re
    hi_ref[...] = h_im

    for b in range(nb):
        for hf in range(2):
            gather = lambda sc: jnp.concatenate(
                [sc[k, pl.ds(b * 2 + hf, tc, stride=rows), :] for k in range(nlb)], axis=1)
            r_hi, r_lo = _split(gather(sr_sc))
            i_hi, i_lo = _split(gather(si_sc))
            y = (_dot3(r_hi, r_lo, ch_ref[hf, :_SSM_ROW], cl_ref[hf, :_SSM_ROW])
                 + _dot3(i_hi, i_lo, ch_ref[hf, _SSM_ROW:], cl_ref[hf, _SSM_ROW:]))
            cols = slice(hf * _SSM_HALF, (hf + 1) * _SSM_HALF)
            y_ref[b, :, cols] = y + d_ref[:, cols] * u_ref[b, :, cols]


def _ssm_prompt(u, sp, tc=128):
    B, T, W = u.shape
    ar, ai, bh, bl, ch, cl, d = sp
    rows = 2 * B
    nlb = _SSM_ROW // LANE
    tiles = lambda a: jnp.tile(a, (B, 1)).reshape(rows, nlb, LANE).transpose(1, 0, 2)
    const = lambda a: pl.BlockSpec(a.shape, lambda t: (0,) * a.ndim)
    blk = pl.BlockSpec((B, tc, W), lambda t: (0, t, 0))
    st = pl.BlockSpec((nlb, rows, LANE), lambda t: (0, 0, 0))
    ins = [tiles(ar), tiles(ai), bh, bl, ch, cl, d]
    st_shape = jax.ShapeDtypeStruct((nlb, rows, LANE), F32)
    y, h_re, h_im = pl.pallas_call(
        functools.partial(_ssm_prompt_kernel, tc=tc, nb=B),
        grid=(T // tc,),
        in_specs=[blk] + [const(a) for a in ins],
        out_specs=[blk, st, st],
        out_shape=[jax.ShapeDtypeStruct((B, T, W), F32), st_shape, st_shape],
        scratch_shapes=[pltpu.VMEM((nlb, tc * rows, LANE), F32), pltpu.VMEM((nlb, tc * rows, LANE), F32),
                        pltpu.VMEM((nlb, rows, LANE), F32), pltpu.VMEM((nlb, rows, LANE), F32)],
        compiler_params=_params(("arbitrary",)),
        name="ssm_prompt",
    )(u, *ins)
    rows_major = lambda a: a.transpose(1, 0, 2).reshape(rows, _SSM_ROW)
    return y, rows_major(h_re), rows_major(h_im)


def _ssm_sample_kernel(u_ref, h0r_ref, h0i_ref, ar_ref, ai_ref, bh_ref, bl_ref, ch_ref, cl_ref, d_ref,
                       y_ref, hr_ref, hi_ref):
    for hf in range(2):
        cols = slice(hf * _SSM_HALF, (hf + 1) * _SSM_HALF)
        lanes = slice(hf * _SSM_ROW, (hf + 1) * _SSM_ROW)
        u = u_ref[:, cols]
        u_hi, u_lo = _split(u)
        bu = _dot3(u_hi, u_lo, bh_ref[hf], bl_ref[hf])
        ar = ar_ref[hf:hf + 1, :]
        ai = ai_ref[hf:hf + 1, :]
        h_re = ar * h0r_ref[:, lanes] - ai * h0i_ref[:, lanes] + bu[:, :_SSM_ROW]
        h_im = ar * h0i_ref[:, lanes] + ai * h0r_ref[:, lanes] + bu[:, _SSM_ROW:]
        hr_ref[:, lanes] = h_re
        hi_ref[:, lanes] = h_im
        r_hi, r_lo = _split(h_re)
        i_hi, i_lo = _split(h_im)
        y = (_dot3(r_hi, r_lo, ch_ref[hf, :_SSM_ROW], cl_ref[hf, :_SSM_ROW])
             + _dot3(i_hi, i_lo, ch_ref[hf, _SSM_ROW:], cl_ref[hf, _SSM_ROW:]))
        y_ref[:, cols] = y + d_ref[:, cols] * u


def _ssm_sample(u, h0r, h0i, sp):
    n = u.shape[0]
    ins = [u, h0r, h0i, *sp]
    full = lambda a: pl.BlockSpec(a.shape, lambda i: (0,) * a.ndim)
    outs = [jax.ShapeDtypeStruct((n, SSM_WIDTH), F32), jax.ShapeDtypeStruct(h0r.shape, F32),
            jax.ShapeDtypeStruct(h0r.shape, F32)]
    return pl.pallas_call(
        _ssm_sample_kernel, grid=(1,),
        in_specs=[full(a) for a in ins], out_specs=[full(a) for a in outs], out_shape=outs,
        compiler_params=_params(("arbitrary",)),
        name="ssm_sample",
    )(*ins)


def _merge_kernel(x_ref, o_ref, y_ref, gm_ref, wa_ref, ws_ref, wo_ref, gf_ref, y2_ref, hf_ref):
    a = _dot(o_ref[...].astype(BF16), wa_ref[...])
    gl = _dot(jax.nn.gelu(y_ref[...]).astype(BF16), ws_ref[...])
    s = gl[:, :D_MODEL] * jax.nn.sigmoid(gl[:, D_MODEL:])
    gm = gm_ref[...]
    m = jax.nn.sigmoid(gm[:, :D_MODEL]) * a + jax.nn.sigmoid(gm[:, D_MODEL:]) * s
    y2 = x_ref[...] + _dot(m.astype(BF16), wo_ref[...])
    y2_ref[...] = y2
    hf_ref[...] = _rms(y2, gf_ref[...])


def _merge(x, o, y, gm, wa, ws, wo, gf, tm):
    n = x.shape[0]
    tok = lambda w: pl.BlockSpec((tm, w), lambda i: (i, 0))
    const = lambda a: pl.BlockSpec(a.shape, lambda i: (0,) * a.ndim)
    return pl.pallas_call(
        _merge_kernel, grid=(n // tm,),
        in_specs=[tok(D_MODEL), tok(ATT_WIDTH), tok(SSM_WIDTH), tok(2 * D_MODEL), const(wa), const(ws), const(wo),
                  const(gf)],
        out_specs=[tok(D_MODEL), tok(D_MODEL)],
        out_shape=[jax.ShapeDtypeStruct((n, D_MODEL), F32)] * 2,
        compiler_params=_params(("parallel",)),
        name="merge",
    )(x, o, y, gm, wa, ws, wo, gf)


def _cand_pairs():
    return [(a, b) for a in range(PEER_TOPK) for b in range(PEER_TOPK) if (a + 1) * (b + 1) <= PEER_TOPK]


def _top_values(s, k):
    vals = []
    for _ in range(k):
        mx = jnp.max(s, axis=0, keepdims=True)
        vals.append(mx)
        s = jnp.where(s == mx, LOWEST, s)
    return vals


def _peer_route_kernel(hf_ref, wqh_ref, wql_ref, k1h_ref, k1l_ref, k2h_ref, k2l_ref,
                       c1_ref, e1_ref, s2_ref, e2_ref):
    half = PEER_QDIM // 2
    h_hi, h_lo = _split(hf_ref[...])
    q = _dot3(h_hi, h_lo, wqh_ref[...], wql_ref[...])
    q1h, q1l = _split(q[:, :half])
    q2h, q2l = _split(q[:, half:])
    s1 = _dot3_nt(k1h_ref[...], k1l_ref[...], q1h, q1l)
    s2 = _dot3_nt(k2h_ref[...], k2l_ref[...], q2h, q2l)
    v1 = _top_values(s1, PEER_TOPK)
    v2 = _top_values(s2, PEER_TOPK)
    cand = jnp.concatenate([v1[a] + v2[b] for a, b in _cand_pairs()], axis=0)
    thr = _top_values(cand, PEER_TOPK)[-1]
    top = v1[0] + v2[0]
    z = jnp.sum(jnp.where(cand >= thr, jnp.exp(cand - top), 0.0), axis=0, keepdims=True)
    c1 = jnp.full(s1.shape, -LOWEST, F32)
    for v in v2:
        c1 = jnp.minimum(c1, jnp.where(s1 + v >= thr, v, -LOWEST))
    c1_ref[0] = c1
    s2_ref[0] = s2
    e1_ref[0] = jnp.exp(s1 - v1[0])
    e2_ref[0] = jnp.exp(s2 - v2[0]) / z


def _peer_route(hf, wq_hi, wq_lo, k1, k2, tn):
    n = hf.shape[0]
    k1h, k1l = _split(k1)
    k2h, k2l = _split(k2)
    const = lambda a: pl.BlockSpec(a.shape, lambda i, h: (0, 0))
    wq = pl.BlockSpec((D_MODEL, PEER_QDIM), lambda i, h: (0, h))
    keyed = pl.BlockSpec((1, PEER_NKEYS, tn), lambda i, h: (h, 0, i))
    shp = jax.ShapeDtypeStruct((PEER_HEADS, PEER_NKEYS, n), F32)
    return pl.pallas_call(
        _peer_route_kernel, grid=(n // tn, PEER_HEADS),
        in_specs=[pl.BlockSpec((tn, D_MODEL), lambda i, h: (i, 0)), wq, wq, const(k1h), const(k1l), const(k2h),
                  const(k2l)],
        out_specs=[keyed, keyed, keyed, keyed],
        out_shape=[shp, shp, shp, shp],
        compiler_params=_params(("parallel", "arbitrary")),
        name="peer_route",
    )(hf, wq_hi, wq_lo, k1h, k1l, k2h, k2l)


_PEER_JROWS = 32
_PEER_SLABS = 4


def _gelu_tanh(x):
    k = math.sqrt(2.0 / math.pi)
    hx = 0.5 * x
    return hx + hx * jnp.tanh(x * (k + (k * 0.044715) * (x * x)))


def _peer_main_kernel(hf_ref, y2_ref, u_ref, vt_ref, c1_ref, e1_ref, s2_ref, e2_ref, gn_ref, out_ref,
                      acc_sc, w_sc, act_sc, hfb_sc, *, tn, n_slab):
    e = pl.program_id(1)

    @pl.when(e == 0)
    def _():
        acc_sc[...] = jnp.zeros_like(acc_sc)
        hfb_sc[...] = hf_ref[...].astype(BF16)

    act_sc[...] = _gelu_tanh(_dot_nt(u_ref[...], hfb_sc[...]))

    for lc in range(tn // LANE):
        ln = slice(lc * LANE, (lc + 1) * LANE)

        def rows(jq, carry, ln=ln):
            j0 = pl.multiple_of(jq * _PEER_JROWS, _PEER_JROWS)
            for i0 in range(0, n_slab, _PEER_SLABS):
                slabs = range(i0, i0 + _PEER_SLABS)
                w = {ii: jnp.zeros((_PEER_JROWS, LANE), F32) for ii in slabs}
                for h in range(PEER_HEADS):
                    s2 = s2_ref[h, pl.ds(j0, _PEER_JROWS), ln]
                    e2 = e2_ref[h, pl.ds(j0, _PEER_JROWS), ln]
                    for ii in slabs:
                        w[ii] = w[ii] + e1_ref[h, ii:ii + 1, ln] * jnp.where(s2 >= c1_ref[h, ii:ii + 1, ln], e2, 0.0)
                for ii in slabs:
                    rs = pl.ds(ii * PEER_NKEYS + j0, _PEER_JROWS)
                    w_sc[rs, ln] = (w[ii] * act_sc[rs, ln]).astype(BF16)
            return carry

        lax.fori_loop(0, PEER_NKEYS // _PEER_JROWS, rows, 0)

    acc_sc[...] += _dot(vt_ref[...], w_sc[...])

    @pl.when(e == pl.num_programs(1) - 1)
    def _():
        out_ref[...] = _rms(y2_ref[...] + acc_sc[...].T, gn_ref[...])


def _peer_main(hf, y2, u_tab, vt_tab, c1, e1, s2, e2, gn, tn, n_slab=8):
    n = hf.shape[0]
    ec = n_slab * PEER_NKEYS
    n_exp = u_tab.shape[0]
    tok = pl.BlockSpec((tn, D_MODEL), lambda i, e: (i, 0))
    slab = pl.BlockSpec((PEER_HEADS, n_slab, tn), lambda i, e: (0, e, i))
    keyed = pl.BlockSpec((PEER_HEADS, PEER_NKEYS, tn), lambda i, e: (0, 0, i))
    return pl.pallas_call(
        functools.partial(_peer_main_kernel, tn=tn, n_slab=n_slab),
        grid=(n // tn, n_exp // ec),
        in_specs=[tok, tok, pl.BlockSpec((ec, D_MODEL), lambda i, e: (e, 0)),
                  pl.BlockSpec((D_MODEL, ec), lambda i, e: (0, e)), slab, slab, keyed, keyed,
                  pl.BlockSpec(gn.shape, lambda i, e: (0, 0))],
        out_specs=tok,
        out_shape=jax.ShapeDtypeStruct((n, D_MODEL), F32),
        scratch_shapes=[pltpu.VMEM((D_MODEL, tn), F32), pltpu.VMEM((ec, tn), BF16), pltpu.VMEM((ec, tn), F32),
                        pltpu.VMEM((tn, D_MODEL), BF16)],
        compiler_params=_params(("parallel", "arbitrary")),
        name="peer_main",
    )(hf, y2, u_tab, vt_tab, c1, e1, s2, e2, gn)


def _peer(hf, y2, wq_hi, wq_lo, k1, k2, u_tab, vt_tab, gn, tn):
    c1, e1, s2, e2 = _peer_route(hf, wq_hi, wq_lo, k1, k2, tn)
    return _peer_main(hf, y2, u_tab, vt_tab, c1, e1, s2, e2, gn, tn)


def _rope_tables(pos):
    half = HEAD_DIM // 2
    inv = ROPE_THETA ** (-jnp.arange(half, dtype=F32) / half)
    ang = pos.astype(F32)[:, None] * inv[None, :]
    cos, sin = jnp.cos(ang), jnp.sin(ang)
    return jnp.tile(jnp.concatenate([cos, cos], -1), (1, 2)), jnp.tile(jnp.concatenate([-sin, sin], -1), (1, 2))


def _proj_weight(w):
    gates = jnp.pad(w[:, 1280:1304], ((0, 0), (0, LANE - 3 * N_HEADS)))
    return jnp.concatenate([w[:, :1280], gates, w[:, 1304:1816], w[:, 1816:]], axis=1)


def _compress_weights(pe, w1, w2):
    r = CMP_LEN // CMP_STRIDE
    eye = jnp.eye(N_KV, dtype=F32)
    w1r = w1.reshape(r, CMP_STRIDE, HEAD_DIM, w1.shape[-1])
    wc = jnp.einsum('jsdh,gk->sgdjkh', w1r, eye).reshape(CMP_STRIDE * KV_WIDTH, r * N_KV * w1.shape[-1])
    pe2 = jnp.broadcast_to(pe.reshape(r, CMP_STRIDE, 1, HEAD_DIM), (r, CMP_STRIDE, N_KV, HEAD_DIM))
    pe2 = jnp.pad(pe2.reshape(r, CMP_STRIDE * KV_WIDTH), ((0, 8 - r), (0, 0)))
    w2bd = jnp.einsum('hd,gk->ghkd', w2, eye).reshape(N_KV * w2.shape[0], KV_WIDTH)
    return wc.astype(BF16), pe2, w2bd.astype(BF16)


def kernel(x_prompt, x_sample, cache_k_cmp, cache_v_cmp, cache_k_sel, cache_v_sel, cache_k_win, cache_v_win,
           state_ssm_re, state_ssm_im, page_table, norm_mix, w_in, cmp_pe_k, cmp_w1_k, cmp_w2_k, cmp_pe_v, cmp_w1_v,
           cmp_w2_v, ssm_a_re, ssm_a_im, ssm_log_dt, ssm_b_re, ssm_b_im, ssm_c_re, ssm_c_im, ssm_d, w_att_proj,
           w_ssm_glu, w_out, norm_ffn, peer_w_q, peer_keys1, peer_keys2, peer_u, peer_v, norm_final):
    assert w_in.shape[0] == 1, "single layer"
    B, T, _ = x_prompt.shape
    DB = x_sample.shape[0]
    n_pages = page_table.shape[1]
    past = n_pages * PAGE_SIZE
    n_pool = cache_k_cmp.shape[1]

    w_hi, w_lo = _split(_proj_weight(w_in[0]))
    g_mix = norm_mix[0].reshape(1, D_MODEL)
    cw = _compress_weights(cmp_pe_k[0], cmp_w1_k[0], cmp_w2_k[0]) + _compress_weights(cmp_pe_v[0], cmp_w1_v[0],
                                                                                       cmp_w2_v[0])
    sp = _ssm_params(ssm_a_re[0], ssm_a_im[0], ssm_log_dt[0], ssm_b_re[0], ssm_b_im[0], ssm_c_re[0], ssm_c_im[0],
                     ssm_d[0])
    wa, ws, wo = w_att_proj[0].astype(BF16), w_ssm_glu[0].astype(BF16), w_out[0].astype(BF16)
    g_ffn = norm_ffn[0].reshape(1, D_MODEL)
    g_fin = norm_final.reshape(1, D_MODEL)
    wq_hi, wq_lo = _split(peer_w_q[0])
    u_tab = peer_u[0].astype(BF16)
    vt_tab = peer_v[0].T.astype(BF16)

    cos_p, sin_p = _rope_tables(jnp.arange(T, dtype=jnp.int32))
    q, kc, vc, ks, vs, kw, vw, gt, u, gm = _project(x_prompt, g_mix, cos_p, sin_p, [w_hi], 256, False)
    chunks = lambda a: a.reshape(B, T // CMP_STRIDE, CMP_STRIDE * KV_WIDTH)
    ck, cv = _compress_prompt(chunks(kc), chunks(vc), cw)
    o_att = _attn_prompt(q, gt, ck, cv, ks, vs, kw, vw)
    y_ssm, hp_re, hp_im = _ssm_prompt(u, sp)
    n_p = B * T
    flat = lambda a: a.reshape(n_p, a.shape[-1])
    y2_p, hf_p = _merge(flat(x_prompt), flat(o_att), flat(y_ssm), flat(gm), wa, ws, wo, g_ffn, 256)
    y_prompt = _peer(hf_p, y2_p, wq_hi, wq_lo, peer_keys1[0], peer_keys2[0], u_tab, vt_tab, g_fin, 512)

    cos_s, sin_s = _rope_tables(jnp.full((DB,), past, jnp.int32))
    xs = x_sample.reshape(1, DB, D_MODEL)
    qs, kcs, vcs, kss, vss, kws, vws, gts, us, gms = [a[0] for a in
                                                      _project(xs, g_mix, cos_s, sin_s, [w_hi, w_lo], DB, True)]
    row3 = lambda a: a.reshape(DB, 1, KV_WIDTH)
    native = lambda c: jnp.transpose(c[0], (0, 2, 3, 1)).reshape(c.shape[1], KV_WIDTH, c.shape[2])
    cks, cvs = _compress_sample(page_table, native(cache_k_cmp), native(cache_v_cmp), row3(kcs), row3(vcs), cw)
    q5 = qs.reshape(DB, N_KV, Q_PER_KV, 1, HEAD_DIM) * jnp.eye(N_KV, dtype=F32).reshape(1, N_KV, 1, N_KV, 1)
    qbd = q5.reshape(DB, N_HEADS, KV_WIDTH)
    n_blocks = -(-(past + 1) // SEL_BLOCK)
    ns_pad = -(-n_blocks // LANE) * LANE
    oc, sel = _attn_sample_cmp(qbd, cks, cvs, past, n_blocks, ns_pad)
    gt3 = gts[:, :3 * N_HEADS].reshape(DB, N_HEADS, 3)
    o_s = _attn_sample(page_table, qbd, sel, oc, gt3, row3(kss), row3(vss), native(cache_k_win), native(cache_v_win),
                       row3(kws), row3(vws), native(cache_k_sel), native(cache_v_sel), past)
    h0r = state_ssm_re[0].reshape(DB, SSM_GROUPS * SSM_STATE)
    h0i = state_ssm_im[0].reshape(DB, SSM_GROUPS * SSM_STATE)
    ys_ssm, hs_re, hs_im = _ssm_sample(us, h0r, h0i, sp)
    y2_s, hf_s = _merge(x_sample.reshape(DB, D_MODEL), o_s.reshape(DB, ATT_WIDTH), ys_ssm, gms, wa, ws, wo, g_ffn, DB)
    y_sample = _peer(hf_s, y2_s, wq_hi, wq_lo, peer_keys1[0], peer_keys2[0], u_tab, vt_tab, g_fin, DB)

    kv5 = lambda a, n: a.reshape(1, n, -1, N_KV, HEAD_DIM)
    wb = min(WINDOW, T)
    st = lambda a, n: a.reshape(1, n, SSM_GROUPS, SSM_STATE)
    wbuf = cache_k_win.shape[2]
    nw = min(WINDOW, wbuf + 1)
    win_s = lambda old, new: jnp.concatenate([old[0].reshape(DB, wbuf, KV_WIDTH), new.reshape(DB, 1, KV_WIDTH)],
                                             axis=1)[:, wbuf + 1 - nw:]
    return (y_prompt.reshape(B, T, D_MODEL), y_sample.reshape(DB, 1, D_MODEL),
            kv5(kc, B), kv5(vc, B), kv5(ks, B), kv5(vs, B), kv5(kw[:, T - wb:], B), kv5(vw[:, T - wb:], B),
            st(hp_re, B), st(hp_im, B),
            kv5(kcs, DB), kv5(vcs, DB), kv5(kss, DB), kv5(vss, DB),
            kv5(win_s(cache_k_win, kws), DB), kv5(win_s(cache_v_win, vws), DB),
            st(hs_re, DB), st(hs_im, DB))
```

```python
import functools
import math

import jax
import jax.numpy as jnp
from jax import lax
from jax.experimental import pallas as pl
from jax.experimental.pallas import tpu as pltpu

F32 = jnp.float32
BF16 = jnp.bfloat16

D_MODEL = 1024
HEAD_DIM = 64
N_HEADS = 8
N_KV = 2
Q_PER_KV = 4
ATT_WIDTH = 512
KV_WIDTH = 128
CMP_LEN = 32
CMP_STRIDE = 16
SEL_BLOCK = 64
SEL_TOPN = 16
WINDOW = 512
ROPE_THETA = 10000.0
PAGE_SIZE = 128
SSM_GROUP = 16
SSM_WIDTH = 512
SSM_GROUPS = 32
SSM_STATE = 64
PEER_HEADS = 8
PEER_NKEYS = 128
PEER_QDIM = 256
PEER_TOPK = 16
RMS_EPS = 1e-6
NEG = -1e30
BIG = 1e9
TINY = 1e-30
LOWEST = -3.0e38

LANE = 128
VMEM_LIMIT = 56 * 1024 * 1024

_NT = (((1,), (1,)), ((), ()))


def _params(sem, vmem=VMEM_LIMIT):
    return pltpu.CompilerParams(dimension_semantics=sem, vmem_limit_bytes=vmem)


def _split(x):
    hi = x.astype(BF16)
    lo = (x - hi.astype(F32)).astype(BF16)
    return hi, lo


def _dot(a, b):
    return jnp.dot(a, b, preferred_element_type=F32)


def _dot_nt(a, b):
    return lax.dot_general(a, b, _NT, preferred_element_type=F32)


def _dot3(a_hi, a_lo, b_hi, b_lo):
    return _dot(a_hi, b_hi) + (_dot(a_hi, b_lo) + _dot(a_lo, b_hi))


def _dot3_nt(a_hi, a_lo, b_hi, b_lo):
    return _dot_nt(a_hi, b_hi) + (_dot_nt(a_hi, b_lo) + _dot_nt(a_lo, b_hi))


def _rms(x, g):
    return x * lax.rsqrt(jnp.mean(x * x, axis=-1, keepdims=True) + RMS_EPS) * g


def _softmax_rows(s, mask):
    s = jnp.where(mask, s, NEG)
    e = jnp.where(mask, jnp.exp(s - jnp.max(s, axis=-1, keepdims=True)), 0.0)
    return e / jnp.maximum(jnp.sum(e, axis=-1, keepdims=True), TINY)


_SEG = {'q': (0, 512), 'kc': (512, 640), 'vc': (640, 768), 'ks': (768, 896), 'vs': (896, 1024),
        'kw': (1024, 1152), 'vw': (1152, 1280), 'gt': (1280, 1408), 'u': (1408, 1920), 'gm': (1920, 3968)}
_PROJ_COLS = 3968


def _proj_kernel(*refs, precise):
    x_ref, g_ref, cos_ref, sin_ref = refs[:4]
    n_w = 2 if precise else 1
    w_refs = refs[4:4 + n_w]
    q_ref, kc_ref, vc_ref, ks_ref, vs_ref, kw_ref, vw_ref, gt_ref, u_ref, gm_ref = refs[4 + n_w:]
    h = _rms(x_ref[0], g_ref[...])
    h_hi = h.astype(BF16)
    h_lo = (h - h_hi.astype(F32)).astype(BF16) if precise else None

    def mm(c0, c1):
        z = _dot(h_hi, w_refs[0][:, c0:c1])
        if precise:
            z = z + (_dot(h_hi, w_refs[1][:, c0:c1]) + _dot(h_lo, w_refs[0][:, c0:c1]))
        return z

    cos = cos_ref[...]
    sin = sin_ref[...]
    first = (lax.broadcasted_iota(jnp.int32, (1, LANE), 1) % HEAD_DIM) < (HEAD_DIM // 2)

    def rope(z):
        rot = jnp.where(first, pltpu.roll(z, LANE - HEAD_DIM // 2, 1), pltpu.roll(z, HEAD_DIM // 2, 1))
        return z * cos + rot * sin

    for i in range(4):
        q_ref[0, :, i * LANE:(i + 1) * LANE] = rope(mm(i * LANE, (i + 1) * LANE))
    kc_ref[0] = rope(mm(*_SEG['kc']))
    vc_ref[0] = mm(*_SEG['vc'])
    ks_ref[0] = rope(mm(*_SEG['ks']))
    vs_ref[0] = mm(*_SEG['vs'])
    kw_ref[0] = rope(mm(*_SEG['kw']))
    vw_ref[0] = mm(*_SEG['vw'])
    gt_ref[0] = mm(*_SEG['gt'])
    u_ref[0] = mm(*_SEG['u'])
    for i in range(4):
        c0 = _SEG['gm'][0] + i * 512
        gm_ref[0, :, i * 512:(i + 1) * 512] = mm(c0, c0 + 512)


def _project(x, gain, cos, sin, w_list, tm, precise):
    B, T, D = x.shape
    widths = [512, 128, 128, 128, 128, 128, 128, 128, 512, 2048]
    tok = lambda w: pl.BlockSpec((1, tm, w), lambda b, t: (b, t, 0))
    const = lambda a: pl.BlockSpec(a.shape, lambda b, t: (0,) * a.ndim)
    return pl.pallas_call(
        functools.partial(_proj_kernel, precise=precise),
        grid=(B, T // tm),
        in_specs=[tok(D), const(gain), pl.BlockSpec((tm, LANE), lambda b, t: (t, 0)),
                  pl.BlockSpec((tm, LANE), lambda b, t: (t, 0))] + [const(w) for w in w_list],
        out_specs=[tok(w) for w in widths],
        out_shape=[jax.ShapeDtypeStruct((B, T, w), F32) for w in widths],
        compiler_params=_params(("parallel", "arbitrary")),
        name="proj",
    )(x, gain, cos, sin, *w_list)


def _compress_rows(x, last_p1, wc_ref, pe_ref, w2_ref):
    C = x.shape[0]
    p = _dot(x.astype(BF16), wc_ref[...])
    pb = _dot(pe_ref[...].astype(BF16), wc_ref[...])
    bias = pb[0:1, :LANE] + pb[1:2, LANE:]
    p1 = pltpu.roll(p[:, LANE:], C - 1, 0)
    if last_p1 is not None:
        row = lax.broadcasted_iota(jnp.int32, (C, 1), 0)
        p1 = jnp.where(row == C - 1, last_p1, p1)
    hid = p[:, :LANE] + p1 + bias
    return _dot(jax.nn.gelu(hid).astype(BF16), w2_ref[...])


def _compress_prompt_kernel(xk_ref, xv_ref, wck_ref, pek_ref, w2k_ref, wcv_ref, pev_ref, w2v_ref, ck_ref, cv_ref):
    ck_ref[0] = _compress_rows(xk_ref[0], None, wck_ref, pek_ref, w2k_ref)
    cv_ref[0] = _compress_rows(xv_ref[0], None, wcv_ref, pev_ref, w2v_ref)


def _compress_prompt(xk, xv, cw):
    B, C, W = xk.shape
    seq = pl.BlockSpec((1, C, W), lambda b: (b, 0, 0))
    const = lambda a: pl.BlockSpec(a.shape, lambda b: (0,) * a.ndim)
    out = pl.BlockSpec((1, C, LANE), lambda b: (b, 0, 0))
    return pl.pallas_call(
        _compress_prompt_kernel,
        grid=(B,),
        in_specs=[seq, seq] + [const(a) for a in cw],
        out_specs=[out, out],
        out_shape=[jax.ShapeDtypeStruct((B, C, LANE), F32)] * 2,
        compiler_params=_params(("arbitrary",)),
        name="compress_prompt",
    )(xk, xv, *cw)


def _paged_fetch(pt_ref, pools, bufs, sems, n_pages, dst):
    b = pl.program_id(0)
    nb = pl.num_programs(0)
    slot = b % 2

    def copies(bb, sl):
        return [pltpu.make_async_copy(pool.at[pt_ref[bb, p]], dst(buf, sl, p), sem.at[sl])
                for pool, buf, sem in zip(pools, bufs, sems) for p in range(n_pages)]

    @pl.when(b == 0)
    def _():
        for cp in copies(b, slot):
            cp.start()

    @pl.when(b + 1 < nb)
    def _():
        for cp in copies(b + 1, 1 - slot):
            cp.start()

    for cp in copies(b, slot):
        cp.wait()
    return slot


def _compress_sample_kernel(pt_ref, kn_ref, vn_ref, perm_ref, wck_ref, pek_ref, w2k_ref, wcv_ref, pev_ref, w2v_ref,
                            kpool_ref, vpool_ref, ck_ref, cv_ref, kbuf, vbuf, ksem, vsem, taps_sc, *, n_pages):
    slot = _paged_fetch(pt_ref, (kpool_ref, vpool_ref), (kbuf, vbuf), (ksem, vsem), n_pages,
                        lambda buf, sl, p: buf.at[sl, p])
    cpp = PAGE_SIZE // CMP_STRIDE

    def one(buf, new_ref, wc_ref, pe_ref, w2_ref, out_ref):
        def flip(p, carry):
            rows = _dot_nt(perm_ref[...], buf[slot, p].astype(BF16))
            c0 = pl.multiple_of(p * cpp, cpp)
            for s in range(CMP_STRIDE):
                taps_sc[s, pl.ds(c0, cpp), :] = rows[s * cpp:(s + 1) * cpp]
            return carry

        lax.fori_loop(0, n_pages, flip, 0, unroll=32)
        x = jnp.concatenate([taps_sc[s] for s in range(CMP_STRIDE)], axis=1)
        new = jnp.broadcast_to(new_ref[0], (8, LANE)).astype(BF16)
        last_p1 = _dot(new, wc_ref[0:LANE, LANE:])[0:1]
        out_ref[0] = _compress_rows(x, last_p1, wc_ref, pe_ref, w2_ref)

    one(kbuf, kn_ref, wck_ref, pek_ref, w2k_ref, ck_ref)
    one(vbuf, vn_ref, wcv_ref, pev_ref, w2v_ref, cv_ref)


def _compress_sample(page_table, kpool, vpool, k_new, v_new, cw):
    DB, n_pages = page_table.shape
    C = n_pages * PAGE_SIZE // CMP_STRIDE
    new = pl.BlockSpec((1, 1, LANE), lambda b, pt: (b, 0, 0))
    const = lambda a: pl.BlockSpec(a.shape, lambda b, pt: (0,) * a.ndim)
    hbm = pl.BlockSpec(memory_space=pl.ANY)
    out = pl.BlockSpec((1, C, LANE), lambda b, pt: (b, 0, 0))
    page_buf = pltpu.VMEM((2, n_pages, KV_WIDTH, PAGE_SIZE), F32)
    cpp = PAGE_SIZE // CMP_STRIDE
    tok = jnp.arange(PAGE_SIZE)
    perm = ((tok[:, None] % cpp) * CMP_STRIDE + tok[:, None] // cpp == tok[None, :]).astype(BF16)
    return pl.pallas_call(
        functools.partial(_compress_sample_kernel, n_pages=n_pages),
        grid_spec=pltpu.PrefetchScalarGridSpec(
            num_scalar_prefetch=1, grid=(DB,),
            in_specs=[new, new, const(perm)] + [const(a) for a in cw] + [hbm, hbm],
            out_specs=[out, out],
            scratch_shapes=[page_buf, page_buf, pltpu.SemaphoreType.DMA((2,)), pltpu.SemaphoreType.DMA((2,)),
                            pltpu.VMEM((CMP_STRIDE, C, KV_WIDTH), F32)]),
        out_shape=[jax.ShapeDtypeStruct((DB, C, LANE), F32)] * 2,
        compiler_params=_params(("arbitrary",)),
        name="compress_sample",
    )(page_table, k_new, v_new, perm, *cw, kpool, vpool)


def _select_blocks(imp, qpos, n_real, axis=1):
    j_shape = (1, imp.shape[1]) if axis == 1 else (imp.shape[0], 1)
    j = lax.broadcasted_iota(jnp.int32, j_shape, axis)
    valid = j * SEL_BLOCK <= qpos
    cur = qpos // SEL_BLOCK
    force = (j == 0) | (j == cur) | (j == cur - 1)
    score = jnp.where(valid & force, BIG, jnp.where(valid, imp, -BIG))
    rank = jnp.zeros(imp.shape, F32)
    for jp in range(n_real):
        other = score[:, jp:jp + 1] if axis == 1 else score[jp:jp + 1, :]
        earlier = jnp.where(j > jp, 1.0, 0.0)
        rank = rank + jnp.where(other > score, 1.0, jnp.where(other == score, earlier, 0.0))
    return jnp.where(rank < SEL_TOPN, 1.0, 0.0)


def _attn_prompt_kernel(q_ref, gt_ref, ck_ref, cv_ref, ks_ref, vs_ref, kw_ref, vw_ref, msel_ref, exp_ref, o_ref,
                        m_sc, acc_sc, *, tq, kc, rb, wb):
    s0 = pl.program_id(1) * tq
    R = Q_PER_KV * tq
    ncb = ck_ref.shape[1]
    qpos_r = s0 + lax.broadcasted_iota(jnp.int32, (R, 1), 0) % tq
    qpos_t = s0 + lax.broadcasted_iota(jnp.int32, (tq, 1), 0)
    qpos_l = s0 + lax.broadcasted_iota(jnp.int32, (1, tq), 1)
    sig = jax.nn.sigmoid(gt_ref[0])
    stack = lambda f: jnp.concatenate([f(r) for r in range(Q_PER_KV)], axis=0)

    def rows_of(bias, r0, n):
        if n >= tq:
            return jnp.concatenate([bias] * (n // tq), axis=0)
        return bias[r0 % tq:r0 % tq + n]

    heads = []
    for g in range(N_KV):
        gl = slice(g * HEAD_DIM, (g + 1) * HEAD_DIM)
        qg = stack(lambda r: q_ref[0, :, (g * Q_PER_KV + r) * HEAD_DIM:(g * Q_PER_KV + r + 1) * HEAD_DIM])
        qg = qg * (HEAD_DIM ** -0.5)
        q_hi = qg.astype(BF16)

        s_c = _dot_nt(q_hi, ck_ref[0, :, gl].astype(BF16))
        cend = lax.broadcasted_iota(jnp.int32, (1, ncb), 1) * CMP_STRIDE + (CMP_LEN - 1)
        p_c = _softmax_rows(s_c, cend <= qpos_r)
        o_c = _dot(p_c.astype(BF16), cv_ref[0, :, gl].astype(BF16))
        p_sum = p_c[0:tq]
        for r in range(1, Q_PER_KV):
            p_sum = p_sum + p_c[r * tq:(r + 1) * tq]
        ps_hi, ps_lo = _split(p_sum)
        msel_t = msel_ref[...]
        imp_t = _dot_nt(msel_t, ps_hi) + _dot_nt(msel_t, ps_lo)
        sel = _select_blocks(imp_t, qpos_l, imp_t.shape[0], axis=0).T

        unpicked = ((sel - 1.0) * -NEG).astype(BF16)
        m_sc[...] = jnp.full((R, 1), NEG, F32)
        acc_sc[...] = jnp.zeros((R, KV_WIDTH), F32)
        own = (lax.broadcasted_iota(jnp.int32, (1, KV_WIDTH), 1) // HEAD_DIM) == g
        ol = slice((1 - g) * HEAD_DIM, (1 - g) * HEAD_DIM + 1)
        with_ones = lambda v: jnp.where(own, v, 1.0).astype(BF16)

        def chunk(c, carry):
            off = pl.multiple_of(c * kc, kc)
            k = ks_ref[0, pl.ds(off, kc), gl].astype(BF16)
            v = with_ones(vs_ref[0, pl.ds(off, kc), :])
            kpos = off + lax.broadcasted_iota(jnp.int32, (1, kc), 1)
            bias = _dot(unpicked, exp_ref[c]) + jnp.where(kpos <= qpos_t, 0.0, NEG)
            for r0 in range(0, R, rb):
                rs = slice(r0, r0 + rb)
                s = _dot_nt(q_hi[rs], k) + rows_of(bias, r0, rb)
                m_old = m_sc[rs]
                m_new = jnp.maximum(m_old, jnp.max(s, axis=-1, keepdims=True))
                p = jnp.exp(s - m_new)
                acc_sc[rs] = jnp.exp(m_old - m_new) * acc_sc[rs] + _dot(p.astype(BF16), v)
                m_sc[rs] = m_new
            return carry

        lax.fori_loop(0, (s0 + tq + kc - 1) // kc, chunk, 0)
        o_s = acc_sc[:, gl] / jnp.maximum(acc_sc[:, ol], TINY)

        wl = WINDOW + tq
        w0 = pl.multiple_of(jnp.maximum(s0 - WINDOW, 0), tq)
        kpos = w0 + lax.broadcasted_iota(jnp.int32, (1, wl), 1)
        dlt = qpos_t - kpos
        bias_w = jnp.where((dlt >= 0) & (dlt < WINDOW), 0.0, NEG)
        k_w = kw_ref[0, pl.ds(w0, wl), gl].astype(BF16)
        v_w = with_ones(vw_ref[0, pl.ds(w0, wl), :])
        o_w = []
        for r0 in range(0, R, wb):
            s_w = _dot_nt(q_hi[r0:r0 + wb], k_w) + rows_of(bias_w, r0, wb)
            p_w = jnp.exp(s_w - jnp.max(s_w, axis=-1, keepdims=True))
            ov = _dot(p_w.astype(BF16), v_w)
            o_w.append(ov[:, gl] / jnp.maximum(ov[:, ol], TINY))
        o_w = jnp.concatenate(o_w, axis=0)

        gate = lambda i: stack(lambda r: sig[:, (g * Q_PER_KV + r) * 3 + i:(g * Q_PER_KV + r) * 3 + i + 1])
        og = gate(0) * o_c + gate(1) * o_s + gate(2) * o_w
        heads += [og[r * tq:(r + 1) * tq] for r in range(Q_PER_KV)]
    o_ref[0] = jnp.concatenate(heads, axis=1)


def _attn_prompt(q, gt, ck, cv, ks, vs, kw, vw, tq=128, kc=512, rb=512, wb=128):
    B, T, _ = q.shape
    assert T % kc == 0 and T % tq == 0 and T >= WINDOW + tq and (tq % rb == 0 or rb % tq == 0)
    ncb = ck.shape[1]
    ns = T // SEL_BLOCK
    i = jnp.arange(ncb)[None, :]
    j = jnp.arange(ns)[:, None]
    msel = ((i * CMP_STRIDE <= j * SEL_BLOCK + SEL_BLOCK - 1)
            & (i * CMP_STRIDE + CMP_LEN - 1 >= j * SEL_BLOCK)).astype(BF16)
    key_blk = (jnp.arange(T) // SEL_BLOCK).reshape(T // kc, 1, kc)
    expand = (key_blk == jnp.arange(ns)[None, :, None]).astype(BF16)
    tok = lambda w: pl.BlockSpec((1, tq, w), lambda b, t: (b, t, 0))
    seq = lambda a: pl.BlockSpec((1,) + a.shape[1:], lambda b, t: (b, 0, 0))
    const = lambda a: pl.BlockSpec(a.shape, lambda b, t: (0,) * a.ndim)
    R = Q_PER_KV * tq
    return pl.pallas_call(
        functools.partial(_attn_prompt_kernel, tq=tq, kc=kc, rb=rb, wb=wb),
        grid=(B, T // tq),
        in_specs=[tok(ATT_WIDTH), tok(LANE), seq(ck), seq(cv), seq(ks), seq(vs), seq(kw), seq(vw),
                  const(msel), const(expand)],
        out_specs=tok(ATT_WIDTH),
        out_shape=jax.ShapeDtypeStruct((B, T, ATT_WIDTH), F32),
        scratch_shapes=[pltpu.VMEM((R, 1), F32), pltpu.VMEM((R, KV_WIDTH), F32)],
        compiler_params=_params(("parallel", "arbitrary")),
        name="attn_prompt",
    )(q, gt, ck, cv, ks, vs, kw, vw, msel, expand)


def _group_lanes():
    row = lax.broadcasted_iota(jnp.int32, (N_HEADS, LANE), 0) // Q_PER_KV
    lane = lax.broadcasted_iota(jnp.int32, (N_HEADS, LANE), 1) // HEAD_DIM
    return row == lane


def _attn_sample_cmp_kernel(q_ref, ck_ref, cv_ref, msel_ref, oc_ref, sel_ref, *, qpos, n_blocks):
    q_hi, q_lo = _split(q_ref[0] * (HEAD_DIM ** -0.5))
    k_hi, k_lo = _split(ck_ref[0])
    s_c = _dot3_nt(q_hi, q_lo, k_hi, k_lo)
    ncb = s_c.shape[1]
    cend = lax.broadcasted_iota(jnp.int32, (1, ncb), 1) * CMP_STRIDE + (CMP_LEN - 1)
    p_c = _softmax_rows(s_c, cend <= qpos)
    o_c = _dot(p_c.astype(BF16), cv_ref[0].astype(BF16))
    oc_ref[0] = jnp.where(_group_lanes(), o_c, 0.0)
    top = jnp.sum(p_c[0:Q_PER_KV], axis=0, keepdims=True)
    bot = jnp.sum(p_c[Q_PER_KV:], axis=0, keepdims=True)
    row = lax.broadcasted_iota(jnp.int32, (N_HEADS, 1), 0)
    p_sum = jnp.where(row < Q_PER_KV, top, bot)
    ps_hi, ps_lo = _split(p_sum)
    msel = msel_ref[...]
    imp = _dot(ps_hi, msel) + _dot(ps_lo, msel)
    sel_ref[0] = _select_blocks(imp, jnp.full((N_HEADS, 1), qpos, jnp.int32), n_blocks)


def _attn_sample_cmp(qbd, ck, cv, qpos, n_blocks, ns_pad):
    DB, ncb, _ = ck.shape
    i = jnp.arange(ncb)[:, None]
    j = jnp.arange(ns_pad)[None, :]
    msel = ((i * CMP_STRIDE <= j * SEL_BLOCK + SEL_BLOCK - 1) & (i * CMP_STRIDE + CMP_LEN - 1 >= j * SEL_BLOCK)
            & (j < n_blocks)).astype(BF16)
    row = lambda a: pl.BlockSpec((1,) + a.shape[1:], lambda b: (b, 0, 0))
    return pl.pallas_call(
        functools.partial(_attn_sample_cmp_kernel, qpos=qpos, n_blocks=n_blocks),
        grid=(DB,),
        in_specs=[row(qbd), row(ck), row(cv), pl.BlockSpec(msel.shape, lambda b: (0, 0))],
        out_specs=[pl.BlockSpec((1, N_HEADS, LANE), lambda b: (b, 0, 0)),
                   pl.BlockSpec((1, N_HEADS, ns_pad), lambda b: (b, 0, 0))],
        out_shape=[jax.ShapeDtypeStruct((DB, N_HEADS, LANE), F32), jax.ShapeDtypeStruct((DB, N_HEADS, ns_pad), F32)],
        compiler_params=_params(("arbitrary",)),
        name="attn_sample_cmp",
    )(qbd, ck, cv, msel)


def _attn_sample_kernel(pt_ref, q_ref, sel_ref, oc_ref, gt_ref, ksn_ref, vsn_ref, kwb_ref, vwb_ref, kwn_ref, vwn_ref,
                        exp_ref, kpool_ref, vpool_ref, o_ref, kbuf, vbuf, ksem, vsem, *, n_pages, qpos):
    slot = _paged_fetch(pt_ref, (kpool_ref, vpool_ref), (kbuf, vbuf), (ksem, vsem), n_pages,
                        lambda buf, sl, p: buf.at[sl, :, pl.ds(p * PAGE_SIZE, PAGE_SIZE)])
    past = n_pages * PAGE_SIZE
    q = q_ref[0] * (HEAD_DIM ** -0.5)
    q_b = q.astype(BF16)
    sel = sel_ref[0]

    def attend(kt_old, vt_old, mask_old, k_new, v_new, mask_new):
        s = jnp.where(mask_old, _dot(q_b, kt_old.astype(BF16)), NEG)
        s_n = jnp.where(mask_new, jnp.sum(q * k_new, axis=-1, keepdims=True), NEG)
        m = jnp.maximum(jnp.max(s, axis=-1, keepdims=True), s_n)
        e = jnp.where(mask_old, jnp.exp(s - m), 0.0)
        e_n = jnp.where(mask_new, jnp.exp(s_n - m), 0.0)
        den = jnp.maximum(jnp.sum(e, axis=-1, keepdims=True) + e_n, TINY)
        return (_dot_nt(e.astype(BF16), vt_old.astype(BF16)) + e_n * v_new) / den

    picked = _dot(sel.astype(BF16), exp_ref[...])
    kpos = lax.broadcasted_iota(jnp.int32, (1, past), 1)
    nb_new = past // SEL_BLOCK
    o_s = attend(kbuf[slot], vbuf[slot], (picked > 0.5) & (kpos <= qpos), ksn_ref[0], vsn_ref[0],
                 (sel[:, nb_new:nb_new + 1] > 0.5) & (past <= qpos))
    wbuf = kwb_ref.shape[2]
    dlt = qpos - (past - wbuf + lax.broadcasted_iota(jnp.int32, (1, wbuf), 1))
    o_w = attend(kwb_ref[0], vwb_ref[0], (dlt >= 0) & (dlt < WINDOW), kwn_ref[0], vwn_ref[0],
                 jnp.full((N_HEADS, 1), (qpos - past >= 0) & (qpos - past < WINDOW)))
    sig = jax.nn.sigmoid(gt_ref[0])
    o = sig[:, 0:1] * oc_ref[0] + sig[:, 1:2] * o_s + sig[:, 2:3] * o_w
    o = jnp.where(_group_lanes(), o, 0.0)
    o_ref[0] = o[:, :HEAD_DIM] + o[:, HEAD_DIM:]


def _attn_sample(page_table, qbd, sel, oc, gt3, ksn, vsn, kwb, vwb, kwn, vwn, kpool, vpool, qpos):
    DB, n_pages = page_table.shape
    past = n_pages * PAGE_SIZE
    nsp = sel.shape[-1]
    expand = ((jnp.arange(past) // SEL_BLOCK)[None, :] == jnp.arange(nsp)[:, None]).astype(BF16)
    row = lambda a: pl.BlockSpec((1,) + a.shape[1:], lambda b, pt: (b, 0, 0))
    hbm = pl.BlockSpec(memory_space=pl.ANY)
    ins = [qbd, sel, oc, gt3, ksn, vsn, kwb, vwb, kwn, vwn]
    return pl.pallas_call(
        functools.partial(_attn_sample_kernel, n_pages=n_pages, qpos=qpos),
        grid_spec=pltpu.PrefetchScalarGridSpec(
            num_scalar_prefetch=1, grid=(DB,),
            in_specs=[row(a) for a in ins] + [pl.BlockSpec(expand.shape, lambda b, pt: (0, 0)), hbm, hbm],
            out_specs=pl.BlockSpec((1, N_HEADS, HEAD_DIM), lambda b, pt: (b, 0, 0)),
            scratch_shapes=[pltpu.VMEM((2, KV_WIDTH, past), F32), pltpu.VMEM((2, KV_WIDTH, past), F32),
                            pltpu.SemaphoreType.DMA((2,)), pltpu.SemaphoreType.DMA((2,))]),
        out_shape=jax.ShapeDtypeStruct((DB, N_HEADS, HEAD_DIM), F32),
        compiler_params=_params(("arbitrary",)),
        name="attn_sample",
    )(page_table, *ins, expand, kpool, vpool)


_SSM_HALF = SSM_WIDTH // 2
_SSM_ROW = SSM_GROUPS // 2 * SSM_STATE


def _ssm_params(a_re, a_im, log_dt, b_re, b_im, c_re, c_im, d):
    lam = lax.complex(a_re, a_im)
    step = jnp.exp(log_dt)[:, None]
    a_bar = jnp.exp(lam * step)
    b_bar = ((a_bar - 1.0) / lam)[..., None] * lax.complex(b_re, b_im)
    eye = jnp.eye(SSM_GROUPS // 2, dtype=F32)

    def b_mat(x):
        x = x.reshape(2, SSM_GROUPS // 2, SSM_STATE, SSM_GROUP)
        return jnp.einsum('hgpc,gk->hgckp', x, eye).reshape(2, _SSM_HALF, _SSM_ROW)

    def c_mat(x):
        x = x.reshape(2, SSM_GROUPS // 2, SSM_GROUP, SSM_STATE)
        return jnp.einsum('hgcp,gk->hgpkc', x, eye).reshape(2, _SSM_ROW, _SSM_HALF)

    bm = jnp.concatenate([b_mat(jnp.real(b_bar)), b_mat(jnp.imag(b_bar))], axis=2)
    cm = jnp.concatenate([c_mat(c_re), -c_mat(c_im)], axis=1)
    ar = jnp.real(a_bar).reshape(2, _SSM_ROW)
    ai = jnp.imag(a_bar).reshape(2, _SSM_ROW)
    bm_hi, bm_lo = _split(bm)
    cm_hi, cm_lo = _split(cm)
    return ar, ai, bm_hi, bm_lo, cm_hi, cm_lo, d.reshape(1, SSM_WIDTH)


def _ssm_prompt_kernel(u_ref, ar_ref, ai_ref, bh_ref, ch_ref, d_ref, y_ref, hr_ref, hi_ref,
                       sr_sc, si_sc, st_re, st_im, *, tc, nb):
    t = pl.program_id(0)
    rows = 2 * nb
    nlb = _SSM_ROW // LANE

    @pl.when(t == 0)
    def _():
        st_re[...] = jnp.zeros_like(st_re)
        st_im[...] = jnp.zeros_like(st_im)

    for b in range(nb):
        for hf in range(2):
            u_hi, u_lo = _split(u_ref[b, :, hf * _SSM_HALF:(hf + 1) * _SSM_HALF])
            bu = _dot(u_hi, bh_ref[hf]) + _dot(u_lo, bh_ref[hf])
            for k in range(nlb):
                sr_sc[k, pl.ds(b * 2 + hf, tc, stride=rows), :] = bu[:, k * LANE:(k + 1) * LANE]
                si_sc[k, pl.ds(b * 2 + hf, tc, stride=rows), :] = bu[:, _SSM_ROW + k * LANE:_SSM_ROW + (k + 1) * LANE]

    ar = ar_ref[...]
    ai = ai_ref[...]

    def step(i, carry):
        h_re, h_im = carry
        r0 = pl.multiple_of(i * rows, rows)
        n_re = ar * h_re - ai * h_im + sr_sc[:, pl.ds(r0, rows), :]
        n_im = ar * h_im + ai * h_re + si_sc[:, pl.ds(r0, rows), :]
        sr_sc[:, pl.ds(r0, rows), :] = n_re
        si_sc[:, pl.ds(r0, rows), :] = n_im
        return n_re, n_im

    h_re, h_im = lax.fori_loop(0, tc, step, (st_re[...], st_im[...]), unroll=4)
    st_re[...] = h_re
    st_im[...] = h_im
    hr_ref[...] = h_re
    hi_ref[...] = h_im

    for b in range(nb):
        for hf in range(2):
            gather = lambda sc: jnp.concatenate(
                [sc[k, pl.ds(b * 2 + hf, tc, stride=rows), :] for k in range(nlb)], axis=1)
            hs = jnp.concatenate([gather(sr_sc).astype(BF16), gather(si_sc).astype(BF16)], axis=1)
            cols = slice(hf * _SSM_HALF, (hf + 1) * _SSM_HALF)
            y_ref[b, :, cols] = _dot(hs, ch_ref[hf]) + d_ref[:, cols] * u_ref[b, :, cols]


def _ssm_prompt(u, sp, tc=128):
    B, T, W = u.shape
    ar, ai, bh, bl, ch, cl, d = sp
    rows = 2 * B
    nlb = _SSM_ROW // LANE
    tiles = lambda a: jnp.tile(a, (B, 1)).reshape(rows, nlb, LANE).transpose(1, 0, 2)
    const = lambda a: pl.BlockSpec(a.shape, lambda t: (0,) * a.ndim)
    blk = pl.BlockSpec((B, tc, W), lambda t: (0, t, 0))
    st = pl.BlockSpec((nlb, rows, LANE), lambda t: (0, 0, 0))
    ins = [tiles(ar), tiles(ai), bh, ch, d]
    st_shape = jax.ShapeDtypeStruct((nlb, rows, LANE), F32)
    y, h_re, h_im = pl.pallas_call(
        functools.partial(_ssm_prompt_kernel, tc=tc, nb=B),
        grid=(T // tc,),
        in_specs=[blk] + [const(a) for a in ins],
        out_specs=[blk, st, st],
        out_shape=[jax.ShapeDtypeStruct((B, T, W), F32), st_shape, st_shape],
        scratch_shapes=[pltpu.VMEM((nlb, tc * rows, LANE), F32), pltpu.VMEM((nlb, tc * rows, LANE), F32),
                        pltpu.VMEM((nlb, rows, LANE), F32), pltpu.VMEM((nlb, rows, LANE), F32)],
        compiler_params=_params(("arbitrary",)),
        name="ssm_prompt",
    )(u, *ins)
    rows_major = lambda a: a.transpose(1, 0, 2).reshape(rows, _SSM_ROW)
    return y, rows_major(h_re), rows_major(h_im)


def _ssm_sample_kernel(u_ref, h0r_ref, h0i_ref, ar_ref, ai_ref, bh_ref, bl_ref, ch_ref, cl_ref, d_ref,
                       y_ref, hr_ref, hi_ref):
    for hf in range(2):
        cols = slice(hf * _SSM_HALF, (hf + 1) * _SSM_HALF)
        lanes = slice(hf * _SSM_ROW, (hf + 1) * _SSM_ROW)
        u = u_ref[:, cols]
        u_hi, u_lo = _split(u)
        bu = _dot3(u_hi, u_lo, bh_ref[hf], bl_ref[hf])
        ar = ar_ref[hf:hf + 1, :]
        ai = ai_ref[hf:hf + 1, :]
        h_re = ar * h0r_ref[:, lanes] - ai * h0i_ref[:, lanes] + bu[:, :_SSM_ROW]
        h_im = ar * h0i_ref[:, lanes] + ai * h0r_ref[:, lanes] + bu[:, _SSM_ROW:]
        hr_ref[:, lanes] = h_re
        hi_ref[:, lanes] = h_im
        r_hi, r_lo = _split(h_re)
        i_hi, i_lo = _split(h_im)
        y = (_dot3(r_hi, r_lo, ch_ref[hf, :_SSM_ROW], cl_ref[hf, :_SSM_ROW])
             + _dot3(i_hi, i_lo, ch_ref[hf, _SSM_ROW:], cl_ref[hf, _SSM_ROW:]))
        y_ref[:, cols] = y + d_ref[:, cols] * u


def _ssm_sample(u, h0r, h0i, sp):
    n = u.shape[0]
    ins = [u, h0r, h0i, *sp]
    full = lambda a: pl.BlockSpec(a.shape, lambda i: (0,) * a.ndim)
    outs = [jax.ShapeDtypeStruct((n, SSM_WIDTH), F32), jax.ShapeDtypeStruct(h0r.shape, F32),
            jax.ShapeDtypeStruct(h0r.shape, F32)]
    return pl.pallas_call(
        _ssm_sample_kernel, grid=(1,),
        in_specs=[full(a) for a in ins], out_specs=[full(a) for a in outs], out_shape=outs,
        compiler_params=_params(("arbitrary",)),
        name="ssm_sample",
    )(*ins)


def _merge_kernel(x_ref, o_ref, y_ref, gm_ref, wa_ref, ws_ref, wo_ref, gf_ref, y2_ref, hf_ref):
    a = _dot(o_ref[...].astype(BF16), wa_ref[...])
    gl = _dot(jax.nn.gelu(y_ref[...]).astype(BF16), ws_ref[...])
    s = gl[:, :D_MODEL] * jax.nn.sigmoid(gl[:, D_MODEL:])
    gm = gm_ref[...]
    m = jax.nn.sigmoid(gm[:, :D_MODEL]) * a + jax.nn.sigmoid(gm[:, D_MODEL:]) * s
    y2 = x_ref[...] + _dot(m.astype(BF16), wo_ref[...])
    y2_ref[...] = y2
    hf_ref[...] = _rms(y2, gf_ref[...])


def _merge(x, o, y, gm, wa, ws, wo, gf, tm):
    n = x.shape[0]
    tok = lambda w: pl.BlockSpec((tm, w), lambda i: (i, 0))
    const = lambda a: pl.BlockSpec(a.shape, lambda i: (0,) * a.ndim)
    return pl.pallas_call(
        _merge_kernel, grid=(n // tm,),
        in_specs=[tok(D_MODEL), tok(ATT_WIDTH), tok(SSM_WIDTH), tok(2 * D_MODEL), const(wa), const(ws), const(wo),
                  const(gf)],
        out_specs=[tok(D_MODEL), tok(D_MODEL)],
        out_shape=[jax.ShapeDtypeStruct((n, D_MODEL), F32)] * 2,
        compiler_params=_params(("parallel",)),
        name="merge",
    )(x, o, y, gm, wa, ws, wo, gf)


def _cand_pairs():
    return [(a, b) for a in range(PEER_TOPK) for b in range(PEER_TOPK) if (a + 1) * (b + 1) <= PEER_TOPK]


def _top_values(s, k):
    vals = []
    for _ in range(k):
        mx = jnp.max(s, axis=0, keepdims=True)
        vals.append(mx)
        s = jnp.where(s == mx, LOWEST, s)
    return vals


def _peer_route_kernel(hf_ref, wqh_ref, wql_ref, k1h_ref, k1l_ref, k2h_ref, k2l_ref,
                       c1_ref, e1_ref, s2_ref, e2_ref, hh_sc, hl_sc):
    half = PEER_QDIM // 2

    @pl.when(pl.program_id(1) == 0)
    def _():
        hh_sc[...], hl_sc[...] = _split(hf_ref[...])

    q = _dot3(hh_sc[...], hl_sc[...], wqh_ref[...], wql_ref[...])
    q1h, q1l = _split(q[:, :half])
    q2h, q2l = _split(q[:, half:])
    s1 = _dot3_nt(k1h_ref[...], k1l_ref[...], q1h, q1l)
    s2 = _dot3_nt(k2h_ref[...], k2l_ref[...], q2h, q2l)
    v1 = _top_values(s1, PEER_TOPK)
    v2 = _top_values(s2, PEER_TOPK)
    cand = jnp.concatenate([v1[a] + v2[b] for a, b in _cand_pairs()], axis=0)
    thr = _top_values(cand, PEER_TOPK)[-1]
    top = v1[0] + v2[0]
    z = jnp.sum(jnp.where(cand >= thr, jnp.exp(cand - top), 0.0), axis=0, keepdims=True)
    c1 = jnp.full(s1.shape, -LOWEST, F32)
    for v in v2:
        c1 = jnp.where(s1 + v >= thr, v, c1)
    c1_ref[0] = c1
    s2_ref[0] = s2
    e1_ref[0] = jnp.exp(s1 - v1[0])
    e2_ref[0] = jnp.exp(s2 - v2[0]) / z


def _peer_route(hf, wq_hi, wq_lo, k1, k2, tn):
    n = hf.shape[0]
    k1h, k1l = _split(k1)
    k2h, k2l = _split(k2)
    const = lambda a: pl.BlockSpec(a.shape, lambda i, h: (0, 0))
    wq = pl.BlockSpec((D_MODEL, PEER_QDIM), lambda i, h: (0, h))
    keyed = pl.BlockSpec((1, PEER_NKEYS, tn), lambda i, h: (h, 0, i))
    shp = jax.ShapeDtypeStruct((PEER_HEADS, PEER_NKEYS, n), F32)
    return pl.pallas_call(
        _peer_route_kernel, grid=(n // tn, PEER_HEADS),
        in_specs=[pl.BlockSpec((tn, D_MODEL), lambda i, h: (i, 0)), wq, wq, const(k1h), const(k1l), const(k2h),
                  const(k2l)],
        out_specs=[keyed, keyed, keyed, keyed],
        out_shape=[shp, shp, shp, shp],
        scratch_shapes=[pltpu.VMEM((tn, D_MODEL), BF16), pltpu.VMEM((tn, D_MODEL), BF16)],
        compiler_params=_params(("parallel", "arbitrary")),
        name="peer_route",
    )(hf, wq_hi, wq_lo, k1h, k1l, k2h, k2l)


_PEER_JROWS = 32
_PEER_SLABS = 4


def _gelu_tanh(x):
    k = math.sqrt(2.0 / math.pi)
    hx = 0.5 * x
    return hx + hx * jnp.tanh(x * (k + (k * 0.044715) * (x * x)))


def _peer_main_kernel(hf_ref, y2_ref, u_ref, vt_ref, c1_ref, e1_ref, s2_ref, e2_ref, gn_ref, out_ref,
                      acc_sc, w_sc, act_sc, hfb_sc, *, tn, n_slab):
    e = pl.program_id(1)

    @pl.when(e == 0)
    def _():
        acc_sc[...] = jnp.zeros_like(acc_sc)
        hfb_sc[...] = hf_ref[...].astype(BF16)

    act_sc[...] = _gelu_tanh(_dot_nt(u_ref[...], hfb_sc[...]))

    for lc in range(tn // LANE):
        ln = slice(lc * LANE, (lc + 1) * LANE)

        def rows(jq, carry, ln=ln):
            j0 = pl.multiple_of(jq * _PEER_JROWS, _PEER_JROWS)
            for i0 in range(0, n_slab, _PEER_SLABS):
                slabs = range(i0, i0 + _PEER_SLABS)
                w = {ii: jnp.zeros((_PEER_JROWS, LANE), F32) for ii in slabs}
                for h in range(PEER_HEADS):
                    s2 = s2_ref[h, pl.ds(j0, _PEER_JROWS), ln]
                    e2 = e2_ref[h, pl.ds(j0, _PEER_JROWS), ln]
                    for ii in slabs:
                        w[ii] = w[ii] + e1_ref[h, ii:ii + 1, ln] * jnp.where(s2 >= c1_ref[h, ii:ii + 1, ln], e2, 0.0)
                for ii in slabs:
                    rs = pl.ds(ii * PEER_NKEYS + j0, _PEER_JROWS)
                    w_sc[rs, ln] = (w[ii] * act_sc[rs, ln]).astype(BF16)
            return carry

        lax.fori_loop(0, PEER_NKEYS // _PEER_JROWS, rows, 0)

    acc_sc[...] += _dot(vt_ref[...], w_sc[...])

    @pl.when(e == pl.num_programs(1) - 1)
    def _():
        out_ref[...] = _rms(y2_ref[...] + acc_sc[...].T, gn_ref[...])


def _peer_main(hf, y2, u_tab, vt_tab, c1, e1, s2, e2, gn, tn, n_slab=8):
    n = hf.shape[0]
    ec = n_slab * PEER_NKEYS
    n_exp = u_tab.shape[0]
    tok = pl.BlockSpec((tn, D_MODEL), lambda i, e: (i, 0))
    slab = pl.BlockSpec((PEER_HEADS, n_slab, tn), lambda i, e: (0, e, i))
    keyed = pl.BlockSpec((PEER_HEADS, PEER_NKEYS, tn), lambda i, e: (0, 0, i))
    return pl.pallas_call(
        functools.partial(_peer_main_kernel, tn=tn, n_slab=n_slab),
        grid=(n // tn, n_exp // ec),
        in_specs=[tok, tok, pl.BlockSpec((ec, D_MODEL), lambda i, e: (e, 0)),
                  pl.BlockSpec((D_MODEL, ec), lambda i, e: (0, e)), slab, slab, keyed, keyed,
                  pl.BlockSpec(gn.shape, lambda i, e: (0, 0))],
        out_specs=tok,
        out_shape=jax.ShapeDtypeStruct((n, D_MODEL), F32),
        scratch_shapes=[pltpu.VMEM((D_MODEL, tn), F32), pltpu.VMEM((ec, tn), BF16), pltpu.VMEM((ec, tn), F32),
                        pltpu.VMEM((tn, D_MODEL), BF16)],
        compiler_params=_params(("parallel", "arbitrary")),
        name="peer_main",
    )(hf, y2, u_tab, vt_tab, c1, e1, s2, e2, gn)


def _peer(hf, y2, wq_hi, wq_lo, k1, k2, u_tab, vt_tab, gn, tn):
    c1, e1, s2, e2 = _peer_route(hf, wq_hi, wq_lo, k1, k2, tn)
    return _peer_main(hf, y2, u_tab, vt_tab, c1, e1, s2, e2, gn, tn)


def _rope_tables(pos):
    half = HEAD_DIM // 2
    inv = ROPE_THETA ** (-jnp.arange(half, dtype=F32) / half)
    ang = pos.astype(F32)[:, None] * inv[None, :]
    cos, sin = jnp.cos(ang), jnp.sin(ang)
    return jnp.tile(jnp.concatenate([cos, cos], -1), (1, 2)), jnp.tile(jnp.concatenate([-sin, sin], -1), (1, 2))


def _proj_weight(w):
    gates = jnp.pad(w[:, 1280:1304], ((0, 0), (0, LANE - 3 * N_HEADS)))
    return jnp.concatenate([w[:, :1280], gates, w[:, 1304:1816], w[:, 1816:]], axis=1)


def _compress_weights(pe, w1, w2):
    r = CMP_LEN // CMP_STRIDE
    eye = jnp.eye(N_KV, dtype=F32)
    w1r = w1.reshape(r, CMP_STRIDE, HEAD_DIM, w1.shape[-1])
    wc = jnp.einsum('jsdh,gk->sgdjkh', w1r, eye).reshape(CMP_STRIDE * KV_WIDTH, r * N_KV * w1.shape[-1])
    pe2 = jnp.broadcast_to(pe.reshape(r, CMP_STRIDE, 1, HEAD_DIM), (r, CMP_STRIDE, N_KV, HEAD_DIM))
    pe2 = jnp.pad(pe2.reshape(r, CMP_STRIDE * KV_WIDTH), ((0, 8 - r), (0, 0)))
    w2bd = jnp.einsum('hd,gk->ghkd', w2, eye).reshape(N_KV * w2.shape[0], KV_WIDTH)
    return wc.astype(BF16), pe2, w2bd.astype(BF16)


def kernel(x_prompt, x_sample, cache_k_cmp, cache_v_cmp, cache_k_sel, cache_v_sel, cache_k_win, cache_v_win,
           state_ssm_re, state_ssm_im, page_table, norm_mix, w_in, cmp_pe_k, cmp_w1_k, cmp_w2_k, cmp_pe_v, cmp_w1_v,
           cmp_w2_v, ssm_a_re, ssm_a_im, ssm_log_dt, ssm_b_re, ssm_b_im, ssm_c_re, ssm_c_im, ssm_d, w_att_proj,
           w_ssm_glu, w_out, norm_ffn, peer_w_q, peer_keys1, peer_keys2, peer_u, peer_v, norm_final):
    assert w_in.shape[0] == 1, "single layer"
    B, T, _ = x_prompt.shape
    DB = x_sample.shape[0]
    n_pages = page_table.shape[1]
    past = n_pages * PAGE_SIZE
    n_pool = cache_k_cmp.shape[1]

    w_hi, w_lo = _split(_proj_weight(w_in[0]))
    g_mix = norm_mix[0].reshape(1, D_MODEL)
    cw = _compress_weights(cmp_pe_k[0], cmp_w1_k[0], cmp_w2_k[0]) + _compress_weights(cmp_pe_v[0], cmp_w1_v[0],
                                                                                       cmp_w2_v[0])
    sp = _ssm_params(ssm_a_re[0], ssm_a_im[0], ssm_log_dt[0], ssm_b_re[0], ssm_b_im[0], ssm_c_re[0], ssm_c_im[0],
                     ssm_d[0])
    wa, ws, wo = w_att_proj[0].astype(BF16), w_ssm_glu[0].astype(BF16), w_out[0].astype(BF16)
    g_ffn = norm_ffn[0].reshape(1, D_MODEL)
    g_fin = norm_final.reshape(1, D_MODEL)
    wq_hi, wq_lo = _split(peer_w_q[0])
    u_tab = peer_u[0].astype(BF16)
    vt_tab = peer_v[0].T.astype(BF16)

    cos_p, sin_p = _rope_tables(jnp.arange(T, dtype=jnp.int32))
    q, kc, vc, ks, vs, kw, vw, gt, u, gm = _project(x_prompt, g_mix, cos_p, sin_p, [w_hi], 256, False)
    chunks = lambda a: a.reshape(B, T // CMP_STRIDE, CMP_STRIDE * KV_WIDTH)
    ck, cv = _compress_prompt(chunks(kc), chunks(vc), cw)
    o_att = _attn_prompt(q, gt, ck, cv, ks, vs, kw, vw)
    y_ssm, hp_re, hp_im = _ssm_prompt(u, sp)
    n_p = B * T
    flat = lambda a: a.reshape(n_p, a.shape[-1])
    y2_p, hf_p = _merge(flat(x_prompt), flat(o_att), flat(y_ssm), flat(gm), wa, ws, wo, g_ffn, 256)
    y_prompt = _peer(hf_p, y2_p, wq_hi, wq_lo, peer_keys1[0], peer_keys2[0], u_tab, vt_tab, g_fin, 512)

    cos_s, sin_s = _rope_tables(jnp.full((DB,), past, jnp.int32))
    xs = x_sample.reshape(1, DB, D_MODEL)
    qs, kcs, vcs, kss, vss, kws, vws, gts, us, gms = [a[0] for a in
                                                      _project(xs, g_mix, cos_s, sin_s, [w_hi, w_lo], DB, True)]
    row3 = lambda a: a.reshape(DB, 1, KV_WIDTH)
    native = lambda c: jnp.transpose(c[0], (0, 2, 3, 1)).reshape(c.shape[1], KV_WIDTH, c.shape[2])
    cks, cvs = _compress_sample(page_table, native(cache_k_cmp), native(cache_v_cmp), row3(kcs), row3(vcs), cw)
    q5 = qs.reshape(DB, N_KV, Q_PER_KV, 1, HEAD_DIM) * jnp.eye(N_KV, dtype=F32).reshape(1, N_KV, 1, N_KV, 1)
    qbd = q5.reshape(DB, N_HEADS, KV_WIDTH)
    n_blocks = -(-(past + 1) // SEL_BLOCK)
    ns_pad = -(-n_blocks // LANE) * LANE
    oc, sel = _attn_sample_cmp(qbd, cks, cvs, past, n_blocks, ns_pad)
    gt3 = gts[:, :3 * N_HEADS].reshape(DB, N_HEADS, 3)
    o_s = _attn_sample(page_table, qbd, sel, oc, gt3, row3(kss), row3(vss), native(cache_k_win), native(cache_v_win),
                       row3(kws), row3(vws), native(cache_k_sel), native(cache_v_sel), past)
    h0r = state_ssm_re[0].reshape(DB, SSM_GROUPS * SSM_STATE)
    h0i = state_ssm_im[0].reshape(DB, SSM_GROUPS * SSM_STATE)
    ys_ssm, hs_re, hs_im = _ssm_sample(us, h0r, h0i, sp)
    y2_s, hf_s = _merge(x_sample.reshape(DB, D_MODEL), o_s.reshape(DB, ATT_WIDTH), ys_ssm, gms, wa, ws, wo, g_ffn, DB)
    y_sample = _peer(hf_s, y2_s, wq_hi, wq_lo, peer_keys1[0], peer_keys2[0], u_tab, vt_tab, g_fin, DB)

    kv5 = lambda a, n: a.reshape(1, n, -1, N_KV, HEAD_DIM)
    wb = min(WINDOW, T)
    st = lambda a, n: a.reshape(1, n, SSM_GROUPS, SSM_STATE)
    wbuf = cache_k_win.shape[2]
    nw = min(WINDOW, wbuf + 1)
    win_s = lambda old, new: jnp.concatenate([old[0].reshape(DB, wbuf, KV_WIDTH), new.reshape(DB, 1, KV_WIDTH)],
                                             axis=1)[:, wbuf + 1 - nw:]
    return (y_prompt.reshape(B, T, D_MODEL), y_sample.reshape(DB, 1, D_MODEL),
            kv5(kc, B), kv5(vc, B), kv5(ks, B), kv5(vs, B), kv5(kw[:, T - wb:], B), kv5(vw[:, T - wb:], B),
            st(hp_re, B), st(hp_im, B),
            kv5(kcs, DB), kv5(vcs, DB), kv5(kss, DB), kv5(vss, DB),
            kv5(win_s(cache_k_win, kws), DB), kv5(win_s(cache_v_win, vws), DB),
            st(hs_re, DB), st(hs_im, DB))
```

```python
import functools
import math

import jax
import jax.numpy as jnp
from jax import lax
from jax.experimental import pallas as pl
from jax.experimental.pallas import tpu as pltpu

F32 = jnp.float32
BF16 = jnp.bfloat16

D_MODEL = 1024
HEAD_DIM = 64
N_HEADS = 8
N_KV = 2
Q_PER_KV = 4
ATT_WIDTH = 512
KV_WIDTH = 128
CMP_LEN = 32
CMP_STRIDE = 16
SEL_BLOCK = 64
SEL_TOPN = 16
WINDOW = 512
ROPE_THETA = 10000.0
PAGE_SIZE = 128
SSM_GROUP = 16
SSM_WIDTH = 512
SSM_GROUPS = 32
SSM_STATE = 64
PEER_HEADS = 8
PEER_NKEYS = 128
PEER_QDIM = 256
PEER_TOPK = 16
RMS_EPS = 1e-6
NEG = -1e30
BIG = 1e9
TINY = 1e-30
LOWEST = -3.0e38

LANE = 128
VMEM_LIMIT = 56 * 1024 * 1024

_NT = (((1,), (1,)), ((), ()))


def _params(sem, vmem=VMEM_LIMIT):
    return pltpu.CompilerParams(dimension_semantics=sem, vmem_limit_bytes=vmem)


def _split(x):
    hi = x.astype(BF16)
    lo = (x - hi.astype(F32)).astype(BF16)
    return hi, lo


def _dot(a, b):
    return jnp.dot(a, b, preferred_element_type=F32)


def _dot_nt(a, b):
    return lax.dot_general(a, b, _NT, preferred_element_type=F32)


def _dot3(a_hi, a_lo, b_hi, b_lo):
    return _dot(a_hi, b_hi) + (_dot(a_hi, b_lo) + _dot(a_lo, b_hi))


def _dot3_nt(a_hi, a_lo, b_hi, b_lo):
    return _dot_nt(a_hi, b_hi) + (_dot_nt(a_hi, b_lo) + _dot_nt(a_lo, b_hi))


def _rms(x, g):
    return x * lax.rsqrt(jnp.mean(x * x, axis=-1, keepdims=True) + RMS_EPS) * g


def _softmax_rows(s, mask):
    s = jnp.where(mask, s, NEG)
    e = jnp.where(mask, jnp.exp(s - jnp.max(s, axis=-1, keepdims=True)), 0.0)
    return e / jnp.maximum(jnp.sum(e, axis=-1, keepdims=True), TINY)


_SEG = {'q': (0, 512), 'kc': (512, 640), 'vc': (640, 768), 'ks': (768, 896), 'vs': (896, 1024),
        'kw': (1024, 1152), 'vw': (1152, 1280), 'gt': (1280, 1408), 'u': (1408, 1920), 'gm': (1920, 3968)}
_PROJ_COLS = 3968


def _proj_kernel(*refs, precise):
    x_ref, g_ref, cos_ref, sin_ref = refs[:4]
    n_w = 2 if precise else 1
    w_refs = refs[4:4 + n_w]
    q_ref, kc_ref, vc_ref, ks_ref, vs_ref, kw_ref, vw_ref, gt_ref, u_ref, gm_ref = refs[4 + n_w:]
    h = _rms(x_ref[0], g_ref[...])
    h_hi = h.astype(BF16)
    h_lo = (h - h_hi.astype(F32)).astype(BF16) if precise else None

    def mm(c0, c1):
        z = _dot(h_hi, w_refs[0][:, c0:c1])
        if precise:
            z = z + (_dot(h_hi, w_refs[1][:, c0:c1]) + _dot(h_lo, w_refs[0][:, c0:c1]))
        return z

    cos = cos_ref[...]
    sin = sin_ref[...]
    first = (lax.broadcasted_iota(jnp.int32, (1, LANE), 1) % HEAD_DIM) < (HEAD_DIM // 2)

    def rope(z):
        rot = jnp.where(first, pltpu.roll(z, LANE - HEAD_DIM // 2, 1), pltpu.roll(z, HEAD_DIM // 2, 1))
        return z * cos + rot * sin

    for i in range(4):
        q_ref[0, :, i * LANE:(i + 1) * LANE] = rope(mm(i * LANE, (i + 1) * LANE))
    kc_ref[0] = rope(mm(*_SEG['kc']))
    vc_ref[0] = mm(*_SEG['vc'])
    ks_ref[0] = rope(mm(*_SEG['ks']))
    vs_ref[0] = mm(*_SEG['vs'])
    kw_ref[0] = rope(mm(*_SEG['kw']))
    vw_ref[0] = mm(*_SEG['vw'])
    gt_ref[0] = mm(*_SEG['gt'])
    u_ref[0] = mm(*_SEG['u'])
    for i in range(4):
        c0 = _SEG['gm'][0] + i * 512
        gm_ref[0, :, i * 512:(i + 1) * 512] = mm(c0, c0 + 512)


def _project(x, gain, cos, sin, w_list, tm, precise):
    B, T, D = x.shape
    widths = [512, 128, 128, 128, 128, 128, 128, 128, 512, 2048]
    tok = lambda w: pl.BlockSpec((1, tm, w), lambda b, t: (b, t, 0))
    const = lambda a: pl.BlockSpec(a.shape, lambda b, t: (0,) * a.ndim)
    return pl.pallas_call(
        functools.partial(_proj_kernel, precise=precise),
        grid=(B, T // tm),
        in_specs=[tok(D), const(gain), pl.BlockSpec((tm, LANE), lambda b, t: (t, 0)),
                  pl.BlockSpec((tm, LANE), lambda b, t: (t, 0))] + [const(w) for w in w_list],
        out_specs=[tok(w) for w in widths],
        out_shape=[jax.ShapeDtypeStruct((B, T, w), F32) for w in widths],
        compiler_params=_params(("parallel", "arbitrary")),
        name="proj",
    )(x, gain, cos, sin, *w_list)


def _compress_rows(x, last_p1, wc_ref, pe_ref, w2_ref):
    C = x.shape[0]
    p = _dot(x.astype(BF16), wc_ref[...])
    pb = _dot(pe_ref[...].astype(BF16), wc_ref[...])
    bias = pb[0:1, :LANE] + pb[1:2, LANE:]
    p1 = pltpu.roll(p[:, LANE:], C - 1, 0)
    if last_p1 is not None:
        row = lax.broadcasted_iota(jnp.int32, (C, 1), 0)
        p1 = jnp.where(row == C - 1, last_p1, p1)
    hid = p[:, :LANE] + p1 + bias
    return _dot(jax.nn.gelu(hid).astype(BF16), w2_ref[...])


def _compress_prompt_kernel(xk_ref, xv_ref, wck_ref, pek_ref, w2k_ref, wcv_ref, pev_ref, w2v_ref, ck_ref, cv_ref):
    ck_ref[0] = _compress_rows(xk_ref[0], None, wck_ref, pek_ref, w2k_ref)
    cv_ref[0] = _compress_rows(xv_ref[0], None, wcv_ref, pev_ref, w2v_ref)


def _compress_prompt(xk, xv, cw):
    B, C, W = xk.shape
    seq = pl.BlockSpec((1, C, W), lambda b: (b, 0, 0))
    const = lambda a: pl.BlockSpec(a.shape, lambda b: (0,) * a.ndim)
    out = pl.BlockSpec((1, C, LANE), lambda b: (b, 0, 0))
    return pl.pallas_call(
        _compress_prompt_kernel,
        grid=(B,),
        in_specs=[seq, seq] + [const(a) for a in cw],
        out_specs=[out, out],
        out_shape=[jax.ShapeDtypeStruct((B, C, LANE), F32)] * 2,
        compiler_params=_params(("arbitrary",)),
        name="compress_prompt",
    )(xk, xv, *cw)


def _paged_fetch(pt_ref, pools, bufs, sems, n_pages, dst):
    b = pl.program_id(0)
    nb = pl.num_programs(0)
    slot = b % 2

    def copies(bb, sl):
        return [pltpu.make_async_copy(pool.at[pt_ref[bb, p]], dst(buf, sl, p), sem.at[sl])
                for pool, buf, sem in zip(pools, bufs, sems) for p in range(n_pages)]

    @pl.when(b == 0)
    def _():
        for cp in copies(b, slot):
            cp.start()

    @pl.when(b + 1 < nb)
    def _():
        for cp in copies(b + 1, 1 - slot):
            cp.start()

    for cp in copies(b, slot):
        cp.wait()
    return slot


def _compress_sample_kernel(pt_ref, kn_ref, vn_ref, perm_ref, wck_ref, pek_ref, w2k_ref, wcv_ref, pev_ref, w2v_ref,
                            kpool_ref, vpool_ref, ck_ref, cv_ref, kbuf, vbuf, ksem, vsem, taps_sc, *, n_pages):
    slot = _paged_fetch(pt_ref, (kpool_ref, vpool_ref), (kbuf, vbuf), (ksem, vsem), n_pages,
                        lambda buf, sl, p: buf.at[sl, p])
    cpp = PAGE_SIZE // CMP_STRIDE

    def one(buf, new_ref, wc_ref, pe_ref, w2_ref, out_ref):
        def flip(p, carry):
            rows = _dot_nt(perm_ref[...], buf[slot, p].astype(BF16))
            c0 = pl.multiple_of(p * cpp, cpp)
            for s in range(CMP_STRIDE):
                taps_sc[s, pl.ds(c0, cpp), :] = rows[s * cpp:(s + 1) * cpp]
            return carry

        lax.fori_loop(0, n_pages, flip, 0, unroll=32)
        x = jnp.concatenate([taps_sc[s] for s in range(CMP_STRIDE)], axis=1)
        new = jnp.broadcast_to(new_ref[0], (8, LANE)).astype(BF16)
        last_p1 = _dot(new, wc_ref[0:LANE, LANE:])[0:1]
        out_ref[0] = _compress_rows(x, last_p1, wc_ref, pe_ref, w2_ref)

    one(kbuf, kn_ref, wck_ref, pek_ref, w2k_ref, ck_ref)
    one(vbuf, vn_ref, wcv_ref, pev_ref, w2v_ref, cv_ref)


def _compress_sample(page_table, kpool, vpool, k_new, v_new, cw):
    DB, n_pages = page_table.shape
    C = n_pages * PAGE_SIZE // CMP_STRIDE
    new = pl.BlockSpec((1, 1, LANE), lambda b, pt: (b, 0, 0))
    const = lambda a: pl.BlockSpec(a.shape, lambda b, pt: (0,) * a.ndim)
    hbm = pl.BlockSpec(memory_space=pl.ANY)
    out = pl.BlockSpec((1, C, LANE), lambda b, pt: (b, 0, 0))
    page_buf = pltpu.VMEM((2, n_pages, KV_WIDTH, PAGE_SIZE), F32)
    cpp = PAGE_SIZE // CMP_STRIDE
    tok = jnp.arange(PAGE_SIZE)
    perm = ((tok[:, None] % cpp) * CMP_STRIDE + tok[:, None] // cpp == tok[None, :]).astype(BF16)
    return pl.pallas_call(
        functools.partial(_compress_sample_kernel, n_pages=n_pages),
        grid_spec=pltpu.PrefetchScalarGridSpec(
            num_scalar_prefetch=1, grid=(DB,),
            in_specs=[new, new, const(perm)] + [const(a) for a in cw] + [hbm, hbm],
            out_specs=[out, out],
            scratch_shapes=[page_buf, page_buf, pltpu.SemaphoreType.DMA((2,)), pltpu.SemaphoreType.DMA((2,)),
                            pltpu.VMEM((CMP_STRIDE, C, KV_WIDTH), F32)]),
        out_shape=[jax.ShapeDtypeStruct((DB, C, LANE), F32)] * 2,
        compiler_params=_params(("arbitrary",)),
        name="compress_sample",
    )(page_table, k_new, v_new, perm, *cw, kpool, vpool)


def _select_blocks(imp, qpos, n_real, axis=1):
    j_shape = (1, imp.shape[1]) if axis == 1 else (imp.shape[0], 1)
    j = lax.broadcasted_iota(jnp.int32, j_shape, axis)
    valid = j * SEL_BLOCK <= qpos
    cur = qpos // SEL_BLOCK
    force = (j == 0) | (j == cur) | (j == cur - 1)
    score = jnp.where(valid & force, BIG, jnp.where(valid, imp, -BIG))
    rank = jnp.zeros(imp.shape, F32)
    for jp in range(n_real):
        other = score[:, jp:jp + 1] if axis == 1 else score[jp:jp + 1, :]
        earlier = jnp.where(j > jp, 1.0, 0.0)
        rank = rank + jnp.where(other > score, 1.0, jnp.where(other == score, earlier, 0.0))
    return jnp.where(rank < SEL_TOPN, 1.0, 0.0)


def _attn_prompt_kernel(q_ref, gt_ref, ck_ref, cv_ref, ks_ref, vs_ref, kw_ref, vw_ref, msel_ref, exp_ref, o_ref,
                        m_sc, acc_sc, s_sc, *, tq, kc, wb):
    s0 = pl.program_id(1) * tq
    R = Q_PER_KV * tq
    ncb = ck_ref.shape[1]
    qpos_r = s0 + lax.broadcasted_iota(jnp.int32, (R, 1), 0) % tq
    qpos_t = s0 + lax.broadcasted_iota(jnp.int32, (tq, 1), 0)
    qpos_l = s0 + lax.broadcasted_iota(jnp.int32, (1, tq), 1)
    sig = jax.nn.sigmoid(gt_ref[0])
    stack = lambda f: jnp.concatenate([f(r) for r in range(Q_PER_KV)], axis=0)

    def rows_of(bias, r0, n):
        if n >= tq:
            return jnp.concatenate([bias] * (n // tq), axis=0)
        return bias[r0 % tq:r0 % tq + n]

    heads = []
    for g in range(N_KV):
        gl = slice(g * HEAD_DIM, (g + 1) * HEAD_DIM)
        qg = stack(lambda r: q_ref[0, :, (g * Q_PER_KV + r) * HEAD_DIM:(g * Q_PER_KV + r + 1) * HEAD_DIM])
        qg = qg * (HEAD_DIM ** -0.5)
        q_hi = qg.astype(BF16)

        s_c = _dot_nt(q_hi, ck_ref[0, :, gl].astype(BF16))
        cend = lax.broadcasted_iota(jnp.int32, (1, ncb), 1) * CMP_STRIDE + (CMP_LEN - 1)
        p_c = _softmax_rows(s_c, cend <= qpos_r)
        o_c = _dot(p_c.astype(BF16), cv_ref[0, :, gl].astype(BF16))
        p_sum = p_c[0:tq]
        for r in range(1, Q_PER_KV):
            p_sum = p_sum + p_c[r * tq:(r + 1) * tq]
        ps_hi, ps_lo = _split(p_sum)
        msel_t = msel_ref[...]
        imp_t = _dot_nt(msel_t, ps_hi) + _dot_nt(msel_t, ps_lo)
        sel = _select_blocks(imp_t, qpos_l, imp_t.shape[0], axis=0).T

        unpicked = ((sel - 1.0) * -NEG).astype(BF16)
        n_chunks = (s0 + tq + kc - 1) // kc

        m_sc[...] = jnp.full((R, LANE), NEG, F32)

        def sweep_max(c, carry):
            off = pl.multiple_of(c * kc, kc)
            k = ks_ref[0, pl.ds(off, kc), gl].astype(BF16)
            kpos = off + lax.broadcasted_iota(jnp.int32, (1, kc), 1)
            bias = _dot(unpicked, exp_ref[c]) + jnp.where(kpos <= qpos_t, 0.0, NEG)
            s = _dot_nt(q_hi, k) + rows_of(bias, 0, R)
            s_sc[c] = s
            mx = s[:, 0:LANE]
            for t in range(1, kc // LANE):
                mx = jnp.maximum(mx, s[:, t * LANE:(t + 1) * LANE])
            m_sc[...] = jnp.maximum(m_sc[...], mx)
            return carry

        lax.fori_loop(0, n_chunks, sweep_max, 0)
        m_sc[...] = jnp.broadcast_to(jnp.max(m_sc[...], axis=-1, keepdims=True), (R, LANE))

        acc_sc[...] = jnp.zeros((R, KV_WIDTH), F32)
        own = (lax.broadcasted_iota(jnp.int32, (1, KV_WIDTH), 1) // HEAD_DIM) == g
        ol = slice((1 - g) * HEAD_DIM, (1 - g) * HEAD_DIM + 1)
        with_ones = lambda v: jnp.where(own, v, 1.0).astype(BF16)

        def sweep_pv(c, carry):
            off = pl.multiple_of(c * kc, kc)
            p = jnp.exp(s_sc[c] - jnp.concatenate([m_sc[...]] * (kc // LANE), axis=1))
            acc_sc[...] += _dot(p.astype(BF16), with_ones(vs_ref[0, pl.ds(off, kc), :]))
            return carry

        lax.fori_loop(0, n_chunks, sweep_pv, 0)
        o_s = acc_sc[:, gl] / jnp.maximum(acc_sc[:, ol], TINY)

        wl = WINDOW + tq
        w0 = pl.multiple_of(jnp.maximum(s0 - WINDOW, 0), tq)
        kpos = w0 + lax.broadcasted_iota(jnp.int32, (1, wl), 1)
        dlt = qpos_t - kpos
        bias_w = jnp.where((dlt >= 0) & (dlt < WINDOW), 0.0, NEG)
        k_w = kw_ref[0, pl.ds(w0, wl), gl].astype(BF16)
        v_w = with_ones(vw_ref[0, pl.ds(w0, wl), :])
        o_w = []
        for r0 in range(0, R, wb):
            s_w = _dot_nt(q_hi[r0:r0 + wb], k_w) + rows_of(bias_w, r0, wb)
            p_w = jnp.exp(s_w - jnp.max(s_w, axis=-1, keepdims=True))
            ov = _dot(p_w.astype(BF16), v_w)
            o_w.append(ov[:, gl] / jnp.maximum(ov[:, ol], TINY))
        o_w = jnp.concatenate(o_w, axis=0)

        gate = lambda i: stack(lambda r: sig[:, (g * Q_PER_KV + r) * 3 + i:(g * Q_PER_KV + r) * 3 + i + 1])
        og = gate(0) * o_c + gate(1) * o_s + gate(2) * o_w
        heads += [og[r * tq:(r + 1) * tq] for r in range(Q_PER_KV)]
    o_ref[0] = jnp.concatenate(heads, axis=1)


def _attn_prompt(q, gt, ck, cv, ks, vs, kw, vw, tq=128, kc=512, wb=256):
    B, T, _ = q.shape
    assert T % kc == 0 and T % tq == 0 and T >= WINDOW + tq and (tq % wb == 0 or wb % tq == 0)
    ncb = ck.shape[1]
    ns = T // SEL_BLOCK
    i = jnp.arange(ncb)[None, :]
    j = jnp.arange(ns)[:, None]
    msel = ((i * CMP_STRIDE <= j * SEL_BLOCK + SEL_BLOCK - 1)
            & (i * CMP_STRIDE + CMP_LEN - 1 >= j * SEL_BLOCK)).astype(BF16)
    key_blk = (jnp.arange(T) // SEL_BLOCK).reshape(T // kc, 1, kc)
    expand = (key_blk == jnp.arange(ns)[None, :, None]).astype(BF16)
    tok = lambda w: pl.BlockSpec((1, tq, w), lambda b, t: (b, t, 0))
    seq = lambda a: pl.BlockSpec((1,) + a.shape[1:], lambda b, t: (b, 0, 0))
    const = lambda a: pl.BlockSpec(a.shape, lambda b, t: (0,) * a.ndim)
    R = Q_PER_KV * tq
    return pl.pallas_call(
        functools.partial(_attn_prompt_kernel, tq=tq, kc=kc, wb=wb),
        grid=(B, T // tq),
        in_specs=[tok(ATT_WIDTH), tok(LANE), seq(ck), seq(cv), seq(ks), seq(vs), seq(kw), seq(vw),
                  const(msel), const(expand)],
        out_specs=tok(ATT_WIDTH),
        out_shape=jax.ShapeDtypeStruct((B, T, ATT_WIDTH), F32),
        scratch_shapes=[pltpu.VMEM((R, LANE), F32), pltpu.VMEM((R, KV_WIDTH), F32), pltpu.VMEM((T // kc, R, kc), F32)],
        compiler_params=_params(("parallel", "arbitrary")),
        name="attn_prompt",
    )(q, gt, ck, cv, ks, vs, kw, vw, msel, expand)


def _group_lanes():
    row = lax.broadcasted_iota(jnp.int32, (N_HEADS, LANE), 0) // Q_PER_KV
    lane = lax.broadcasted_iota(jnp.int32, (N_HEADS, LANE), 1) // HEAD_DIM
    return row == lane


def _attn_sample_cmp_kernel(q_ref, ck_ref, cv_ref, msel_ref, oc_ref, sel_ref, *, qpos, n_blocks):
    q_hi, q_lo = _split(q_ref[0] * (HEAD_DIM ** -0.5))
    k_hi, k_lo = _split(ck_ref[0])
    s_c = _dot3_nt(q_hi, q_lo, k_hi, k_lo)
    ncb = s_c.shape[1]
    cend = lax.broadcasted_iota(jnp.int32, (1, ncb), 1) * CMP_STRIDE + (CMP_LEN - 1)
    p_c = _softmax_rows(s_c, cend <= qpos)
    o_c = _dot(p_c.astype(BF16), cv_ref[0].astype(BF16))
    oc_ref[0] = jnp.where(_group_lanes(), o_c, 0.0)
    top = jnp.sum(p_c[0:Q_PER_KV], axis=0, keepdims=True)
    bot = jnp.sum(p_c[Q_PER_KV:], axis=0, keepdims=True)
    row = lax.broadcasted_iota(jnp.int32, (N_HEADS, 1), 0)
    p_sum = jnp.where(row < Q_PER_KV, top, bot)
    ps_hi, ps_lo = _split(p_sum)
    msel = msel_ref[...]
    imp = _dot(ps_hi, msel) + _dot(ps_lo, msel)
    sel_ref[0] = _select_blocks(imp, jnp.full((N_HEADS, 1), qpos, jnp.int32), n_blocks)


def _attn_sample_cmp(qbd, ck, cv, qpos, n_blocks, ns_pad):
    DB, ncb, _ = ck.shape
    i = jnp.arange(ncb)[:, None]
    j = jnp.arange(ns_pad)[None, :]
    msel = ((i * CMP_STRIDE <= j * SEL_BLOCK + SEL_BLOCK - 1) & (i * CMP_STRIDE + CMP_LEN - 1 >= j * SEL_BLOCK)
            & (j < n_blocks)).astype(BF16)
    row = lambda a: pl.BlockSpec((1,) + a.shape[1:], lambda b: (b, 0, 0))
    return pl.pallas_call(
        functools.partial(_attn_sample_cmp_kernel, qpos=qpos, n_blocks=n_blocks),
        grid=(DB,),
        in_specs=[row(qbd), row(ck), row(cv), pl.BlockSpec(msel.shape, lambda b: (0, 0))],
        out_specs=[pl.BlockSpec((1, N_HEADS, LANE), lambda b: (b, 0, 0)),
                   pl.BlockSpec((1, N_HEADS, ns_pad), lambda b: (b, 0, 0))],
        out_shape=[jax.ShapeDtypeStruct((DB, N_HEADS, LANE), F32), jax.ShapeDtypeStruct((DB, N_HEADS, ns_pad), F32)],
        compiler_params=_params(("arbitrary",)),
        name="attn_sample_cmp",
    )(qbd, ck, cv, msel)


def _attn_sample_kernel(pt_ref, q_ref, sel_ref, oc_ref, gt_ref, ksn_ref, vsn_ref, kwb_ref, vwb_ref, kwn_ref, vwn_ref,
                        exp_ref, kpool_ref, vpool_ref, o_ref, kbuf, vbuf, ksem, vsem, *, n_pages, qpos):
    slot = _paged_fetch(pt_ref, (kpool_ref, vpool_ref), (kbuf, vbuf), (ksem, vsem), n_pages,
                        lambda buf, sl, p: buf.at[sl, :, pl.ds(p * PAGE_SIZE, PAGE_SIZE)])
    past = n_pages * PAGE_SIZE
    q = q_ref[0] * (HEAD_DIM ** -0.5)
    q_b = q.astype(BF16)
    sel = sel_ref[0]

    def attend(kt_old, vt_old, mask_old, k_new, v_new, mask_new):
        s = jnp.where(mask_old, _dot(q_b, kt_old.astype(BF16)), NEG)
        s_n = jnp.where(mask_new, jnp.sum(q * k_new, axis=-1, keepdims=True), NEG)
        m = jnp.maximum(jnp.max(s, axis=-1, keepdims=True), s_n)
        e = jnp.where(mask_old, jnp.exp(s - m), 0.0)
        e_n = jnp.where(mask_new, jnp.exp(s_n - m), 0.0)
        den = jnp.maximum(jnp.sum(e, axis=-1, keepdims=True) + e_n, TINY)
        return (_dot_nt(e.astype(BF16), vt_old.astype(BF16)) + e_n * v_new) / den

    picked = _dot(sel.astype(BF16), exp_ref[...])
    kpos = lax.broadcasted_iota(jnp.int32, (1, past), 1)
    nb_new = past // SEL_BLOCK
    o_s = attend(kbuf[slot], vbuf[slot], (picked > 0.5) & (kpos <= qpos), ksn_ref[0], vsn_ref[0],
                 (sel[:, nb_new:nb_new + 1] > 0.5) & (past <= qpos))
    wbuf = kwb_ref.shape[2]
    dlt = qpos - (past - wbuf + lax.broadcasted_iota(jnp.int32, (1, wbuf), 1))
    o_w = attend(kwb_ref[0], vwb_ref[0], (dlt >= 0) & (dlt < WINDOW), kwn_ref[0], vwn_ref[0],
                 jnp.full((N_HEADS, 1), (qpos - past >= 0) & (qpos - past < WINDOW)))
    sig = jax.nn.sigmoid(gt_ref[0])
    o = sig[:, 0:1] * oc_ref[0] + sig[:, 1:2] * o_s + sig[:, 2:3] * o_w
    o = jnp.where(_group_lanes(), o, 0.0)
    o_ref[0] = o[:, :HEAD_DIM] + o[:, HEAD_DIM:]


def _attn_sample(page_table, qbd, sel, oc, gt3, ksn, vsn, kwb, vwb, kwn, vwn, kpool, vpool, qpos):
    DB, n_pages = page_table.shape
    past = n_pages * PAGE_SIZE
    nsp = sel.shape[-1]
    expand = ((jnp.arange(past) // SEL_BLOCK)[None, :] == jnp.arange(nsp)[:, None]).astype(BF16)
    row = lambda a: pl.BlockSpec((1,) + a.shape[1:], lambda b, pt: (b, 0, 0))
    hbm = pl.BlockSpec(memory_space=pl.ANY)
    ins = [qbd, sel, oc, gt3, ksn, vsn, kwb, vwb, kwn, vwn]
    return pl.pallas_call(
        functools.partial(_attn_sample_kernel, n_pages=n_pages, qpos=qpos),
        grid_spec=pltpu.PrefetchScalarGridSpec(
            num_scalar_prefetch=1, grid=(DB,),
            in_specs=[row(a) for a in ins] + [pl.BlockSpec(expand.shape, lambda b, pt: (0, 0)), hbm, hbm],
            out_specs=pl.BlockSpec((1, N_HEADS, HEAD_DIM), lambda b, pt: (b, 0, 0)),
            scratch_shapes=[pltpu.VMEM((2, KV_WIDTH, past), F32), pltpu.VMEM((2, KV_WIDTH, past), F32),
                            pltpu.SemaphoreType.DMA((2,)), pltpu.SemaphoreType.DMA((2,))]),
        out_shape=jax.ShapeDtypeStruct((DB, N_HEADS, HEAD_DIM), F32),
        compiler_params=_params(("arbitrary",)),
        name="attn_sample",
    )(page_table, *ins, expand, kpool, vpool)


_SSM_HALF = SSM_WIDTH // 2
_SSM_ROW = SSM_GROUPS // 2 * SSM_STATE


def _ssm_params(a_re, a_im, log_dt, b_re, b_im, c_re, c_im, d):
    lam = lax.complex(a_re, a_im)
    step = jnp.exp(log_dt)[:, None]
    a_bar = jnp.exp(lam * step)
    b_bar = ((a_bar - 1.0) / lam)[..., None] * lax.complex(b_re, b_im)
    eye = jnp.eye(SSM_GROUPS // 2, dtype=F32)

    def b_mat(x):
        x = x.reshape(2, SSM_GROUPS // 2, SSM_STATE, SSM_GROUP)
        return jnp.einsum('hgpc,gk->hgckp', x, eye).reshape(2, _SSM_HALF, _SSM_ROW)

    def c_mat(x):
        x = x.reshape(2, SSM_GROUPS // 2, SSM_GROUP, SSM_STATE)
        return jnp.einsum('hgcp,gk->hgpkc', x, eye).reshape(2, _SSM_ROW, _SSM_HALF)

    bm = jnp.concatenate([b_mat(jnp.real(b_bar)), b_mat(jnp.imag(b_bar))], axis=2)
    cm = jnp.concatenate([c_mat(c_re), -c_mat(c_im)], axis=1)
    ar = jnp.real(a_bar).reshape(2, _SSM_ROW)
    ai = jnp.imag(a_bar).reshape(2, _SSM_ROW)
    bm_hi, bm_lo = _split(bm)
    cm_hi, cm_lo = _split(cm)
    return ar, ai, bm_hi, bm_lo, cm_hi, cm_lo, d.reshape(1, SSM_WIDTH)


def _ssm_prompt_kernel(u_ref, ar_ref, ai_ref, bh_ref, ch_ref, d_ref, y_ref, hr_ref, hi_ref,
                       sr_sc, si_sc, st_re, st_im, *, tc, nb):
    t = pl.program_id(0)
    rows = 2 * nb
    nlb = _SSM_ROW // LANE

    @pl.when(t == 0)
    def _():
        st_re[...] = jnp.zeros_like(st_re)
        st_im[...] = jnp.zeros_like(st_im)

    for b in range(nb):
        for hf in range(2):
            u_hi, u_lo = _split(u_ref[b, :, hf * _SSM_HALF:(hf + 1) * _SSM_HALF])
            bu = _dot(u_hi, bh_ref[hf]) + _dot(u_lo, bh_ref[hf])
            for k in range(nlb):
                sr_sc[k, pl.ds(b * 2 + hf, tc, stride=rows), :] = bu[:, k * LANE:(k + 1) * LANE]
                si_sc[k, pl.ds(b * 2 + hf, tc, stride=rows), :] = bu[:, _SSM_ROW + k * LANE:_SSM_ROW + (k + 1) * LANE]

    ar = ar_ref[...]
    ai = ai_ref[...]

    def step(i, carry):
        h_re, h_im = carry
        r0 = pl.multiple_of(i * rows, rows)
        n_re = ar * h_re - ai * h_im + sr_sc[:, pl.ds(r0, rows), :]
        n_im = ar * h_im + ai * h_re + si_sc[:, pl.ds(r0, rows), :]
        sr_sc[:, pl.ds(r0, rows), :] = n_re
        si_sc[:, pl.ds(r0, rows), :] = n_im
        return n_re, n_im

    h_re, h_im = lax.fori_loop(0, tc, step, (st_re[...], st_im[...]), unroll=4)
    st_re[...] = h_re
    st_im[...] = h_im
    hr_ref[...] = h_re
    hi_ref[...] = h_im

    for b in range(nb):
        for hf in range(2):
            gather = lambda sc: jnp.concatenate(
                [sc[k, pl.ds(b * 2 + hf, tc, stride=rows), :] for k in range(nlb)], axis=1)
            hs = jnp.concatenate([gather(sr_sc).astype(BF16), gather(si_sc).astype(BF16)], axis=1)
            cols = slice(hf * _SSM_HALF, (hf + 1) * _SSM_HALF)
            y_ref[b, :, cols] = _dot(hs, ch_ref[hf]) + d_ref[:, cols] * u_ref[b, :, cols]


def _ssm_prompt(u, sp, tc=128):
    B, T, W = u.shape
    ar, ai, bh, bl, ch, cl, d = sp
    rows = 2 * B
    nlb = _SSM_ROW // LANE
    tiles = lambda a: jnp.tile(a, (B, 1)).reshape(rows, nlb, LANE).transpose(1, 0, 2)
    const = lambda a: pl.BlockSpec(a.shape, lambda t: (0,) * a.ndim)
    blk = pl.BlockSpec((B, tc, W), lambda t: (0, t, 0))
    st = pl.BlockSpec((nlb, rows, LANE), lambda t: (0, 0, 0))
    ins = [tiles(ar), tiles(ai), bh, ch, d]
    st_shape = jax.ShapeDtypeStruct((nlb, rows, LANE), F32)
    y, h_re, h_im = pl.pallas_call(
        functools.partial(_ssm_prompt_kernel, tc=tc, nb=B),
        grid=(T // tc,),
        in_specs=[blk] + [const(a) for a in ins],
        out_specs=[blk, st, st],
        out_shape=[jax.ShapeDtypeStruct((B, T, W), F32), st_shape, st_shape],
        scratch_shapes=[pltpu.VMEM((nlb, tc * rows, LANE), F32), pltpu.VMEM((nlb, tc * rows, LANE), F32),
                        pltpu.VMEM((nlb, rows, LANE), F32), pltpu.VMEM((nlb, rows, LANE), F32)],
        compiler_params=_params(("arbitrary",)),
        name="ssm_prompt",
    )(u, *ins)
    rows_major = lambda a: a.transpose(1, 0, 2).reshape(rows, _SSM_ROW)
    return y, rows_major(h_re), rows_major(h_im)


def _ssm_sample_kernel(u_ref, h0r_ref, h0i_ref, ar_ref, ai_ref, bh_ref, bl_ref, ch_ref, cl_ref, d_ref,
                       y_ref, hr_ref, hi_ref):
    for hf in range(2):
        cols = slice(hf * _SSM_HALF, (hf + 1) * _SSM_HALF)
        lanes = slice(hf * _SSM_ROW, (hf + 1) * _SSM_ROW)
        u = u_ref[:, cols]
        u_hi, u_lo = _split(u)
        bu = _dot3(u_hi, u_lo, bh_ref[hf], bl_ref[hf])
        ar = ar_ref[hf:hf + 1, :]
        ai = ai_ref[hf:hf + 1, :]
        h_re = ar * h0r_ref[:, lanes] - ai * h0i_ref[:, lanes] + bu[:, :_SSM_ROW]
        h_im = ar * h0i_ref[:, lanes] + ai * h0r_ref[:, lanes] + bu[:, _SSM_ROW:]
        hr_ref[:, lanes] = h_re
        hi_ref[:, lanes] = h_im
        r_hi, r_lo = _split(h_re)
        i_hi, i_lo = _split(h_im)
        y = (_dot3(r_hi, r_lo, ch_ref[hf, :_SSM_ROW], cl_ref[hf, :_SSM_ROW])
             + _dot3(i_hi, i_lo, ch_ref[hf, _SSM_ROW:], cl_ref[hf, _SSM_ROW:]))
        y_ref[:, cols] = y + d_ref[:, cols] * u


def _ssm_sample(u, h0r, h0i, sp):
    n = u.shape[0]
    ins = [u, h0r, h0i, *sp]
    full = lambda a: pl.BlockSpec(a.shape, lambda i: (0,) * a.ndim)
    outs = [jax.ShapeDtypeStruct((n, SSM_WIDTH), F32), jax.ShapeDtypeStruct(h0r.shape, F32),
            jax.ShapeDtypeStruct(h0r.shape, F32)]
    return pl.pallas_call(
        _ssm_sample_kernel, grid=(1,),
        in_specs=[full(a) for a in ins], out_specs=[full(a) for a in outs], out_shape=outs,
        compiler_params=_params(("arbitrary",)),
        name="ssm_sample",
    )(*ins)


def _merge_kernel(x_ref, o_ref, y_ref, gm_ref, wa_ref, ws_ref, wo_ref, gf_ref, y2_ref, hf_ref):
    a = _dot(o_ref[...].astype(BF16), wa_ref[...])
    gl = _dot(jax.nn.gelu(y_ref[...]).astype(BF16), ws_ref[...])
    s = gl[:, :D_MODEL] * jax.nn.sigmoid(gl[:, D_MODEL:])
    gm = gm_ref[...]
    m = jax.nn.sigmoid(gm[:, :D_MODEL]) * a + jax.nn.sigmoid(gm[:, D_MODEL:]) * s
    y2 = x_ref[...] + _dot(m.astype(BF16), wo_ref[...])
    y2_ref[...] = y2
    hf_ref[...] = _rms(y2, gf_ref[...])


def _merge(x, o, y, gm, wa, ws, wo, gf, tm):
    n = x.shape[0]
    tok = lambda w: pl.BlockSpec((tm, w), lambda i: (i, 0))
    const = lambda a: pl.BlockSpec(a.shape, lambda i: (0,) * a.ndim)
    return pl.pallas_call(
        _merge_kernel, grid=(n // tm,),
        in_specs=[tok(D_MODEL), tok(ATT_WIDTH), tok(SSM_WIDTH), tok(2 * D_MODEL), const(wa), const(ws), const(wo),
                  const(gf)],
        out_specs=[tok(D_MODEL), tok(D_MODEL)],
        out_shape=[jax.ShapeDtypeStruct((n, D_MODEL), F32)] * 2,
        compiler_params=_params(("parallel",)),
        name="merge",
    )(x, o, y, gm, wa, ws, wo, gf)


def _cand_pairs():
    return [(a, b) for a in range(PEER_TOPK) for b in range(PEER_TOPK) if (a + 1) * (b + 1) <= PEER_TOPK]


def _top_values(s, k):
    row = lax.broadcasted_iota(jnp.int32, (k, 1), 0)

    def body(it, carry):
        s, vals = carry
        mx = jnp.max(s, axis=0, keepdims=True)
        return jnp.where(s == mx, LOWEST, s), jnp.where(row == it, mx, vals)

    _, vals = lax.fori_loop(0, k, body, (s, jnp.full((k, s.shape[1]), LOWEST, F32)))
    return [vals[a:a + 1] for a in range(k)]


def _peer_route_kernel(hf_ref, wqh_ref, wql_ref, k1h_ref, k1l_ref, k2h_ref, k2l_ref,
                       c1_ref, e1_ref, s2_ref, e2_ref, hh_sc, hl_sc):
    half = PEER_QDIM // 2

    @pl.when(pl.program_id(1) == 0)
    def _():
        hh_sc[...], hl_sc[...] = _split(hf_ref[...])

    q = _dot3(hh_sc[...], hl_sc[...], wqh_ref[...], wql_ref[...])
    q1h, q1l = _split(q[:, :half])
    q2h, q2l = _split(q[:, half:])
    s1 = _dot3_nt(k1h_ref[...], k1l_ref[...], q1h, q1l)
    s2 = _dot3_nt(k2h_ref[...], k2l_ref[...], q2h, q2l)
    v1 = _top_values(s1, PEER_TOPK)
    v2 = _top_values(s2, PEER_TOPK)
    cand = jnp.concatenate([v1[a] + v2[b] for a, b in _cand_pairs()], axis=0)
    thr = _top_values(cand, PEER_TOPK)[-1]
    top = v1[0] + v2[0]
    z = jnp.sum(jnp.where(cand >= thr, jnp.exp(cand - top), 0.0), axis=0, keepdims=True)
    c1 = jnp.full(s1.shape, -LOWEST, F32)
    for v in v2:
        c1 = jnp.where(s1 + v >= thr, v, c1)
    c1_ref[0] = c1
    s2_ref[0] = s2
    e1_ref[0] = jnp.exp(s1 - v1[0])
    e2_ref[0] = jnp.exp(s2 - v2[0]) / z


def _peer_route(hf, wq_hi, wq_lo, k1, k2, tn):
    n = hf.shape[0]
    k1h, k1l = _split(k1)
    k2h, k2l = _split(k2)
    const = lambda a: pl.BlockSpec(a.shape, lambda i, h: (0, 0))
    wq = pl.BlockSpec((D_MODEL, PEER_QDIM), lambda i, h: (0, h))
    keyed = pl.BlockSpec((1, PEER_NKEYS, tn), lambda i, h: (h, 0, i))
    shp = jax.ShapeDtypeStruct((PEER_HEADS, PEER_NKEYS, n), F32)
    return pl.pallas_call(
        _peer_route_kernel, grid=(n // tn, PEER_HEADS),
        in_specs=[pl.BlockSpec((tn, D_MODEL), lambda i, h: (i, 0)), wq, wq, const(k1h), const(k1l), const(k2h),
                  const(k2l)],
        out_specs=[keyed, keyed, keyed, keyed],
        out_shape=[shp, shp, shp, shp],
        scratch_shapes=[pltpu.VMEM((tn, D_MODEL), BF16), pltpu.VMEM((tn, D_MODEL), BF16)],
        compiler_params=_params(("parallel", "arbitrary")),
        name="peer_route",
    )(hf, wq_hi, wq_lo, k1h, k1l, k2h, k2l)


_PEER_JROWS = 32
_PEER_SLABS = 4


def _gelu_tanh(x):
    k = math.sqrt(2.0 / math.pi)
    hx = 0.5 * x
    return hx + hx * jnp.tanh(x * (k + (k * 0.044715) * (x * x)))


def _peer_main_kernel(hf_ref, y2_ref, u_ref, vt_ref, c1_ref, e1_ref, s2_ref, e2_ref, gn_ref, out_ref,
                      acc_sc, w_sc, act_sc, hfb_sc, *, tn, n_slab):
    e = pl.program_id(1)

    @pl.when(e == 0)
    def _():
        acc_sc[...] = jnp.zeros_like(acc_sc)
        hfb_sc[...] = hf_ref[...].astype(BF16)

    act_sc[...] = _gelu_tanh(_dot_nt(u_ref[...], hfb_sc[...]))

    for lc in range(tn // LANE):
        ln = slice(lc * LANE, (lc + 1) * LANE)

        def rows(jq, carry, ln=ln):
            j0 = pl.multiple_of(jq * _PEER_JROWS, _PEER_JROWS)
            for i0 in range(0, n_slab, _PEER_SLABS):
                slabs = range(i0, i0 + _PEER_SLABS)
                w = {ii: jnp.zeros((_PEER_JROWS, LANE), F32) for ii in slabs}
                for h in range(PEER_HEADS):
                    s2 = s2_ref[h, pl.ds(j0, _PEER_JROWS), ln]
                    e2 = e2_ref[h, pl.ds(j0, _PEER_JROWS), ln]
                    for ii in slabs:
                        w[ii] = w[ii] + e1_ref[h, ii:ii + 1, ln] * jnp.where(s2 >= c1_ref[h, ii:ii + 1, ln], e2, 0.0)
                for ii in slabs:
                    rs = pl.ds(ii * PEER_NKEYS + j0, _PEER_JROWS)
                    w_sc[rs, ln] = (w[ii] * act_sc[rs, ln]).astype(BF16)
            return carry

        lax.fori_loop(0, PEER_NKEYS // _PEER_JROWS, rows, 0)

    acc_sc[...] += _dot(vt_ref[...], w_sc[...])

    @pl.when(e == pl.num_programs(1) - 1)
    def _():
        out_ref[...] = _rms(y2_ref[...] + acc_sc[...].T, gn_ref[...])


def _peer_main(hf, y2, u_tab, vt_tab, c1, e1, s2, e2, gn, tn, n_slab=8):
    n = hf.shape[0]
    ec = n_slab * PEER_NKEYS
    n_exp = u_tab.shape[0]
    tok = pl.BlockSpec((tn, D_MODEL), lambda i, e: (i, 0))
    slab = pl.BlockSpec((PEER_HEADS, n_slab, tn), lambda i, e: (0, e, i))
    keyed = pl.BlockSpec((PEER_HEADS, PEER_NKEYS, tn), lambda i, e: (0, 0, i))
    return pl.pallas_call(
        functools.partial(_peer_main_kernel, tn=tn, n_slab=n_slab),
        grid=(n // tn, n_exp // ec),
        in_specs=[tok, tok, pl.BlockSpec((ec, D_MODEL), lambda i, e: (e, 0)),
                  pl.BlockSpec((D_MODEL, ec), lambda i, e: (0, e)), slab, slab, keyed, keyed,
                  pl.BlockSpec(gn.shape, lambda i, e: (0, 0))],
        out_specs=tok,
        out_shape=jax.ShapeDtypeStruct((n, D_MODEL), F32),
        scratch_shapes=[pltpu.VMEM((D_MODEL, tn), F32), pltpu.VMEM((ec, tn), BF16), pltpu.VMEM((ec, tn), F32),
                        pltpu.VMEM((tn, D_MODEL), BF16)],
        compiler_params=_params(("parallel", "arbitrary")),
        name="peer_main",
    )(hf, y2, u_tab, vt_tab, c1, e1, s2, e2, gn)


def _peer(hf, y2, wq_hi, wq_lo, k1, k2, u_tab, vt_tab, gn, tn):
    c1, e1, s2, e2 = _peer_route(hf, wq_hi, wq_lo, k1, k2, tn)
    return _peer_main(hf, y2, u_tab, vt_tab, c1, e1, s2, e2, gn, tn)


def _rope_tables(pos):
    half = HEAD_DIM // 2
    inv = ROPE_THETA ** (-jnp.arange(half, dtype=F32) / half)
    ang = pos.astype(F32)[:, None] * inv[None, :]
    cos, sin = jnp.cos(ang), jnp.sin(ang)
    return jnp.tile(jnp.concatenate([cos, cos], -1), (1, 2)), jnp.tile(jnp.concatenate([-sin, sin], -1), (1, 2))


def _proj_weight(w):
    gates = jnp.pad(w[:, 1280:1304], ((0, 0), (0, LANE - 3 * N_HEADS)))
    return jnp.concatenate([w[:, :1280], gates, w[:, 1304:1816], w[:, 1816:]], axis=1)


def _compress_weights(pe, w1, w2):
    r = CMP_LEN // CMP_STRIDE
    eye = jnp.eye(N_KV, dtype=F32)
    w1r = w1.reshape(r, CMP_STRIDE, HEAD_DIM, w1.shape[-1])
    wc = jnp.einsum('jsdh,gk->sgdjkh', w1r, eye).reshape(CMP_STRIDE * KV_WIDTH, r * N_KV * w1.shape[-1])
    pe2 = jnp.broadcast_to(pe.reshape(r, CMP_STRIDE, 1, HEAD_DIM), (r, CMP_STRIDE, N_KV, HEAD_DIM))
    pe2 = jnp.pad(pe2.reshape(r, CMP_STRIDE * KV_WIDTH), ((0, 8 - r), (0, 0)))
    w2bd = jnp.einsum('hd,gk->ghkd', w2, eye).reshape(N_KV * w2.shape[0], KV_WIDTH)
    return wc.astype(BF16), pe2, w2bd.astype(BF16)


def kernel(x_prompt, x_sample, cache_k_cmp, cache_v_cmp, cache_k_sel, cache_v_sel, cache_k_win, cache_v_win,
           state_ssm_re, state_ssm_im, page_table, norm_mix, w_in, cmp_pe_k, cmp_w1_k, cmp_w2_k, cmp_pe_v, cmp_w1_v,
           cmp_w2_v, ssm_a_re, ssm_a_im, ssm_log_dt, ssm_b_re, ssm_b_im, ssm_c_re, ssm_c_im, ssm_d, w_att_proj,
           w_ssm_glu, w_out, norm_ffn, peer_w_q, peer_keys1, peer_keys2, peer_u, peer_v, norm_final):
    assert w_in.shape[0] == 1, "single layer"
    B, T, _ = x_prompt.shape
    DB = x_sample.shape[0]
    n_pages = page_table.shape[1]
    past = n_pages * PAGE_SIZE
    n_pool = cache_k_cmp.shape[1]

    w_hi, w_lo = _split(_proj_weight(w_in[0]))
    g_mix = norm_mix[0].reshape(1, D_MODEL)
    cw = _compress_weights(cmp_pe_k[0], cmp_w1_k[0], cmp_w2_k[0]) + _compress_weights(cmp_pe_v[0], cmp_w1_v[0],
                                                                                       cmp_w2_v[0])
    sp = _ssm_params(ssm_a_re[0], ssm_a_im[0], ssm_log_dt[0], ssm_b_re[0], ssm_b_im[0], ssm_c_re[0], ssm_c_im[0],
                     ssm_d[0])
    wa, ws, wo = w_att_proj[0].astype(BF16), w_ssm_glu[0].astype(BF16), w_out[0].astype(BF16)
    g_ffn = norm_ffn[0].reshape(1, D_MODEL)
    g_fin = norm_final.reshape(1, D_MODEL)
    wq_hi, wq_lo = _split(peer_w_q[0])
    u_tab = peer_u[0].astype(BF16)
    vt_tab = peer_v[0].T.astype(BF16)

    cos_p, sin_p = _rope_tables(jnp.arange(T, dtype=jnp.int32))
    q, kc, vc, ks, vs, kw, vw, gt, u, gm = _project(x_prompt, g_mix, cos_p, sin_p, [w_hi], 256, False)
    chunks = lambda a: a.reshape(B, T // CMP_STRIDE, CMP_STRIDE * KV_WIDTH)
    ck, cv = _compress_prompt(chunks(kc), chunks(vc), cw)
    o_att = _attn_prompt(q, gt, ck, cv, ks, vs, kw, vw)
    y_ssm, hp_re, hp_im = _ssm_prompt(u, sp)
    n_p = B * T
    flat = lambda a: a.reshape(n_p, a.shape[-1])
    y2_p, hf_p = _merge(flat(x_prompt), flat(o_att), flat(y_ssm), flat(gm), wa, ws, wo, g_ffn, 256)
    y_prompt = _peer(hf_p, y2_p, wq_hi, wq_lo, peer_keys1[0], peer_keys2[0], u_tab, vt_tab, g_fin, 512)

    cos_s, sin_s = _rope_tables(jnp.full((DB,), past, jnp.int32))
    xs = x_sample.reshape(1, DB, D_MODEL)
    qs, kcs, vcs, kss, vss, kws, vws, gts, us, gms = [a[0] for a in
                                                      _project(xs, g_mix, cos_s, sin_s, [w_hi, w_lo], DB, True)]
    row3 = lambda a: a.reshape(DB, 1, KV_WIDTH)
    native = lambda c: jnp.transpose(c[0], (0, 2, 3, 1)).reshape(c.shape[1], KV_WIDTH, c.shape[2])
    cks, cvs = _compress_sample(page_table, native(cache_k_cmp), native(cache_v_cmp), row3(kcs), row3(vcs), cw)
    q5 = qs.reshape(DB, N_KV, Q_PER_KV, 1, HEAD_DIM) * jnp.eye(N_KV, dtype=F32).reshape(1, N_KV, 1, N_KV, 1)
    qbd = q5.reshape(DB, N_HEADS, KV_WIDTH)
    n_blocks = -(-(past + 1) // SEL_BLOCK)
    ns_pad = -(-n_blocks // LANE) * LANE
    oc, sel = _attn_sample_cmp(qbd, cks, cvs, past, n_blocks, ns_pad)
    gt3 = gts[:, :3 * N_HEADS].reshape(DB, N_HEADS, 3)
    o_s = _attn_sample(page_table, qbd, sel, oc, gt3, row3(kss), row3(vss), native(cache_k_win), native(cache_v_win),
                       row3(kws), row3(vws), native(cache_k_sel), native(cache_v_sel), past)
    h0r = state_ssm_re[0].reshape(DB, SSM_GROUPS * SSM_STATE)
    h0i = state_ssm_im[0].reshape(DB, SSM_GROUPS * SSM_STATE)
    ys_ssm, hs_re, hs_im = _ssm_sample(us, h0r, h0i, sp)
    y2_s, hf_s = _merge(x_sample.reshape(DB, D_MODEL), o_s.reshape(DB, ATT_WIDTH), ys_ssm, gms, wa, ws, wo, g_ffn, DB)
    y_sample = _peer(hf_s, y2_s, wq_hi, wq_lo, peer_keys1[0], peer_keys2[0], u_tab, vt_tab, g_fin, DB)

    kv5 = lambda a, n: a.reshape(1, n, -1, N_KV, HEAD_DIM)
    wb = min(WINDOW, T)
    st = lambda a, n: a.reshape(1, n, SSM_GROUPS, SSM_STATE)
    wbuf = cache_k_win.shape[2]
    nw = min(WINDOW, wbuf + 1)
    win_s = lambda old, new: jnp.concatenate([old[0].reshape(DB, wbuf, KV_WIDTH), new.reshape(DB, 1, KV_WIDTH)],
                                             axis=1)[:, wbuf + 1 - nw:]
    return (y_prompt.reshape(B, T, D_MODEL), y_sample.reshape(DB, 1, D_MODEL),
            kv5(kc, B), kv5(vc, B), kv5(ks, B), kv5(vs, B), kv5(kw[:, T - wb:], B), kv5(vw[:, T - wb:], B),
            st(hp_re, B), st(hp_im, B),
            kv5(kcs, DB), kv5(vcs, DB), kv5(kss, DB), kv5(vss, DB),
            kv5(win_s(cache_k_win, kws), DB), kv5(win_s(cache_v_win, vws), DB),
            st(hs_re, DB), st(hs_im, DB))
```

```python
import functools
import math

import jax
import jax.numpy as jnp
from jax import lax
from jax.experimental import pallas as pl
from jax.experimental.pallas import tpu as pltpu

F32 = jnp.float32
BF16 = jnp.bfloat16

D_MODEL = 1024
HEAD_DIM = 64
N_HEADS = 8
N_KV = 2
Q_PER_KV = 4
ATT_WIDTH = 512
KV_WIDTH = 128
CMP_LEN = 32
CMP_STRIDE = 16
SEL_BLOCK = 64
SEL_TOPN = 16
WINDOW = 512
ROPE_THETA = 10000.0
PAGE_SIZE = 128
SSM_GROUP = 16
SSM_WIDTH = 512
SSM_GROUPS = 32
SSM_STATE = 64
PEER_HEADS = 8
PEER_NKEYS = 128
PEER_QDIM = 256
PEER_TOPK = 16
RMS_EPS = 1e-6
NEG = -1e30
BIG = 1e9
TINY = 1e-30
LOWEST = -3.0e38

LANE = 128
VMEM_LIMIT = 56 * 1024 * 1024

_NT = (((1,), (1,)), ((), ()))


def _params(sem, vmem=VMEM_LIMIT):
    return pltpu.CompilerParams(dimension_semantics=sem, vmem_limit_bytes=vmem)


def _split(x):
    hi = x.astype(BF16)
    lo = (x - hi.astype(F32)).astype(BF16)
    return hi, lo


def _dot(a, b):
    return jnp.dot(a, b, preferred_element_type=F32)


def _dot_nt(a, b):
    return lax.dot_general(a, b, _NT, preferred_element_type=F32)


def _dot3(a_hi, a_lo, b_hi, b_lo):
    return _dot(a_hi, b_hi) + (_dot(a_hi, b_lo) + _dot(a_lo, b_hi))


def _dot3_nt(a_hi, a_lo, b_hi, b_lo):
    return _dot_nt(a_hi, b_hi) + (_dot_nt(a_hi, b_lo) + _dot_nt(a_lo, b_hi))


def _rms(x, g):
    return x * lax.rsqrt(jnp.mean(x * x, axis=-1, keepdims=True) + RMS_EPS) * g


def _softmax_rows(s, mask):
    s = jnp.where(mask, s, NEG)
    e = jnp.where(mask, jnp.exp(s - jnp.max(s, axis=-1, keepdims=True)), 0.0)
    return e / jnp.maximum(jnp.sum(e, axis=-1, keepdims=True), TINY)


_SEG = {'q': (0, 512), 'kc': (512, 640), 'vc': (640, 768), 'ks': (768, 896), 'vs': (896, 1024),
        'kw': (1024, 1152), 'vw': (1152, 1280), 'gt': (1280, 1408), 'u': (1408, 1920), 'gm': (1920, 3968)}
_PROJ_COLS = 3968


def _proj_kernel(*refs, precise):
    x_ref, g_ref, cos_ref, sin_ref = refs[:4]
    n_w = 2 if precise else 1
    w_refs = refs[4:4 + n_w]
    q_ref, kc_ref, vc_ref, ks_ref, vs_ref, kw_ref, vw_ref, gt_ref, u_ref, gm_ref = refs[4 + n_w:]
    h = _rms(x_ref[0], g_ref[...])
    h_hi = h.astype(BF16)
    h_lo = (h - h_hi.astype(F32)).astype(BF16) if precise else None

    def mm(c0, c1):
        z = _dot(h_hi, w_refs[0][:, c0:c1])
        if precise:
            z = z + (_dot(h_hi, w_refs[1][:, c0:c1]) + _dot(h_lo, w_refs[0][:, c0:c1]))
        return z

    cos = cos_ref[...]
    sin = sin_ref[...]
    first = (lax.broadcasted_iota(jnp.int32, (1, LANE), 1) % HEAD_DIM) < (HEAD_DIM // 2)

    def rope(z):
        rot = jnp.where(first, pltpu.roll(z, LANE - HEAD_DIM // 2, 1), pltpu.roll(z, HEAD_DIM // 2, 1))
        return z * cos + rot * sin

    for i in range(4):
        q_ref[0, :, i * LANE:(i + 1) * LANE] = rope(mm(i * LANE, (i + 1) * LANE))
    kc_ref[0] = rope(mm(*_SEG['kc']))
    vc_ref[0] = mm(*_SEG['vc'])
    ks_ref[0] = rope(mm(*_SEG['ks']))
    vs_ref[0] = mm(*_SEG['vs'])
    kw_ref[0] = rope(mm(*_SEG['kw']))
    vw_ref[0] = mm(*_SEG['vw'])
    gt_ref[0] = mm(*_SEG['gt'])
    u_ref[0] = mm(*_SEG['u'])
    for i in range(4):
        c0 = _SEG['gm'][0] + i * 512
        gm_ref[0, :, i * 512:(i + 1) * 512] = mm(c0, c0 + 512)


def _project(x, gain, cos, sin, w_list, tm, precise):
    B, T, D = x.shape
    widths = [512, 128, 128, 128, 128, 128, 128, 128, 512, 2048]
    tok = lambda w: pl.BlockSpec((1, tm, w), lambda b, t: (b, t, 0))
    const = lambda a: pl.BlockSpec(a.shape, lambda b, t: (0,) * a.ndim)
    return pl.pallas_call(
        functools.partial(_proj_kernel, precise=precise),
        grid=(B, T // tm),
        in_specs=[tok(D), const(gain), pl.BlockSpec((tm, LANE), lambda b, t: (t, 0)),
                  pl.BlockSpec((tm, LANE), lambda b, t: (t, 0))] + [const(w) for w in w_list],
        out_specs=[tok(w) for w in widths],
        out_shape=[jax.ShapeDtypeStruct((B, T, w), F32) for w in widths],
        compiler_params=_params(("parallel", "arbitrary")),
        name="proj",
    )(x, gain, cos, sin, *w_list)


def _compress_rows(x, last_p1, wc_ref, pe_ref, w2_ref):
    C = x.shape[0]
    p = _dot(x.astype(BF16), wc_ref[...])
    pb = _dot(pe_ref[...].astype(BF16), wc_ref[...])
    bias = pb[0:1, :LANE] + pb[1:2, LANE:]
    p1 = pltpu.roll(p[:, LANE:], C - 1, 0)
    if last_p1 is not None:
        row = lax.broadcasted_iota(jnp.int32, (C, 1), 0)
        p1 = jnp.where(row == C - 1, last_p1, p1)
    hid = p[:, :LANE] + p1 + bias
    return _dot(jax.nn.gelu(hid).astype(BF16), w2_ref[...])


def _compress_prompt_kernel(xk_ref, xv_ref, wck_ref, pek_ref, w2k_ref, wcv_ref, pev_ref, w2v_ref, ck_ref, cv_ref):
    ck_ref[0] = _compress_rows(xk_ref[0], None, wck_ref, pek_ref, w2k_ref)
    cv_ref[0] = _compress_rows(xv_ref[0], None, wcv_ref, pev_ref, w2v_ref)


def _compress_prompt(xk, xv, cw):
    B, C, W = xk.shape
    seq = pl.BlockSpec((1, C, W), lambda b: (b, 0, 0))
    const = lambda a: pl.BlockSpec(a.shape, lambda b: (0,) * a.ndim)
    out = pl.BlockSpec((1, C, LANE), lambda b: (b, 0, 0))
    return pl.pallas_call(
        _compress_prompt_kernel,
        grid=(B,),
        in_specs=[seq, seq] + [const(a) for a in cw],
        out_specs=[out, out],
        out_shape=[jax.ShapeDtypeStruct((B, C, LANE), F32)] * 2,
        compiler_params=_params(("arbitrary",)),
        name="compress_prompt",
    )(xk, xv, *cw)


def _paged_fetch(pt_ref, pools, bufs, sems, n_pages, dst):
    b = pl.program_id(0)
    nb = pl.num_programs(0)
    slot = b % 2

    def copies(bb, sl):
        return [pltpu.make_async_copy(pool.at[pt_ref[bb, p]], dst(buf, sl, p), sem.at[sl])
                for pool, buf, sem in zip(pools, bufs, sems) for p in range(n_pages)]

    @pl.when(b == 0)
    def _():
        for cp in copies(b, slot):
            cp.start()

    @pl.when(b + 1 < nb)
    def _():
        for cp in copies(b + 1, 1 - slot):
            cp.start()

    for cp in copies(b, slot):
        cp.wait()
    return slot


def _compress_sample_kernel(pt_ref, kn_ref, vn_ref, perm_ref, wck_ref, pek_ref, w2k_ref, wcv_ref, pev_ref, w2v_ref,
                            kpool_ref, vpool_ref, ck_ref, cv_ref, kbuf, vbuf, ksem, vsem, taps_sc, *, n_pages):
    slot = _paged_fetch(pt_ref, (kpool_ref, vpool_ref), (kbuf, vbuf), (ksem, vsem), n_pages,
                        lambda buf, sl, p: buf.at[sl, p])
    cpp = PAGE_SIZE // CMP_STRIDE

    def one(buf, new_ref, wc_ref, pe_ref, w2_ref, out_ref):
        def flip(p, carry):
            rows = _dot_nt(perm_ref[...], buf[slot, p].astype(BF16))
            c0 = pl.multiple_of(p * cpp, cpp)
            for s in range(CMP_STRIDE):
                taps_sc[s, pl.ds(c0, cpp), :] = rows[s * cpp:(s + 1) * cpp]
            return carry

        lax.fori_loop(0, n_pages, flip, 0, unroll=32)
        x = jnp.concatenate([taps_sc[s] for s in range(CMP_STRIDE)], axis=1)
        new = jnp.broadcast_to(new_ref[0], (8, LANE)).astype(BF16)
        last_p1 = _dot(new, wc_ref[0:LANE, LANE:])[0:1]
        out_ref[0] = _compress_rows(x, last_p1, wc_ref, pe_ref, w2_ref)

    one(kbuf, kn_ref, wck_ref, pek_ref, w2k_ref, ck_ref)
    one(vbuf, vn_ref, wcv_ref, pev_ref, w2v_ref, cv_ref)


def _compress_sample(page_table, kpool, vpool, k_new, v_new, cw):
    DB, n_pages = page_table.shape
    C = n_pages * PAGE_SIZE // CMP_STRIDE
    new = pl.BlockSpec((1, 1, LANE), lambda b, pt: (b, 0, 0))
    const = lambda a: pl.BlockSpec(a.shape, lambda b, pt: (0,) * a.ndim)
    hbm = pl.BlockSpec(memory_space=pl.ANY)
    out = pl.BlockSpec((1, C, LANE), lambda b, pt: (b, 0, 0))
    page_buf = pltpu.VMEM((2, n_pages, KV_WIDTH, PAGE_SIZE), F32)
    cpp = PAGE_SIZE // CMP_STRIDE
    tok = jnp.arange(PAGE_SIZE)
    perm = ((tok[:, None] % cpp) * CMP_STRIDE + tok[:, None] // cpp == tok[None, :]).astype(BF16)
    return pl.pallas_call(
        functools.partial(_compress_sample_kernel, n_pages=n_pages),
        grid_spec=pltpu.PrefetchScalarGridSpec(
            num_scalar_prefetch=1, grid=(DB,),
            in_specs=[new, new, const(perm)] + [const(a) for a in cw] + [hbm, hbm],
            out_specs=[out, out],
            scratch_shapes=[page_buf, page_buf, pltpu.SemaphoreType.DMA((2,)), pltpu.SemaphoreType.DMA((2,)),
                            pltpu.VMEM((CMP_STRIDE, C, KV_WIDTH), F32)]),
        out_shape=[jax.ShapeDtypeStruct((DB, C, LANE), F32)] * 2,
        compiler_params=_params(("arbitrary",)),
        name="compress_sample",
    )(page_table, k_new, v_new, perm, *cw, kpool, vpool)


def _select_blocks(imp, qpos, n_real, axis=1):
    j_shape = (1, imp.shape[1]) if axis == 1 else (imp.shape[0], 1)
    j = lax.broadcasted_iota(jnp.int32, j_shape, axis)
    valid = j * SEL_BLOCK <= qpos
    cur = qpos // SEL_BLOCK
    force = (j == 0) | (j == cur) | (j == cur - 1)
    score = jnp.where(valid & force, BIG, jnp.where(valid, imp, -BIG))
    rank = jnp.zeros(imp.shape, F32)
    for jp in range(n_real):
        other = score[:, jp:jp + 1] if axis == 1 else score[jp:jp + 1, :]
        earlier = jnp.where(j > jp, 1.0, 0.0)
        rank = rank + jnp.where(other > score, 1.0, jnp.where(other == score, earlier, 0.0))
    return jnp.where(rank < SEL_TOPN, 1.0, 0.0)


def _attn_prompt_kernel(q_ref, gt_ref, ck_ref, cv_ref, ks_ref, vs_ref, kw_ref, vw_ref, msel_ref, exp_ref, o_ref,
                        m_sc, acc_sc, s_sc, *, tq, kc, wb):
    s0 = pl.program_id(1) * tq
    R = Q_PER_KV * tq
    ncb = ck_ref.shape[1]
    qpos_r = s0 + lax.broadcasted_iota(jnp.int32, (R, 1), 0) % tq
    qpos_t = s0 + lax.broadcasted_iota(jnp.int32, (tq, 1), 0)
    qpos_l = s0 + lax.broadcasted_iota(jnp.int32, (1, tq), 1)
    sig = jax.nn.sigmoid(gt_ref[0])
    stack = lambda f: jnp.concatenate([f(r) for r in range(Q_PER_KV)], axis=0)

    def rows_of(bias, r0, n):
        if n >= tq:
            return jnp.concatenate([bias] * (n // tq), axis=0)
        return bias[r0 % tq:r0 % tq + n]

    n_chunks = (s0 + tq + kc - 1) // kc
    group_lanes = lambda g: slice(g * HEAD_DIM, (g + 1) * HEAD_DIM)
    ones_lane = lambda g: slice((1 - g) * HEAD_DIM, (1 - g) * HEAD_DIM + 1)

    def with_ones(v, g):
        own = (lax.broadcasted_iota(jnp.int32, (1, KV_WIDTH), 1) // HEAD_DIM) == g
        return jnp.where(own, v, 1.0).astype(BF16)

    pre = []
    for g in range(N_KV):
        gl = group_lanes(g)
        qg = stack(lambda r: q_ref[0, :, (g * Q_PER_KV + r) * HEAD_DIM:(g * Q_PER_KV + r + 1) * HEAD_DIM])
        qg = qg * (HEAD_DIM ** -0.5)
        q_hi = qg.astype(BF16)

        s_c = _dot_nt(q_hi, ck_ref[0, :, gl].astype(BF16))
        cend = lax.broadcasted_iota(jnp.int32, (1, ncb), 1) * CMP_STRIDE + (CMP_LEN - 1)
        p_c = _softmax_rows(s_c, cend <= qpos_r)
        o_c = _dot(p_c.astype(BF16), cv_ref[0, :, gl].astype(BF16))
        p_sum = p_c[0:tq]
        for r in range(1, Q_PER_KV):
            p_sum = p_sum + p_c[r * tq:(r + 1) * tq]
        ps_hi, ps_lo = _split(p_sum)
        msel_t = msel_ref[...]
        imp_t = _dot_nt(msel_t, ps_hi) + _dot_nt(msel_t, ps_lo)
        sel = _select_blocks(imp_t, qpos_l, imp_t.shape[0], axis=0).T
        unpicked = ((sel - 1.0) * -NEG).astype(BF16)

        wl = WINDOW + tq
        w0 = pl.multiple_of(jnp.maximum(s0 - WINDOW, 0), tq)
        kpos = w0 + lax.broadcasted_iota(jnp.int32, (1, wl), 1)
        dlt = qpos_t - kpos
        bias_w = jnp.where((dlt >= 0) & (dlt < WINDOW), 0.0, NEG)
        k_w = kw_ref[0, pl.ds(w0, wl), gl].astype(BF16)
        v_w = with_ones(vw_ref[0, pl.ds(w0, wl), :], g)
        o_w = []
        for r0 in range(0, R, wb):
            s_w = _dot_nt(q_hi[r0:r0 + wb], k_w) + rows_of(bias_w, r0, wb)
            p_w = jnp.exp(s_w - jnp.max(s_w, axis=-1, keepdims=True))
            ov = _dot(p_w.astype(BF16), v_w)
            o_w.append(ov[:, gl] / jnp.maximum(ov[:, ones_lane(g)], TINY))
        pre.append((q_hi, unpicked, o_c, jnp.concatenate(o_w, axis=0)))

    heads = []
    for g in range(N_KV):
        gl = group_lanes(g)
        q_hi, unpicked, o_c, o_w = pre[g]
        m_sc[...] = jnp.full((R, LANE), NEG, F32)

        def sweep_max(c, carry):
            off = pl.multiple_of(c * kc, kc)
            k = ks_ref[0, pl.ds(off, kc), gl].astype(BF16)
            kpos = off + lax.broadcasted_iota(jnp.int32, (1, kc), 1)
            bias = _dot(unpicked, exp_ref[c]) + jnp.where(kpos <= qpos_t, 0.0, NEG)
            s = _dot_nt(q_hi, k) + rows_of(bias, 0, R)
            s_sc[c] = s
            mx = s[:, 0:LANE]
            for t in range(1, kc // LANE):
                mx = jnp.maximum(mx, s[:, t * LANE:(t + 1) * LANE])
            m_sc[...] = jnp.maximum(m_sc[...], mx)
            return carry

        lax.fori_loop(0, n_chunks, sweep_max, 0)
        m_sc[...] = jnp.broadcast_to(jnp.max(m_sc[...], axis=-1, keepdims=True), (R, LANE))
        acc_sc[...] = jnp.zeros((R, KV_WIDTH), F32)

        def sweep_pv(c, carry):
            off = pl.multiple_of(c * kc, kc)
            p = jnp.exp(s_sc[c] - jnp.concatenate([m_sc[...]] * (kc // LANE), axis=1))
            acc_sc[...] += _dot(p.astype(BF16), with_ones(vs_ref[0, pl.ds(off, kc), :], g))
            return carry

        lax.fori_loop(0, n_chunks, sweep_pv, 0)
        o_s = acc_sc[:, gl] / jnp.maximum(acc_sc[:, ones_lane(g)], TINY)

        gate = lambda i: stack(lambda r: sig[:, (g * Q_PER_KV + r) * 3 + i:(g * Q_PER_KV + r) * 3 + i + 1])
        og = gate(0) * o_c + gate(1) * o_s + gate(2) * o_w
        heads += [og[r * tq:(r + 1) * tq] for r in range(Q_PER_KV)]
    o_ref[0] = jnp.concatenate(heads, axis=1)


def _attn_prompt(q, gt, ck, cv, ks, vs, kw, vw, tq=128, kc=512, wb=256):
    B, T, _ = q.shape
    assert T % kc == 0 and T % tq == 0 and T >= WINDOW + tq and (tq % wb == 0 or wb % tq == 0)
    ncb = ck.shape[1]
    ns = T // SEL_BLOCK
    i = jnp.arange(ncb)[None, :]
    j = jnp.arange(ns)[:, None]
    msel = ((i * CMP_STRIDE <= j * SEL_BLOCK + SEL_BLOCK - 1)
            & (i * CMP_STRIDE + CMP_LEN - 1 >= j * SEL_BLOCK)).astype(BF16)
    key_blk = (jnp.arange(T) // SEL_BLOCK).reshape(T // kc, 1, kc)
    expand = (key_blk == jnp.arange(ns)[None, :, None]).astype(BF16)
    tok = lambda w: pl.BlockSpec((1, tq, w), lambda b, t: (b, t, 0))
    seq = lambda a: pl.BlockSpec((1,) + a.shape[1:], lambda b, t: (b, 0, 0))
    const = lambda a: pl.BlockSpec(a.shape, lambda b, t: (0,) * a.ndim)
    R = Q_PER_KV * tq
    return pl.pallas_call(
        functools.partial(_attn_prompt_kernel, tq=tq, kc=kc, wb=wb),
        grid=(B, T // tq),
        in_specs=[tok(ATT_WIDTH), tok(LANE), seq(ck), seq(cv), seq(ks), seq(vs), seq(kw), seq(vw),
                  const(msel), const(expand)],
        out_specs=tok(ATT_WIDTH),
        out_shape=jax.ShapeDtypeStruct((B, T, ATT_WIDTH), F32),
        scratch_shapes=[pltpu.VMEM((R, LANE), F32), pltpu.VMEM((R, KV_WIDTH), F32), pltpu.VMEM((T // kc, R, kc), F32)],
        compiler_params=_params(("parallel", "arbitrary")),
        name="attn_prompt",
    )(q, gt, ck, cv, ks, vs, kw, vw, msel, expand)


def _group_lanes():
    row = lax.broadcasted_iota(jnp.int32, (N_HEADS, LANE), 0) // Q_PER_KV
    lane = lax.broadcasted_iota(jnp.int32, (N_HEADS, LANE), 1) // HEAD_DIM
    return row == lane


def _attn_sample_cmp_kernel(q_ref, ck_ref, cv_ref, msel_ref, oc_ref, sel_ref, *, qpos, n_blocks):
    q_hi, q_lo = _split(q_ref[0] * (HEAD_DIM ** -0.5))
    k_hi, k_lo = _split(ck_ref[0])
    s_c = _dot3_nt(q_hi, q_lo, k_hi, k_lo)
    ncb = s_c.shape[1]
    cend = lax.broadcasted_iota(jnp.int32, (1, ncb), 1) * CMP_STRIDE + (CMP_LEN - 1)
    p_c = _softmax_rows(s_c, cend <= qpos)
    o_c = _dot(p_c.astype(BF16), cv_ref[0].astype(BF16))
    oc_ref[0] = jnp.where(_group_lanes(), o_c, 0.0)
    top = jnp.sum(p_c[0:Q_PER_KV], axis=0, keepdims=True)
    bot = jnp.sum(p_c[Q_PER_KV:], axis=0, keepdims=True)
    row = lax.broadcasted_iota(jnp.int32, (N_HEADS, 1), 0)
    p_sum = jnp.where(row < Q_PER_KV, top, bot)
    ps_hi, ps_lo = _split(p_sum)
    msel = msel_ref[...]
    imp = _dot(ps_hi, msel) + _dot(ps_lo, msel)
    sel_ref[0] = _select_blocks(imp, jnp.full((N_HEADS, 1), qpos, jnp.int32), n_blocks)


def _attn_sample_cmp(qbd, ck, cv, qpos, n_blocks, ns_pad):
    DB, ncb, _ = ck.shape
    i = jnp.arange(ncb)[:, None]
    j = jnp.arange(ns_pad)[None, :]
    msel = ((i * CMP_STRIDE <= j * SEL_BLOCK + SEL_BLOCK - 1) & (i * CMP_STRIDE + CMP_LEN - 1 >= j * SEL_BLOCK)
            & (j < n_blocks)).astype(BF16)
    row = lambda a: pl.BlockSpec((1,) + a.shape[1:], lambda b: (b, 0, 0))
    return pl.pallas_call(
        functools.partial(_attn_sample_cmp_kernel, qpos=qpos, n_blocks=n_blocks),
        grid=(DB,),
        in_specs=[row(qbd), row(ck), row(cv), pl.BlockSpec(msel.shape, lambda b: (0, 0))],
        out_specs=[pl.BlockSpec((1, N_HEADS, LANE), lambda b: (b, 0, 0)),
                   pl.BlockSpec((1, N_HEADS, ns_pad), lambda b: (b, 0, 0))],
        out_shape=[jax.ShapeDtypeStruct((DB, N_HEADS, LANE), F32), jax.ShapeDtypeStruct((DB, N_HEADS, ns_pad), F32)],
        compiler_params=_params(("arbitrary",)),
        name="attn_sample_cmp",
    )(qbd, ck, cv, msel)


def _attn_sample_kernel(pt_ref, q_ref, sel_ref, oc_ref, gt_ref, ksn_ref, vsn_ref, kwb_ref, vwb_ref, kwn_ref, vwn_ref,
                        exp_ref, kpool_ref, vpool_ref, o_ref, kbuf, vbuf, ksem, vsem, *, n_pages, qpos):
    slot = _paged_fetch(pt_ref, (kpool_ref, vpool_ref), (kbuf, vbuf), (ksem, vsem), n_pages,
                        lambda buf, sl, p: buf.at[sl, :, pl.ds(p * PAGE_SIZE, PAGE_SIZE)])
    past = n_pages * PAGE_SIZE
    q = q_ref[0] * (HEAD_DIM ** -0.5)
    q_b = q.astype(BF16)
    sel = sel_ref[0]

    def attend(kt_old, vt_old, mask_old, k_new, v_new, mask_new):
        s = jnp.where(mask_old, _dot(q_b, kt_old.astype(BF16)), NEG)
        s_n = jnp.where(mask_new, jnp.sum(q * k_new, axis=-1, keepdims=True), NEG)
        m = jnp.maximum(jnp.max(s, axis=-1, keepdims=True), s_n)
        e = jnp.where(mask_old, jnp.exp(s - m), 0.0)
        e_n = jnp.where(mask_new, jnp.exp(s_n - m), 0.0)
        den = jnp.maximum(jnp.sum(e, axis=-1, keepdims=True) + e_n, TINY)
        return (_dot_nt(e.astype(BF16), vt_old.astype(BF16)) + e_n * v_new) / den

    picked = _dot(sel.astype(BF16), exp_ref[...])
    kpos = lax.broadcasted_iota(jnp.int32, (1, past), 1)
    nb_new = past // SEL_BLOCK
    o_s = attend(kbuf[slot], vbuf[slot], (picked > 0.5) & (kpos <= qpos), ksn_ref[0], vsn_ref[0],
                 (sel[:, nb_new:nb_new + 1] > 0.5) & (past <= qpos))
    wbuf = kwb_ref.shape[2]
    dlt = qpos - (past - wbuf + lax.broadcasted_iota(jnp.int32, (1, wbuf), 1))
    o_w = attend(kwb_ref[0], vwb_ref[0], (dlt >= 0) & (dlt < WINDOW), kwn_ref[0], vwn_ref[0],
                 jnp.full((N_HEADS, 1), (qpos - past >= 0) & (qpos - past < WINDOW)))
    sig = jax.nn.sigmoid(gt_ref[0])
    o = sig[:, 0:1] * oc_ref[0] + sig[:, 1:2] * o_s + sig[:, 2:3] * o_w
    o = jnp.where(_group_lanes(), o, 0.0)
    o_ref[0] = o[:, :HEAD_DIM] + o[:, HEAD_DIM:]


def _attn_sample(page_table, qbd, sel, oc, gt3, ksn, vsn, kwb, vwb, kwn, vwn, kpool, vpool, qpos):
    DB, n_pages = page_table.shape
    past = n_pages * PAGE_SIZE
    nsp = sel.shape[-1]
    expand = ((jnp.arange(past) // SEL_BLOCK)[None, :] == jnp.arange(nsp)[:, None]).astype(BF16)
    row = lambda a: pl.BlockSpec((1,) + a.shape[1:], lambda b, pt: (b, 0, 0))
    hbm = pl.BlockSpec(memory_space=pl.ANY)
    ins = [qbd, sel, oc, gt3, ksn, vsn, kwb, vwb, kwn, vwn]
    return pl.pallas_call(
        functools.partial(_attn_sample_kernel, n_pages=n_pages, qpos=qpos),
        grid_spec=pltpu.PrefetchScalarGridSpec(
            num_scalar_prefetch=1, grid=(DB,),
            in_specs=[row(a) for a in ins] + [pl.BlockSpec(expand.shape, lambda b, pt: (0, 0)), hbm, hbm],
            out_specs=pl.BlockSpec((1, N_HEADS, HEAD_DIM), lambda b, pt: (b, 0, 0)),
            scratch_shapes=[pltpu.VMEM((2, KV_WIDTH, past), F32), pltpu.VMEM((2, KV_WIDTH, past), F32),
                            pltpu.SemaphoreType.DMA((2,)), pltpu.SemaphoreType.DMA((2,))]),
        out_shape=jax.ShapeDtypeStruct((DB, N_HEADS, HEAD_DIM), F32),
        compiler_params=_params(("arbitrary",)),
        name="attn_sample",
    )(page_table, *ins, expand, kpool, vpool)


_SSM_HALF = SSM_WIDTH // 2
_SSM_ROW = SSM_GROUPS // 2 * SSM_STATE


def _ssm_params(a_re, a_im, log_dt, b_re, b_im, c_re, c_im, d):
    lam = lax.complex(a_re, a_im)
    step = jnp.exp(log_dt)[:, None]
    a_bar = jnp.exp(lam * step)
    b_bar = ((a_bar - 1.0) / lam)[..., None] * lax.complex(b_re, b_im)
    eye = jnp.eye(SSM_GROUPS // 2, dtype=F32)

    def b_mat(x):
        x = x.reshape(2, SSM_GROUPS // 2, SSM_STATE, SSM_GROUP)
        return jnp.einsum('hgpc,gk->hgckp', x, eye).reshape(2, _SSM_HALF, _SSM_ROW)

    def c_mat(x):
        x = x.reshape(2, SSM_GROUPS // 2, SSM_GROUP, SSM_STATE)
        return jnp.einsum('hgcp,gk->hgpkc', x, eye).reshape(2, _SSM_ROW, _SSM_HALF)

    bm = jnp.concatenate([b_mat(jnp.real(b_bar)), b_mat(jnp.imag(b_bar))], axis=2)
    cm = jnp.concatenate([c_mat(c_re), -c_mat(c_im)], axis=1)
    ar = jnp.real(a_bar).reshape(2, _SSM_ROW)
    ai = jnp.imag(a_bar).reshape(2, _SSM_ROW)
    bm_hi, bm_lo = _split(bm)
    cm_hi, cm_lo = _split(cm)
    return ar, ai, bm_hi, bm_lo, cm_hi, cm_lo, d.reshape(1, SSM_WIDTH)


def _ssm_prompt_kernel(u_ref, ar_ref, ai_ref, bh_ref, ch_ref, d_ref, y_ref, hr_ref, hi_ref,
                       sr_sc, si_sc, st_re, st_im, *, tc, nb):
    t = pl.program_id(0)
    rows = 2 * nb
    nlb = _SSM_ROW // LANE

    @pl.when(t == 0)
    def _():
        st_re[...] = jnp.zeros_like(st_re)
        st_im[...] = jnp.zeros_like(st_im)

    for b in range(nb):
        for hf in range(2):
            u_hi, u_lo = _split(u_ref[b, :, hf * _SSM_HALF:(hf + 1) * _SSM_HALF])
            bu = _dot(u_hi, bh_ref[hf]) + _dot(u_lo, bh_ref[hf])
            for k in range(nlb):
                sr_sc[k, pl.ds(b * 2 + hf, tc, stride=rows), :] = bu[:, k * LANE:(k + 1) * LANE]
                si_sc[k, pl.ds(b * 2 + hf, tc, stride=rows), :] = bu[:, _SSM_ROW + k * LANE:_SSM_ROW + (k + 1) * LANE]

    ar = ar_ref[...]
    ai = ai_ref[...]

    def step(i, carry):
        h_re, h_im = carry
        r0 = pl.multiple_of(i * rows, rows)
        n_re = ar * h_re - ai * h_im + sr_sc[:, pl.ds(r0, rows), :]
        n_im = ar * h_im + ai * h_re + si_sc[:, pl.ds(r0, rows), :]
        sr_sc[:, pl.ds(r0, rows), :] = n_re
        si_sc[:, pl.ds(r0, rows), :] = n_im
        return n_re, n_im

    h_re, h_im = lax.fori_loop(0, tc, step, (st_re[...], st_im[...]), unroll=4)
    st_re[...] = h_re
    st_im[...] = h_im
    hr_ref[...] = h_re
    hi_ref[...] = h_im

    for b in range(nb):
        for hf in range(2):
            gather = lambda sc: jnp.concatenate(
                [sc[k, pl.ds(b * 2 + hf, tc, stride=rows), :] for k in range(nlb)], axis=1)
            hs = jnp.concatenate([gather(sr_sc).astype(BF16), gather(si_sc).astype(BF16)], axis=1)
            cols = slice(hf * _SSM_HALF, (hf + 1) * _SSM_HALF)
            y_ref[b, :, cols] = _dot(hs, ch_ref[hf]) + d_ref[:, cols] * u_ref[b, :, cols]


def _ssm_prompt(u, sp, tc=128):
    B, T, W = u.shape
    ar, ai, bh, bl, ch, cl, d = sp
    rows = 2 * B
    nlb = _SSM_ROW // LANE
    tiles = lambda a: jnp.tile(a, (B, 1)).reshape(rows, nlb, LANE).transpose(1, 0, 2)
    const = lambda a: pl.BlockSpec(a.shape, lambda t: (0,) * a.ndim)
    blk = pl.BlockSpec((B, tc, W), lambda t: (0, t, 0))
    st = pl.BlockSpec((nlb, rows, LANE), lambda t: (0, 0, 0))
    ins = [tiles(ar), tiles(ai), bh, ch, d]
    st_shape = jax.ShapeDtypeStruct((nlb, rows, LANE), F32)
    y, h_re, h_im = pl.pallas_call(
        functools.partial(_ssm_prompt_kernel, tc=tc, nb=B),
        grid=(T // tc,),
        in_specs=[blk] + [const(a) for a in ins],
        out_specs=[blk, st, st],
        out_shape=[jax.ShapeDtypeStruct((B, T, W), F32), st_shape, st_shape],
        scratch_shapes=[pltpu.VMEM((nlb, tc * rows, LANE), F32), pltpu.VMEM((nlb, tc * rows, LANE), F32),
                        pltpu.VMEM((nlb, rows, LANE), F32), pltpu.VMEM((nlb, rows, LANE), F32)],
        compiler_params=_params(("arbitrary",)),
        name="ssm_prompt",
    )(u, *ins)
    rows_major = lambda a: a.transpose(1, 0, 2).reshape(rows, _SSM_ROW)
    return y, rows_major(h_re), rows_major(h_im)


def _ssm_sample_kernel(u_ref, h0r_ref, h0i_ref, ar_ref, ai_ref, bh_ref, bl_ref, ch_ref, cl_ref, d_ref,
                       y_ref, hr_ref, hi_ref):
    for hf in range(2):
        cols = slice(hf * _SSM_HALF, (hf + 1) * _SSM_HALF)
        lanes = slice(hf * _SSM_ROW, (hf + 1) * _SSM_ROW)
        u = u_ref[:, cols]
        u_hi, u_lo = _split(u)
        bu = _dot3(u_hi, u_lo, bh_ref[hf], bl_ref[hf])
        ar = ar_ref[hf:hf + 1, :]
        ai = ai_ref[hf:hf + 1, :]
        h_re = ar * h0r_ref[:, lanes] - ai * h0i_ref[:, lanes] + bu[:, :_SSM_ROW]
        h_im = ar * h0i_ref[:, lanes] + ai * h0r_ref[:, lanes] + bu[:, _SSM_ROW:]
        hr_ref[:, lanes] = h_re
        hi_ref[:, lanes] = h_im
        r_hi, r_lo = _split(h_re)
        i_hi, i_lo = _split(h_im)
        y = (_dot3(r_hi, r_lo, ch_ref[hf, :_SSM_ROW], cl_ref[hf, :_SSM_ROW])
             + _dot3(i_hi, i_lo, ch_ref[hf, _SSM_ROW:], cl_ref[hf, _SSM_ROW:]))
        y_ref[:, cols] = y + d_ref[:, cols] * u


def _ssm_sample(u, h0r, h0i, sp):
    n = u.shape[0]
    ins = [u, h0r, h0i, *sp]
    full = lambda a: pl.BlockSpec(a.shape, lambda i: (0,) * a.ndim)
    outs = [jax.ShapeDtypeStruct((n, SSM_WIDTH), F32), jax.ShapeDtypeStruct(h0r.shape, F32),
            jax.ShapeDtypeStruct(h0r.shape, F32)]
    return pl.pallas_call(
        _ssm_sample_kernel, grid=(1,),
        in_specs=[full(a) for a in ins], out_specs=[full(a) for a in outs], out_shape=outs,
        compiler_params=_params(("arbitrary",)),
        name="ssm_sample",
    )(*ins)


def _merge_kernel(x_ref, o_ref, y_ref, gm_ref, wa_ref, ws_ref, wo_ref, gf_ref, y2_ref, hf_ref):
    a = _dot(o_ref[...].astype(BF16), wa_ref[...])
    gl = _dot(jax.nn.gelu(y_ref[...]).astype(BF16), ws_ref[...])
    s = gl[:, :D_MODEL] * jax.nn.sigmoid(gl[:, D_MODEL:])
    gm = gm_ref[...]
    m = jax.nn.sigmoid(gm[:, :D_MODEL]) * a + jax.nn.sigmoid(gm[:, D_MODEL:]) * s
    y2 = x_ref[...] + _dot(m.astype(BF16), wo_ref[...])
    y2_ref[...] = y2
    hf_ref[...] = _rms(y2, gf_ref[...])


def _merge(x, o, y, gm, wa, ws, wo, gf, tm):
    n = x.shape[0]
    tok = lambda w: pl.BlockSpec((tm, w), lambda i: (i, 0))
    const = lambda a: pl.BlockSpec(a.shape, lambda i: (0,) * a.ndim)
    return pl.pallas_call(
        _merge_kernel, grid=(n // tm,),
        in_specs=[tok(D_MODEL), tok(ATT_WIDTH), tok(SSM_WIDTH), tok(2 * D_MODEL), const(wa), const(ws), const(wo),
                  const(gf)],
        out_specs=[tok(D_MODEL), tok(D_MODEL)],
        out_shape=[jax.ShapeDtypeStruct((n, D_MODEL), F32)] * 2,
        compiler_params=_params(("parallel",)),
        name="merge",
    )(x, o, y, gm, wa, ws, wo, gf)


def _cand_pairs():
    return [(a, b) for a in range(PEER_TOPK) for b in range(PEER_TOPK) if (a + 1) * (b + 1) <= PEER_TOPK]


def _top_values(s, k):
    vals = []
    for _ in range(k):
        mx = jnp.max(s, axis=0, keepdims=True)
        vals.append(mx)
        s = jnp.where(s == mx, LOWEST, s)
    return vals


def _peer_route_kernel(hf_ref, wqh_ref, wql_ref, k1h_ref, k1l_ref, k2h_ref, k2l_ref,
                       c1_ref, e1_ref, s2_ref, e2_ref, hh_sc, hl_sc):
    half = PEER_QDIM // 2

    @pl.when(pl.program_id(1) == 0)
    def _():
        hh_sc[...], hl_sc[...] = _split(hf_ref[...])

    q = _dot3(hh_sc[...], hl_sc[...], wqh_ref[...], wql_ref[...])
    q1h, q1l = _split(q[:, :half])
    q2h, q2l = _split(q[:, half:])
    s1 = _dot3_nt(k1h_ref[...], k1l_ref[...], q1h, q1l)
    s2 = _dot3_nt(k2h_ref[...], k2l_ref[...], q2h, q2l)
    v1 = _top_values(s1, PEER_TOPK)
    v2 = _top_values(s2, PEER_TOPK)
    cand = jnp.concatenate([v1[a] + v2[b] for a, b in _cand_pairs()], axis=0)
    thr = _top_values(cand, PEER_TOPK)[-1]
    top = v1[0] + v2[0]
    z = jnp.sum(jnp.where(cand >= thr, jnp.exp(cand - top), 0.0), axis=0, keepdims=True)
    c1 = jnp.full(s1.shape, -LOWEST, F32)
    for v in v2:
        c1 = jnp.where(s1 + v >= thr, v, c1)
    c1_ref[0] = c1
    s2_ref[0] = s2
    e1_ref[0] = jnp.exp(s1 - v1[0])
    e2_ref[0] = jnp.exp(s2 - v2[0]) / z


def _peer_route(hf, wq_hi, wq_lo, k1, k2, tn):
    n = hf.shape[0]
    k1h, k1l = _split(k1)
    k2h, k2l = _split(k2)
    const = lambda a: pl.BlockSpec(a.shape, lambda i, h: (0, 0))
    wq = pl.BlockSpec((D_MODEL, PEER_QDIM), lambda i, h: (0, h))
    keyed = pl.BlockSpec((1, PEER_NKEYS, tn), lambda i, h: (h, 0, i))
    shp = jax.ShapeDtypeStruct((PEER_HEADS, PEER_NKEYS, n), F32)
    return pl.pallas_call(
        _peer_route_kernel, grid=(n // tn, PEER_HEADS),
        in_specs=[pl.BlockSpec((tn, D_MODEL), lambda i, h: (i, 0)), wq, wq, const(k1h), const(k1l), const(k2h),
                  const(k2l)],
        out_specs=[keyed, keyed, keyed, keyed],
        out_shape=[shp, shp, shp, shp],
        scratch_shapes=[pltpu.VMEM((tn, D_MODEL), BF16), pltpu.VMEM((tn, D_MODEL), BF16)],
        compiler_params=_params(("parallel", "arbitrary")),
        name="peer_route",
    )(hf, wq_hi, wq_lo, k1h, k1l, k2h, k2l)


_PEER_JROWS = 32
_PEER_SLABS = 4


def _gelu_tanh(x):
    k = math.sqrt(2.0 / math.pi)
    hx = 0.5 * x
    return hx + hx * jnp.tanh(x * (k + (k * 0.044715) * (x * x)))


def _peer_main_kernel(hf_ref, y2_ref, u_ref, vt_ref, c1_ref, e1_ref, s2_ref, e2_ref, gn_ref, out_ref,
                      acc_sc, w_sc, act_sc, hfb_sc, *, tn, n_slab):
    e = pl.program_id(1)

    @pl.when(e == 0)
    def _():
        acc_sc[...] = jnp.zeros_like(acc_sc)
        hfb_sc[...] = hf_ref[...].astype(BF16)

    act_sc[...] = _gelu_tanh(_dot_nt(u_ref[...], hfb_sc[...]))

    for lc in range(tn // LANE):
        ln = slice(lc * LANE, (lc + 1) * LANE)

        def rows(jq, carry, ln=ln):
            j0 = pl.multiple_of(jq * _PEER_JROWS, _PEER_JROWS)
            for i0 in range(0, n_slab, _PEER_SLABS):
                slabs = range(i0, i0 + _PEER_SLABS)
                w = {ii: jnp.zeros((_PEER_JROWS, LANE), F32) for ii in slabs}
                for h in range(PEER_HEADS):
                    s2 = s2_ref[h, pl.ds(j0, _PEER_JROWS), ln]
                    e2 = e2_ref[h, pl.ds(j0, _PEER_JROWS), ln]
                    for ii in slabs:
                        w[ii] = w[ii] + e1_ref[h, ii:ii + 1, ln] * jnp.where(s2 >= c1_ref[h, ii:ii + 1, ln], e2, 0.0)
                for ii in slabs:
                    rs = pl.ds(ii * PEER_NKEYS + j0, _PEER_JROWS)
                    w_sc[rs, ln] = (w[ii] * act_sc[rs, ln]).astype(BF16)
            return carry

        lax.fori_loop(0, PEER_NKEYS // _PEER_JROWS, rows, 0)

    acc_sc[...] += _dot(vt_ref[...], w_sc[...])

    @pl.when(e == pl.num_programs(1) - 1)
    def _():
        out_ref[...] = _rms(y2_ref[...] + acc_sc[...].T, gn_ref[...])


def _peer_main(hf, y2, u_tab, vt_tab, c1, e1, s2, e2, gn, tn, n_slab=16):
    n = hf.shape[0]
    ec = n_slab * PEER_NKEYS
    n_exp = u_tab.shape[0]
    tok = pl.BlockSpec((tn, D_MODEL), lambda i, e: (i, 0))
    slab = pl.BlockSpec((PEER_HEADS, n_slab, tn), lambda i, e: (0, e, i))
    keyed = pl.BlockSpec((PEER_HEADS, PEER_NKEYS, tn), lambda i, e: (0, 0, i))
    return pl.pallas_call(
        functools.partial(_peer_main_kernel, tn=tn, n_slab=n_slab),
        grid=(n // tn, n_exp // ec),
        in_specs=[tok, tok, pl.BlockSpec((ec, D_MODEL), lambda i, e: (e, 0)),
                  pl.BlockSpec((D_MODEL, ec), lambda i, e: (0, e)), slab, slab, keyed, keyed,
                  pl.BlockSpec(gn.shape, lambda i, e: (0, 0))],
        out_specs=tok,
        out_shape=jax.ShapeDtypeStruct((n, D_MODEL), F32),
        scratch_shapes=[pltpu.VMEM((D_MODEL, tn), F32), pltpu.VMEM((ec, tn), BF16), pltpu.VMEM((ec, tn), F32),
                        pltpu.VMEM((tn, D_MODEL), BF16)],
        compiler_params=_params(("parallel", "arbitrary")),
        name="peer_main",
    )(hf, y2, u_tab, vt_tab, c1, e1, s2, e2, gn)


def _peer(hf, y2, wq_hi, wq_lo, k1, k2, u_tab, vt_tab, gn, tn):
    c1, e1, s2, e2 = _peer_route(hf, wq_hi, wq_lo, k1, k2, tn)
    return _peer_main(hf, y2, u_tab, vt_tab, c1, e1, s2, e2, gn, tn)


def _rope_tables(pos):
    half = HEAD_DIM // 2
    inv = ROPE_THETA ** (-jnp.arange(half, dtype=F32) / half)
    ang = pos.astype(F32)[:, None] * inv[None, :]
    cos, sin = jnp.cos(ang), jnp.sin(ang)
    return jnp.tile(jnp.concatenate([cos, cos], -1), (1, 2)), jnp.tile(jnp.concatenate([-sin, sin], -1), (1, 2))


def _proj_weight(w):
    gates = jnp.pad(w[:, 1280:1304], ((0, 0), (0, LANE - 3 * N_HEADS)))
    return jnp.concatenate([w[:, :1280], gates, w[:, 1304:1816], w[:, 1816:]], axis=1)


def _compress_weights(pe, w1, w2):
    r = CMP_LEN // CMP_STRIDE
    eye = jnp.eye(N_KV, dtype=F32)
    w1r = w1.reshape(r, CMP_STRIDE, HEAD_DIM, w1.shape[-1])
    wc = jnp.einsum('jsdh,gk->sgdjkh', w1r, eye).reshape(CMP_STRIDE * KV_WIDTH, r * N_KV * w1.shape[-1])
    pe2 = jnp.broadcast_to(pe.reshape(r, CMP_STRIDE, 1, HEAD_DIM), (r, CMP_STRIDE, N_KV, HEAD_DIM))
    pe2 = jnp.pad(pe2.reshape(r, CMP_STRIDE * KV_WIDTH), ((0, 8 - r), (0, 0)))
    w2bd = jnp.einsum('hd,gk->ghkd', w2, eye).reshape(N_KV * w2.shape[0], KV_WIDTH)
    return wc.astype(BF16), pe2, w2bd.astype(BF16)


def kernel(x_prompt, x_sample, cache_k_cmp, cache_v_cmp, cache_k_sel, cache_v_sel, cache_k_win, cache_v_win,
           state_ssm_re, state_ssm_im, page_table, norm_mix, w_in, cmp_pe_k, cmp_w1_k, cmp_w2_k, cmp_pe_v, cmp_w1_v,
           cmp_w2_v, ssm_a_re, ssm_a_im, ssm_log_dt, ssm_b_re, ssm_b_im, ssm_c_re, ssm_c_im, ssm_d, w_att_proj,
           w_ssm_glu, w_out, norm_ffn, peer_w_q, peer_keys1, peer_keys2, peer_u, peer_v, norm_final):
    assert w_in.shape[0] == 1, "single layer"
    B, T, _ = x_prompt.shape
    DB = x_sample.shape[0]
    n_pages = page_table.shape[1]
    past = n_pages * PAGE_SIZE
    n_pool = cache_k_cmp.shape[1]

    w_hi, w_lo = _split(_proj_weight(w_in[0]))
    g_mix = norm_mix[0].reshape(1, D_MODEL)
    cw = _compress_weights(cmp_pe_k[0], cmp_w1_k[0], cmp_w2_k[0]) + _compress_weights(cmp_pe_v[0], cmp_w1_v[0],
                                                                                       cmp_w2_v[0])
    sp = _ssm_params(ssm_a_re[0], ssm_a_im[0], ssm_log_dt[0], ssm_b_re[0], ssm_b_im[0], ssm_c_re[0], ssm_c_im[0],
                     ssm_d[0])
    wa, ws, wo = w_att_proj[0].astype(BF16), w_ssm_glu[0].astype(BF16), w_out[0].astype(BF16)
    g_ffn = norm_ffn[0].reshape(1, D_MODEL)
    g_fin = norm_final.reshape(1, D_MODEL)
    wq_hi, wq_lo = _split(peer_w_q[0])
    u_tab = peer_u[0].astype(BF16)
    vt_tab = peer_v[0].T.astype(BF16)

    cos_p, sin_p = _rope_tables(jnp.arange(T, dtype=jnp.int32))
    q, kc, vc, ks, vs, kw, vw, gt, u, gm = _project(x_prompt, g_mix, cos_p, sin_p, [w_hi], 512, False)
    chunks = lambda a: a.reshape(B, T // CMP_STRIDE, CMP_STRIDE * KV_WIDTH)
    ck, cv = _compress_prompt(chunks(kc), chunks(vc), cw)
    o_att = _attn_prompt(q, gt, ck, cv, ks, vs, kw, vw)
    y_ssm, hp_re, hp_im = _ssm_prompt(u, sp)
    n_p = B * T
    flat = lambda a: a.reshape(n_p, a.shape[-1])
    y2_p, hf_p = _merge(flat(x_prompt), flat(o_att), flat(y_ssm), flat(gm), wa, ws, wo, g_ffn, 512)
    y_prompt = _peer(hf_p, y2_p, wq_hi, wq_lo, peer_keys1[0], peer_keys2[0], u_tab, vt_tab, g_fin, 512)

    cos_s, sin_s = _rope_tables(jnp.full((DB,), past, jnp.int32))
    xs = x_sample.reshape(1, DB, D_MODEL)
    qs, kcs, vcs, kss, vss, kws, vws, gts, us, gms = [a[0] for a in
                                                      _project(xs, g_mix, cos_s, sin_s, [w_hi, w_lo], DB, True)]
    row3 = lambda a: a.reshape(DB, 1, KV_WIDTH)
    native = lambda c: jnp.transpose(c[0], (0, 2, 3, 1)).reshape(c.shape[1], KV_WIDTH, c.shape[2])
    cks, cvs = _compress_sample(page_table, native(cache_k_cmp), native(cache_v_cmp), row3(kcs), row3(vcs), cw)
    q5 = qs.reshape(DB, N_KV, Q_PER_KV, 1, HEAD_DIM) * jnp.eye(N_KV, dtype=F32).reshape(1, N_KV, 1, N_KV, 1)
    qbd = q5.reshape(DB, N_HEADS, KV_WIDTH)
    n_blocks = -(-(past + 1) // SEL_BLOCK)
    ns_pad = -(-n_blocks // LANE) * LANE
    oc, sel = _attn_sample_cmp(qbd, cks, cvs, past, n_blocks, ns_pad)
    gt3 = gts[:, :3 * N_HEADS].reshape(DB, N_HEADS, 3)
    o_s = _attn_sample(page_table, qbd, sel, oc, gt3, row3(kss), row3(vss), native(cache_k_win), native(cache_v_win),
                       row3(kws), row3(vws), native(cache_k_sel), native(cache_v_sel), past)
    h0r = state_ssm_re[0].reshape(DB, SSM_GROUPS * SSM_STATE)
    h0i = state_ssm_im[0].reshape(DB, SSM_GROUPS * SSM_STATE)
    ys_ssm, hs_re, hs_im = _ssm_sample(us, h0r, h0i, sp)
    y2_s, hf_s = _merge(x_sample.reshape(DB, D_MODEL), o_s.reshape(DB, ATT_WIDTH), ys_ssm, gms, wa, ws, wo, g_ffn, DB)
    y_sample = _peer(hf_s, y2_s, wq_hi, wq_lo, peer_keys1[0], peer_keys2[0], u_tab, vt_tab, g_fin, DB)

    kv5 = lambda a, n: a.reshape(1, n, -1, N_KV, HEAD_DIM)
    wb = min(WINDOW, T)
    st = lambda a, n: a.reshape(1, n, SSM_GROUPS, SSM_STATE)
    wbuf = cache_k_win.shape[2]
    nw = min(WINDOW, wbuf + 1)
    win_s = lambda old, new: jnp.concatenate([old[0].reshape(DB, wbuf, KV_WIDTH), new.reshape(DB, 1, KV_WIDTH)],
                                             axis=1)[:, wbuf + 1 - nw:]
    return (y_prompt.reshape(B, T, D_MODEL), y_sample.reshape(DB, 1, D_MODEL),
            kv5(kc, B), kv5(vc, B), kv5(ks, B), kv5(vs, B), kv5(kw[:, T - wb:], B), kv5(vw[:, T - wb:], B),
            st(hp_re, B), st(hp_im, B),
            kv5(kcs, DB), kv5(vcs, DB), kv5(kss, DB), kv5(vss, DB),
            kv5(win_s(cache_k_win, kws), DB), kv5(win_s(cache_v_win, vws), DB),
            st(hs_re, DB), st(hs_im, DB))
```

```python
import functools
import math

import jax
import jax.numpy as jnp
from jax import lax
from jax.experimental import pallas as pl
from jax.experimental.pallas import tpu as pltpu

F32 = jnp.float32
BF16 = jnp.bfloat16

D_MODEL = 1024
HEAD_DIM = 64
N_HEADS = 8
N_KV = 2
Q_PER_KV = 4
ATT_WIDTH = 512
KV_WIDTH = 128
CMP_LEN = 32
CMP_STRIDE = 16
SEL_BLOCK = 64
SEL_TOPN = 16
WINDOW = 512
ROPE_THETA = 10000.0
PAGE_SIZE = 128
SSM_GROUP = 16
SSM_WIDTH = 512
SSM_GROUPS = 32
SSM_STATE = 64
PEER_HEADS = 8
PEER_NKEYS = 128
PEER_QDIM = 256
PEER_TOPK = 16
RMS_EPS = 1e-6
NEG = -1e30
BIG = 1e9
TINY = 1e-30
LOWEST = -3.0e38

LANE = 128
VMEM_LIMIT = 56 * 1024 * 1024

_NT = (((1,), (1,)), ((), ()))


def _params(sem, vmem=VMEM_LIMIT):
    return pltpu.CompilerParams(dimension_semantics=sem, vmem_limit_bytes=vmem)


def _split(x):
    hi = x.astype(BF16)
    lo = (x - hi.astype(F32)).astype(BF16)
    return hi, lo


def _dot(a, b):
    return jnp.dot(a, b, preferred_element_type=F32)


def _dot_nt(a, b):
    return lax.dot_general(a, b, _NT, preferred_element_type=F32)


def _dot3(a_hi, a_lo, b_hi, b_lo):
    return _dot(a_hi, b_hi) + (_dot(a_hi, b_lo) + _dot(a_lo, b_hi))


def _dot3_nt(a_hi, a_lo, b_hi, b_lo):
    return _dot_nt(a_hi, b_hi) + (_dot_nt(a_hi, b_lo) + _dot_nt(a_lo, b_hi))


def _rms(x, g):
    return x * lax.rsqrt(jnp.mean(x * x, axis=-1, keepdims=True) + RMS_EPS) * g


def _softmax_rows(s, mask):
    s = jnp.where(mask, s, NEG)
    e = jnp.where(mask, jnp.exp(s - jnp.max(s, axis=-1, keepdims=True)), 0.0)
    return e / jnp.maximum(jnp.sum(e, axis=-1, keepdims=True), TINY)


_SEG = {'q': (0, 512), 'kc': (512, 640), 'vc': (640, 768), 'ks': (768, 896), 'vs': (896, 1024),
        'kw': (1024, 1152), 'vw': (1152, 1280), 'gt': (1280, 1408), 'u': (1408, 1920), 'gm': (1920, 3968)}
_PROJ_COLS = 3968


def _proj_kernel(*refs, precise):
    x_ref, g_ref, cos_ref, sin_ref = refs[:4]
    n_w = 2 if precise else 1
    w_refs = refs[4:4 + n_w]
    q_ref, kc_ref, vc_ref, ks_ref, vs_ref, kw_ref, vw_ref, gt_ref, u_ref, gm_ref = refs[4 + n_w:]
    h = _rms(x_ref[0], g_ref[...])
    h_hi = h.astype(BF16)
    h_lo = (h - h_hi.astype(F32)).astype(BF16) if precise else None

    def mm(c0, c1):
        z = _dot(h_hi, w_refs[0][:, c0:c1])
        if precise:
            z = z + (_dot(h_hi, w_refs[1][:, c0:c1]) + _dot(h_lo, w_refs[0][:, c0:c1]))
        return z

    cos = cos_ref[...]
    sin = sin_ref[...]
    first = (lax.broadcasted_iota(jnp.int32, (1, LANE), 1) % HEAD_DIM) < (HEAD_DIM // 2)

    def rope(z):
        rot = jnp.where(first, pltpu.roll(z, LANE - HEAD_DIM // 2, 1), pltpu.roll(z, HEAD_DIM // 2, 1))
        return z * cos + rot * sin

    for i in range(4):
        q_ref[0, :, i * LANE:(i + 1) * LANE] = rope(mm(i * LANE, (i + 1) * LANE))
    kc_ref[0] = rope(mm(*_SEG['kc']))
    vc_ref[0] = mm(*_SEG['vc'])
    ks_ref[0] = rope(mm(*_SEG['ks']))
    vs_ref[0] = mm(*_SEG['vs'])
    kw_ref[0] = rope(mm(*_SEG['kw']))
    vw_ref[0] = mm(*_SEG['vw'])
    gt_ref[0] = mm(*_SEG['gt'])
    u_ref[0] = mm(*_SEG['u'])
    for i in range(4):
        c0 = _SEG['gm'][0] + i * 512
        gm_ref[0, :, i * 512:(i + 1) * 512] = mm(c0, c0 + 512)


def _project(x, gain, cos, sin, w_list, tm, precise):
    B, T, D = x.shape
    widths = [512, 128, 128, 128, 128, 128, 128, 128, 512, 2048]
    tok = lambda w: pl.BlockSpec((1, tm, w), lambda b, t: (b, t, 0))
    const = lambda a: pl.BlockSpec(a.shape, lambda b, t: (0,) * a.ndim)
    return pl.pallas_call(
        functools.partial(_proj_kernel, precise=precise),
        grid=(B, T // tm),
        in_specs=[tok(D), const(gain), pl.BlockSpec((tm, LANE), lambda b, t: (t, 0)),
                  pl.BlockSpec((tm, LANE), lambda b, t: (t, 0))] + [const(w) for w in w_list],
        out_specs=[tok(w) for w in widths],
        out_shape=[jax.ShapeDtypeStruct((B, T, w), F32) for w in widths],
        compiler_params=_params(("parallel", "arbitrary")),
        name="proj",
    )(x, gain, cos, sin, *w_list)


def _compress_rows(x, last_p1, wc_ref, pe_ref, w2_ref):
    C = x.shape[0]
    p = _dot(x.astype(BF16), wc_ref[...])
    pb = _dot(pe_ref[...].astype(BF16), wc_ref[...])
    bias = pb[0:1, :LANE] + pb[1:2, LANE:]
    p1 = pltpu.roll(p[:, LANE:], C - 1, 0)
    if last_p1 is not None:
        row = lax.broadcasted_iota(jnp.int32, (C, 1), 0)
        p1 = jnp.where(row == C - 1, last_p1, p1)
    hid = p[:, :LANE] + p1 + bias
    return _dot(jax.nn.gelu(hid).astype(BF16), w2_ref[...])


def _compress_prompt_kernel(xk_ref, xv_ref, wck_ref, pek_ref, w2k_ref, wcv_ref, pev_ref, w2v_ref, ck_ref, cv_ref):
    ck_ref[0] = _compress_rows(xk_ref[0], None, wck_ref, pek_ref, w2k_ref)
    cv_ref[0] = _compress_rows(xv_ref[0], None, wcv_ref, pev_ref, w2v_ref)


def _compress_prompt(xk, xv, cw):
    B, C, W = xk.shape
    seq = pl.BlockSpec((1, C, W), lambda b: (b, 0, 0))
    const = lambda a: pl.BlockSpec(a.shape, lambda b: (0,) * a.ndim)
    out = pl.BlockSpec((1, C, LANE), lambda b: (b, 0, 0))
    return pl.pallas_call(
        _compress_prompt_kernel,
        grid=(B,),
        in_specs=[seq, seq] + [const(a) for a in cw],
        out_specs=[out, out],
        out_shape=[jax.ShapeDtypeStruct((B, C, LANE), F32)] * 2,
        compiler_params=_params(("arbitrary",)),
        name="compress_prompt",
    )(xk, xv, *cw)


def _paged_fetch(pt_ref, pools, bufs, sems, n_pages, dst):
    b = pl.program_id(0)
    nb = pl.num_programs(0)
    slot = b % 2

    def copies(bb, sl):
        return [pltpu.make_async_copy(pool.at[pt_ref[bb, p]], dst(buf, sl, p), sem.at[sl])
                for pool, buf, sem in zip(pools, bufs, sems) for p in range(n_pages)]

    @pl.when(b == 0)
    def _():
        for cp in copies(b, slot):
            cp.start()

    @pl.when(b + 1 < nb)
    def _():
        for cp in copies(b + 1, 1 - slot):
            cp.start()

    for cp in copies(b, slot):
        cp.wait()
    return slot


def _compress_sample_kernel(pt_ref, kn_ref, vn_ref, perm_ref, wck_ref, pek_ref, w2k_ref, wcv_ref, pev_ref, w2v_ref,
                            kpool_ref, vpool_ref, ck_ref, cv_ref, kbuf, vbuf, ksem, vsem, taps_sc, *, n_pages):
    slot = _paged_fetch(pt_ref, (kpool_ref, vpool_ref), (kbuf, vbuf), (ksem, vsem), n_pages,
                        lambda buf, sl, p: buf.at[sl, p])
    cpp = PAGE_SIZE // CMP_STRIDE

    def one(buf, new_ref, wc_ref, pe_ref, w2_ref, out_ref):
        def flip(pp, carry):
            pair = buf[slot, pl.ds(pl.multiple_of(pp * 2, 2), 2)].reshape(2 * KV_WIDTH, PAGE_SIZE)
            rows = _dot_nt(perm_ref[...], pair.astype(BF16))
            c0 = pl.multiple_of(pp * 2 * cpp, 2 * cpp)
            for s in range(CMP_STRIDE):
                taps_sc[s, pl.ds(c0, 2 * cpp), :] = jnp.concatenate(
                    [rows[s * cpp:(s + 1) * cpp, :KV_WIDTH], rows[s * cpp:(s + 1) * cpp, KV_WIDTH:]], axis=0)
            return carry

        lax.fori_loop(0, n_pages // 2, flip, 0, unroll=16)
        x = jnp.concatenate([taps_sc[s] for s in range(CMP_STRIDE)], axis=1)
        new = jnp.broadcast_to(new_ref[0], (8, LANE)).astype(BF16)
        last_p1 = _dot(new, wc_ref[0:LANE, LANE:])[0:1]
        out_ref[0] = _compress_rows(x, last_p1, wc_ref, pe_ref, w2_ref)

    one(kbuf, kn_ref, wck_ref, pek_ref, w2k_ref, ck_ref)
    one(vbuf, vn_ref, wcv_ref, pev_ref, w2v_ref, cv_ref)


def _compress_sample(page_table, kpool, vpool, k_new, v_new, cw):
    DB, n_pages = page_table.shape
    C = n_pages * PAGE_SIZE // CMP_STRIDE
    new = pl.BlockSpec((1, 1, LANE), lambda b, pt: (b, 0, 0))
    const = lambda a: pl.BlockSpec(a.shape, lambda b, pt: (0,) * a.ndim)
    hbm = pl.BlockSpec(memory_space=pl.ANY)
    out = pl.BlockSpec((1, C, LANE), lambda b, pt: (b, 0, 0))
    page_buf = pltpu.VMEM((2, n_pages, KV_WIDTH, PAGE_SIZE), F32)
    cpp = PAGE_SIZE // CMP_STRIDE
    tok = jnp.arange(PAGE_SIZE)
    perm = ((tok[:, None] % cpp) * CMP_STRIDE + tok[:, None] // cpp == tok[None, :]).astype(BF16)
    return pl.pallas_call(
        functools.partial(_compress_sample_kernel, n_pages=n_pages),
        grid_spec=pltpu.PrefetchScalarGridSpec(
            num_scalar_prefetch=1, grid=(DB,),
            in_specs=[new, new, const(perm)] + [const(a) for a in cw] + [hbm, hbm],
            out_specs=[out, out],
            scratch_shapes=[page_buf, page_buf, pltpu.SemaphoreType.DMA((2,)), pltpu.SemaphoreType.DMA((2,)),
                            pltpu.VMEM((CMP_STRIDE, C, KV_WIDTH), F32)]),
        out_shape=[jax.ShapeDtypeStruct((DB, C, LANE), F32)] * 2,
        compiler_params=_params(("arbitrary",)),
        name="compress_sample",
    )(page_table, k_new, v_new, perm, *cw, kpool, vpool)


def _select_blocks(imp, qpos, n_real, axis=1):
    j_shape = (1, imp.shape[1]) if axis == 1 else (imp.shape[0], 1)
    j = lax.broadcasted_iota(jnp.int32, j_shape, axis)
    valid = j * SEL_BLOCK <= qpos
    cur = qpos // SEL_BLOCK
    force = (j == 0) | (j == cur) | (j == cur - 1)
    score = jnp.where(valid & force, BIG, jnp.where(valid, imp, -BIG))
    rank = jnp.zeros(imp.shape, F32)
    for jp in range(n_real):
        other = score[:, jp:jp + 1] if axis == 1 else score[jp:jp + 1, :]
        earlier = jnp.where(j > jp, 1.0, 0.0)
        rank = rank + jnp.where(other > score, 1.0, jnp.where(other == score, earlier, 0.0))
    return jnp.where(rank < SEL_TOPN, 1.0, 0.0)


def _attn_prompt_kernel(q_ref, gt_ref, ck_ref, cv_ref, ks_ref, vs_ref, kw_ref, vw_ref, msel_ref, exp_ref, o_ref,
                        m_sc, acc_sc, s_sc, *, tq, kc, wb):
    s0 = pl.program_id(1) * tq
    R = Q_PER_KV * tq
    ncb = ck_ref.shape[1]
    qpos_r = s0 + lax.broadcasted_iota(jnp.int32, (R, 1), 0) % tq
    qpos_t = s0 + lax.broadcasted_iota(jnp.int32, (tq, 1), 0)
    qpos_l = s0 + lax.broadcasted_iota(jnp.int32, (1, tq), 1)
    sig = jax.nn.sigmoid(gt_ref[0])
    stack = lambda f: jnp.concatenate([f(r) for r in range(Q_PER_KV)], axis=0)

    def rows_of(bias, r0, n):
        if n >= tq:
            return jnp.concatenate([bias] * (n // tq), axis=0)
        return bias[r0 % tq:r0 % tq + n]

    n_chunks = (s0 + tq + kc - 1) // kc
    group_lanes = lambda g: slice(g * HEAD_DIM, (g + 1) * HEAD_DIM)
    ones_lane = lambda g: slice((1 - g) * HEAD_DIM, (1 - g) * HEAD_DIM + 1)

    def with_ones(v, g):
        own = (lax.broadcasted_iota(jnp.int32, (1, KV_WIDTH), 1) // HEAD_DIM) == g
        return jnp.where(own, v, 1.0).astype(BF16)

    pre = []
    for g in range(N_KV):
        gl = group_lanes(g)
        qg = stack(lambda r: q_ref[0, :, (g * Q_PER_KV + r) * HEAD_DIM:(g * Q_PER_KV + r + 1) * HEAD_DIM])
        qg = qg * (HEAD_DIM ** -0.5)
        q_hi = qg.astype(BF16)

        s_c = _dot_nt(q_hi, ck_ref[0, :, gl].astype(BF16))
        cend = lax.broadcasted_iota(jnp.int32, (1, ncb), 1) * CMP_STRIDE + (CMP_LEN - 1)
        p_c = _softmax_rows(s_c, cend <= qpos_r)
        o_c = _dot(p_c.astype(BF16), cv_ref[0, :, gl].astype(BF16))
        p_sum = p_c[0:tq]
        for r in range(1, Q_PER_KV):
            p_sum = p_sum + p_c[r * tq:(r + 1) * tq]
        ps_hi, ps_lo = _split(p_sum)
        msel_t = msel_ref[...]
        imp_t = _dot_nt(msel_t, ps_hi) + _dot_nt(msel_t, ps_lo)
        sel = _select_blocks(imp_t, qpos_l, imp_t.shape[0], axis=0).T
        unpicked = ((sel - 1.0) * -NEG).astype(BF16)

        wl = WINDOW + tq
        w0 = pl.multiple_of(jnp.maximum(s0 - WINDOW, 0), tq)
        kpos = w0 + lax.broadcasted_iota(jnp.int32, (1, wl), 1)
        dlt = qpos_t - kpos
        bias_w = jnp.where((dlt >= 0) & (dlt < WINDOW), 0.0, NEG)
        k_w = kw_ref[0, pl.ds(w0, wl), gl].astype(BF16)
        v_w = with_ones(vw_ref[0, pl.ds(w0, wl), :], g)
        o_w = []
        for r0 in range(0, R, wb):
            s_w = _dot_nt(q_hi[r0:r0 + wb], k_w) + rows_of(bias_w, r0, wb)
            p_w = jnp.exp(s_w - jnp.max(s_w, axis=-1, keepdims=True))
            ov = _dot(p_w.astype(BF16), v_w)
            o_w.append(ov[:, gl] / jnp.maximum(ov[:, ones_lane(g)], TINY))
        pre.append((q_hi, unpicked, o_c, jnp.concatenate(o_w, axis=0)))

    heads = []
    for g in range(N_KV):
        gl = group_lanes(g)
        q_hi, unpicked, o_c, o_w = pre[g]
        m_sc[...] = jnp.full((R, LANE), NEG, F32)

        def sweep_max(c, carry):
            off = pl.multiple_of(c * kc, kc)
            k = ks_ref[0, pl.ds(off, kc), gl].astype(BF16)
            kpos = off + lax.broadcasted_iota(jnp.int32, (1, kc), 1)
            bias = _dot(unpicked, exp_ref[c]) + jnp.where(kpos <= qpos_t, 0.0, NEG)
            s = _dot_nt(q_hi, k) + rows_of(bias, 0, R)
            s_sc[c] = s
            mx = s[:, 0:LANE]
            for t in range(1, kc // LANE):
                mx = jnp.maximum(mx, s[:, t * LANE:(t + 1) * LANE])
            m_sc[...] = jnp.maximum(m_sc[...], mx)
            return carry

        lax.fori_loop(0, n_chunks, sweep_max, 0)
        m_sc[...] = jnp.broadcast_to(jnp.max(m_sc[...], axis=-1, keepdims=True), (R, LANE))
        acc_sc[...] = jnp.zeros((R, KV_WIDTH), F32)

        def sweep_pv(c, carry):
            off = pl.multiple_of(c * kc, kc)
            p = jnp.exp(s_sc[c] - jnp.concatenate([m_sc[...]] * (kc // LANE), axis=1))
            acc_sc[...] += _dot(p.astype(BF16), with_ones(vs_ref[0, pl.ds(off, kc), :], g))
            return carry

        lax.fori_loop(0, n_chunks, sweep_pv, 0)
        o_s = acc_sc[:, gl] / jnp.maximum(acc_sc[:, ones_lane(g)], TINY)

        gate = lambda i: stack(lambda r: sig[:, (g * Q_PER_KV + r) * 3 + i:(g * Q_PER_KV + r) * 3 + i + 1])
        og = gate(0) * o_c + gate(1) * o_s + gate(2) * o_w
        heads += [og[r * tq:(r + 1) * tq] for r in range(Q_PER_KV)]
    o_ref[0] = jnp.concatenate(heads, axis=1)


def _attn_prompt(q, gt, ck, cv, ks, vs, kw, vw, tq=128, kc=512, wb=256):
    B, T, _ = q.shape
    assert T % kc == 0 and T % tq == 0 and T >= WINDOW + tq and (tq % wb == 0 or wb % tq == 0)
    ncb = ck.shape[1]
    ns = T // SEL_BLOCK
    i = jnp.arange(ncb)[None, :]
    j = jnp.arange(ns)[:, None]
    msel = ((i * CMP_STRIDE <= j * SEL_BLOCK + SEL_BLOCK - 1)
            & (i * CMP_STRIDE + CMP_LEN - 1 >= j * SEL_BLOCK)).astype(BF16)
    key_blk = (jnp.arange(T) // SEL_BLOCK).reshape(T // kc, 1, kc)
    expand = (key_blk == jnp.arange(ns)[None, :, None]).astype(BF16)
    tok = lambda w: pl.BlockSpec((1, tq, w), lambda b, t: (b, t, 0))
    seq = lambda a: pl.BlockSpec((1,) + a.shape[1:], lambda b, t: (b, 0, 0))
    const = lambda a: pl.BlockSpec(a.shape, lambda b, t: (0,) * a.ndim)
    R = Q_PER_KV * tq
    return pl.pallas_call(
        functools.partial(_attn_prompt_kernel, tq=tq, kc=kc, wb=wb),
        grid=(B, T // tq),
        in_specs=[tok(ATT_WIDTH), tok(LANE), seq(ck), seq(cv), seq(ks), seq(vs), seq(kw), seq(vw),
                  const(msel), const(expand)],
        out_specs=tok(ATT_WIDTH),
        out_shape=jax.ShapeDtypeStruct((B, T, ATT_WIDTH), F32),
        scratch_shapes=[pltpu.VMEM((R, LANE), F32), pltpu.VMEM((R, KV_WIDTH), F32), pltpu.VMEM((T // kc, R, kc), F32)],
        compiler_params=_params(("parallel", "arbitrary")),
        name="attn_prompt",
    )(q, gt, ck, cv, ks, vs, kw, vw, msel, expand)


def _group_lanes():
    row = lax.broadcasted_iota(jnp.int32, (N_HEADS, LANE), 0) // Q_PER_KV
    lane = lax.broadcasted_iota(jnp.int32, (N_HEADS, LANE), 1) // HEAD_DIM
    return row == lane


def _attn_sample_cmp_kernel(q_ref, ck_ref, cv_ref, msel_ref, oc_ref, sel_ref, *, qpos, n_blocks):
    q_hi, q_lo = _split(q_ref[0] * (HEAD_DIM ** -0.5))
    k_hi, k_lo = _split(ck_ref[0])
    s_c = _dot3_nt(q_hi, q_lo, k_hi, k_lo)
    ncb = s_c.shape[1]
    cend = lax.broadcasted_iota(jnp.int32, (1, ncb), 1) * CMP_STRIDE + (CMP_LEN - 1)
    p_c = _softmax_rows(s_c, cend <= qpos)
    o_c = _dot(p_c.astype(BF16), cv_ref[0].astype(BF16))
    oc_ref[0] = jnp.where(_group_lanes(), o_c, 0.0)
    top = jnp.sum(p_c[0:Q_PER_KV], axis=0, keepdims=True)
    bot = jnp.sum(p_c[Q_PER_KV:], axis=0, keepdims=True)
    row = lax.broadcasted_iota(jnp.int32, (N_HEADS, 1), 0)
    p_sum = jnp.where(row < Q_PER_KV, top, bot)
    ps_hi, ps_lo = _split(p_sum)
    msel = msel_ref[...]
    imp = _dot(ps_hi, msel) + _dot(ps_lo, msel)
    sel_ref[0] = _select_blocks(imp, jnp.full((N_HEADS, 1), qpos, jnp.int32), n_blocks)


def _attn_sample_cmp(qbd, ck, cv, qpos, n_blocks, ns_pad):
    DB, ncb, _ = ck.shape
    i = jnp.arange(ncb)[:, None]
    j = jnp.arange(ns_pad)[None, :]
    msel = ((i * CMP_STRIDE <= j * SEL_BLOCK + SEL_BLOCK - 1) & (i * CMP_STRIDE + CMP_LEN - 1 >= j * SEL_BLOCK)
            & (j < n_blocks)).astype(BF16)
    row = lambda a: pl.BlockSpec((1,) + a.shape[1:], lambda b: (b, 0, 0))
    return pl.pallas_call(
        functools.partial(_attn_sample_cmp_kernel, qpos=qpos, n_blocks=n_blocks),
        grid=(DB,),
        in_specs=[row(qbd), row(ck), row(cv), pl.BlockSpec(msel.shape, lambda b: (0, 0))],
        out_specs=[pl.BlockSpec((1, N_HEADS, LANE), lambda b: (b, 0, 0)),
                   pl.BlockSpec((1, N_HEADS, ns_pad), lambda b: (b, 0, 0))],
        out_shape=[jax.ShapeDtypeStruct((DB, N_HEADS, LANE), F32), jax.ShapeDtypeStruct((DB, N_HEADS, ns_pad), F32)],
        compiler_params=_params(("arbitrary",)),
        name="attn_sample_cmp",
    )(qbd, ck, cv, msel)


def _attn_sample_kernel(pt_ref, q_ref, sel_ref, oc_ref, gt_ref, ksn_ref, vsn_ref, kwb_ref, vwb_ref, kwn_ref, vwn_ref,
                        exp_ref, kpool_ref, vpool_ref, o_ref, kbuf, vbuf, ksem, vsem, *, n_pages, qpos):
    slot = _paged_fetch(pt_ref, (kpool_ref, vpool_ref), (kbuf, vbuf), (ksem, vsem), n_pages,
                        lambda buf, sl, p: buf.at[sl, :, pl.ds(p * PAGE_SIZE, PAGE_SIZE)])
    past = n_pages * PAGE_SIZE
    q = q_ref[0] * (HEAD_DIM ** -0.5)
    q_b = q.astype(BF16)
    sel = sel_ref[0]

    def attend(kt_old, vt_old, mask_old, k_new, v_new, mask_new):
        s = jnp.where(mask_old, _dot(q_b, kt_old.astype(BF16)), NEG)
        s_n = jnp.where(mask_new, jnp.sum(q * k_new, axis=-1, keepdims=True), NEG)
        m = jnp.maximum(jnp.max(s, axis=-1, keepdims=True), s_n)
        e = jnp.where(mask_old, jnp.exp(s - m), 0.0)
        e_n = jnp.where(mask_new, jnp.exp(s_n - m), 0.0)
        den = jnp.maximum(jnp.sum(e, axis=-1, keepdims=True) + e_n, TINY)
        return (_dot_nt(e.astype(BF16), vt_old.astype(BF16)) + e_n * v_new) / den

    picked = _dot(sel.astype(BF16), exp_ref[...])
    kpos = lax.broadcasted_iota(jnp.int32, (1, past), 1)
    nb_new = past // SEL_BLOCK
    o_s = attend(kbuf[slot], vbuf[slot], (picked > 0.5) & (kpos <= qpos), ksn_ref[0], vsn_ref[0],
                 (sel[:, nb_new:nb_new + 1] > 0.5) & (past <= qpos))
    wbuf = kwb_ref.shape[2]
    dlt = qpos - (past - wbuf + lax.broadcasted_iota(jnp.int32, (1, wbuf), 1))
    o_w = attend(kwb_ref[0], vwb_ref[0], (dlt >= 0) & (dlt < WINDOW), kwn_ref[0], vwn_ref[0],
                 jnp.full((N_HEADS, 1), (qpos - past >= 0) & (qpos - past < WINDOW)))
    sig = jax.nn.sigmoid(gt_ref[0])
    o = sig[:, 0:1] * oc_ref[0] + sig[:, 1:2] * o_s + sig[:, 2:3] * o_w
    o = jnp.where(_group_lanes(), o, 0.0)
    o_ref[0] = o[:, :HEAD_DIM] + o[:, HEAD_DIM:]


def _attn_sample(page_table, qbd, sel, oc, gt3, ksn, vsn, kwb, vwb, kwn, vwn, kpool, vpool, qpos):
    DB, n_pages = page_table.shape
    past = n_pages * PAGE_SIZE
    nsp = sel.shape[-1]
    expand = ((jnp.arange(past) // SEL_BLOCK)[None, :] == jnp.arange(nsp)[:, None]).astype(BF16)
    row = lambda a: pl.BlockSpec((1,) + a.shape[1:], lambda b, pt: (b, 0, 0))
    hbm = pl.BlockSpec(memory_space=pl.ANY)
    ins = [qbd, sel, oc, gt3, ksn, vsn, kwb, vwb, kwn, vwn]
    return pl.pallas_call(
        functools.partial(_attn_sample_kernel, n_pages=n_pages, qpos=qpos),
        grid_spec=pltpu.PrefetchScalarGridSpec(
            num_scalar_prefetch=1, grid=(DB,),
            in_specs=[row(a) for a in ins] + [pl.BlockSpec(expand.shape, lambda b, pt: (0, 0)), hbm, hbm],
            out_specs=pl.BlockSpec((1, N_HEADS, HEAD_DIM), lambda b, pt: (b, 0, 0)),
            scratch_shapes=[pltpu.VMEM((2, KV_WIDTH, past), F32), pltpu.VMEM((2, KV_WIDTH, past), F32),
                            pltpu.SemaphoreType.DMA((2,)), pltpu.SemaphoreType.DMA((2,))]),
        out_shape=jax.ShapeDtypeStruct((DB, N_HEADS, HEAD_DIM), F32),
        compiler_params=_params(("arbitrary",)),
        name="attn_sample",
    )(page_table, *ins, expand, kpool, vpool)


_SSM_HALF = SSM_WIDTH // 2
_SSM_ROW = SSM_GROUPS // 2 * SSM_STATE


def _ssm_params(a_re, a_im, log_dt, b_re, b_im, c_re, c_im, d):
    lam = lax.complex(a_re, a_im)
    step = jnp.exp(log_dt)[:, None]
    a_bar = jnp.exp(lam * step)
    b_bar = ((a_bar - 1.0) / lam)[..., None] * lax.complex(b_re, b_im)
    eye = jnp.eye(SSM_GROUPS // 2, dtype=F32)

    def b_mat(x):
        x = x.reshape(2, SSM_GROUPS // 2, SSM_STATE, SSM_GROUP)
        return jnp.einsum('hgpc,gk->hgckp', x, eye).reshape(2, _SSM_HALF, _SSM_ROW)

    def c_mat(x):
        x = x.reshape(2, SSM_GROUPS // 2, SSM_GROUP, SSM_STATE)
        return jnp.einsum('hgcp,gk->hgpkc', x, eye).reshape(2, _SSM_ROW, _SSM_HALF)

    bm = jnp.concatenate([b_mat(jnp.real(b_bar)), b_mat(jnp.imag(b_bar))], axis=2)
    cm = jnp.concatenate([c_mat(c_re), -c_mat(c_im)], axis=1)
    ar = jnp.real(a_bar).reshape(2, _SSM_ROW)
    ai = jnp.imag(a_bar).reshape(2, _SSM_ROW)
    bm_hi, bm_lo = _split(bm)
    cm_hi, cm_lo = _split(cm)
    return ar, ai, bm_hi, bm_lo, cm_hi, cm_lo, d.reshape(1, SSM_WIDTH)


def _ssm_prompt_kernel(u_ref, ar_ref, ai_ref, bh_ref, ch_ref, d_ref, y_ref, hr_ref, hi_ref,
                       sr_sc, si_sc, st_re, st_im, *, tc, nb):
    t = pl.program_id(0)
    rows = 2 * nb
    nlb = _SSM_ROW // LANE

    @pl.when(t == 0)
    def _():
        st_re[...] = jnp.zeros_like(st_re)
        st_im[...] = jnp.zeros_like(st_im)

    for b in range(nb):
        for hf in range(2):
            u_hi, u_lo = _split(u_ref[b, :, hf * _SSM_HALF:(hf + 1) * _SSM_HALF])
            bu = _dot(u_hi, bh_ref[hf]) + _dot(u_lo, bh_ref[hf])
            for k in range(nlb):
                sr_sc[k, pl.ds(b * 2 + hf, tc, stride=rows), :] = bu[:, k * LANE:(k + 1) * LANE]
                si_sc[k, pl.ds(b * 2 + hf, tc, stride=rows), :] = bu[:, _SSM_ROW + k * LANE:_SSM_ROW + (k + 1) * LANE]

    ar = ar_ref[...]
    ai = ai_ref[...]

    def step(i, carry):
        h_re, h_im = carry
        r0 = pl.multiple_of(i * rows, rows)
        n_re = ar * h_re - ai * h_im + sr_sc[:, pl.ds(r0, rows), :]
        n_im = ar * h_im + ai * h_re + si_sc[:, pl.ds(r0, rows), :]
        sr_sc[:, pl.ds(r0, rows), :] = n_re
        si_sc[:, pl.ds(r0, rows), :] = n_im
        return n_re, n_im

    h_re, h_im = lax.fori_loop(0, tc, step, (st_re[...], st_im[...]), unroll=4)
    st_re[...] = h_re
    st_im[...] = h_im
    hr_ref[...] = h_re
    hi_ref[...] = h_im

    for b in range(nb):
        for hf in range(2):
            gather = lambda sc: jnp.concatenate(
                [sc[k, pl.ds(b * 2 + hf, tc, stride=rows), :] for k in range(nlb)], axis=1)
            hs = jnp.concatenate([gather(sr_sc).astype(BF16), gather(si_sc).astype(BF16)], axis=1)
            cols = slice(hf * _SSM_HALF, (hf + 1) * _SSM_HALF)
            y_ref[b, :, cols] = _dot(hs, ch_ref[hf]) + d_ref[:, cols] * u_ref[b, :, cols]


def _ssm_prompt(u, sp, tc=128):
    B, T, W = u.shape
    ar, ai, bh, bl, ch, cl, d = sp
    rows = 2 * B
    nlb = _SSM_ROW // LANE
    tiles = lambda a: jnp.tile(a, (B, 1)).reshape(rows, nlb, LANE).transpose(1, 0, 2)
    const = lambda a: pl.BlockSpec(a.shape, lambda t: (0,) * a.ndim)
    blk = pl.BlockSpec((B, tc, W), lambda t: (0, t, 0))
    st = pl.BlockSpec((nlb, rows, LANE), lambda t: (0, 0, 0))
    ins = [tiles(ar), tiles(ai), bh, ch, d]
    st_shape = jax.ShapeDtypeStruct((nlb, rows, LANE), F32)
    y, h_re, h_im = pl.pallas_call(
        functools.partial(_ssm_prompt_kernel, tc=tc, nb=B),
        grid=(T // tc,),
        in_specs=[blk] + [const(a) for a in ins],
        out_specs=[blk, st, st],
        out_shape=[jax.ShapeDtypeStruct((B, T, W), F32), st_shape, st_shape],
        scratch_shapes=[pltpu.VMEM((nlb, tc * rows, LANE), F32), pltpu.VMEM((nlb, tc * rows, LANE), F32),
                        pltpu.VMEM((nlb, rows, LANE), F32), pltpu.VMEM((nlb, rows, LANE), F32)],
        compiler_params=_params(("arbitrary",)),
        name="ssm_prompt",
    )(u, *ins)
    rows_major = lambda a: a.transpose(1, 0, 2).reshape(rows, _SSM_ROW)
    return y, rows_major(h_re), rows_major(h_im)


def _ssm_sample_kernel(u_ref, h0r_ref, h0i_ref, ar_ref, ai_ref, bh_ref, bl_ref, ch_ref, cl_ref, d_ref,
                       y_ref, hr_ref, hi_ref):
    for hf in range(2):
        cols = slice(hf * _SSM_HALF, (hf + 1) * _SSM_HALF)
        lanes = slice(hf * _SSM_ROW, (hf + 1) * _SSM_ROW)
        u = u_ref[:, cols]
        u_hi, u_lo = _split(u)
        bu = _dot3(u_hi, u_lo, bh_ref[hf], bl_ref[hf])
        ar = ar_ref[hf:hf + 1, :]
        ai = ai_ref[hf:hf + 1, :]
        h_re = ar * h0r_ref[:, lanes] - ai * h0i_ref[:, lanes] + bu[:, :_SSM_ROW]
        h_im = ar * h0i_ref[:, lanes] + ai * h0r_ref[:, lanes] + bu[:, _SSM_ROW:]
        hr_ref[:, lanes] = h_re
        hi_ref[:, lanes] = h_im
        r_hi, r_lo = _split(h_re)
        i_hi, i_lo = _split(h_im)
        y = (_dot3(r_hi, r_lo, ch_ref[hf, :_SSM_ROW], cl_ref[hf, :_SSM_ROW])
             + _dot3(i_hi, i_lo, ch_ref[hf, _SSM_ROW:], cl_ref[hf, _SSM_ROW:]))
        y_ref[:, cols] = y + d_ref[:, cols] * u


def _ssm_sample(u, h0r, h0i, sp):
    n = u.shape[0]
    ins = [u, h0r, h0i, *sp]
    full = lambda a: pl.BlockSpec(a.shape, lambda i: (0,) * a.ndim)
    outs = [jax.ShapeDtypeStruct((n, SSM_WIDTH), F32), jax.ShapeDtypeStruct(h0r.shape, F32),
            jax.ShapeDtypeStruct(h0r.shape, F32)]
    return pl.pallas_call(
        _ssm_sample_kernel, grid=(1,),
        in_specs=[full(a) for a in ins], out_specs=[full(a) for a in outs], out_shape=outs,
        compiler_params=_params(("arbitrary",)),
        name="ssm_sample",
    )(*ins)


def _merge_kernel(x_ref, o_ref, y_ref, gm_ref, wa_ref, ws_ref, wo_ref, gf_ref, y2_ref, hf_ref):
    a = _dot(o_ref[...].astype(BF16), wa_ref[...])
    gl = _dot(jax.nn.gelu(y_ref[...]).astype(BF16), ws_ref[...])
    s = gl[:, :D_MODEL] * jax.nn.sigmoid(gl[:, D_MODEL:])
    gm = gm_ref[...]
    m = jax.nn.sigmoid(gm[:, :D_MODEL]) * a + jax.nn.sigmoid(gm[:, D_MODEL:]) * s
    y2 = x_ref[...] + _dot(m.astype(BF16), wo_ref[...])
    y2_ref[...] = y2
    hf_ref[...] = _rms(y2, gf_ref[...])


def _merge(x, o, y, gm, wa, ws, wo, gf, tm):
    n = x.shape[0]
    tok = lambda w: pl.BlockSpec((tm, w), lambda i: (i, 0))
    const = lambda a: pl.BlockSpec(a.shape, lambda i: (0,) * a.ndim)
    return pl.pallas_call(
        _merge_kernel, grid=(n // tm,),
        in_specs=[tok(D_MODEL), tok(ATT_WIDTH), tok(SSM_WIDTH), tok(2 * D_MODEL), const(wa), const(ws), const(wo),
                  const(gf)],
        out_specs=[tok(D_MODEL), tok(D_MODEL)],
        out_shape=[jax.ShapeDtypeStruct((n, D_MODEL), F32)] * 2,
        compiler_params=_params(("parallel",)),
        name="merge",
    )(x, o, y, gm, wa, ws, wo, gf)


def _cand_pairs():
    return [(a, b) for a in range(PEER_TOPK) for b in range(PEER_TOPK) if (a + 1) * (b + 1) <= PEER_TOPK]


def _top_values(s, k):
    vals = []
    for _ in range(k):
        mx = jnp.max(s, axis=0, keepdims=True)
        vals.append(mx)
        s = jnp.where(s == mx, LOWEST, s)
    return vals


def _peer_route_kernel(hf_ref, wqh_ref, wql_ref, k1h_ref, k1l_ref, k2h_ref, k2l_ref,
                       c1_ref, e1_ref, s2_ref, e2_ref, hh_sc, hl_sc):
    half = PEER_QDIM // 2

    @pl.when(pl.program_id(1) == 0)
    def _():
        hh_sc[...], hl_sc[...] = _split(hf_ref[...])

    q = _dot3(hh_sc[...], hl_sc[...], wqh_ref[...], wql_ref[...])
    q1h, q1l = _split(q[:, :half])
    q2h, q2l = _split(q[:, half:])
    s1 = _dot3_nt(k1h_ref[...], k1l_ref[...], q1h, q1l)
    s2 = _dot3_nt(k2h_ref[...], k2l_ref[...], q2h, q2l)
    v1 = _top_values(s1, PEER_TOPK)
    v2 = _top_values(s2, PEER_TOPK)
    cand = jnp.concatenate([v1[a] + v2[b] for a, b in _cand_pairs()], axis=0)
    thr = _top_values(cand, PEER_TOPK)[-1]
    top = v1[0] + v2[0]
    z = jnp.sum(jnp.where(cand >= thr, jnp.exp(cand - top), 0.0), axis=0, keepdims=True)
    c1 = jnp.full(s1.shape, -LOWEST, F32)
    for v in v2:
        c1 = jnp.where(s1 + v >= thr, v, c1)
    c1_ref[0] = c1
    s2_ref[0] = s2
    e1_ref[0] = jnp.exp(s1 - v1[0])
    e2_ref[0] = jnp.exp(s2 - v2[0]) / z


def _peer_route(hf, wq_hi, wq_lo, k1, k2, tn):
    n = hf.shape[0]
    k1h, k1l = _split(k1)
    k2h, k2l = _split(k2)
    const = lambda a: pl.BlockSpec(a.shape, lambda i, h: (0, 0))
    wq = pl.BlockSpec((D_MODEL, PEER_QDIM), lambda i, h: (0, h))
    keyed = pl.BlockSpec((1, PEER_NKEYS, tn), lambda i, h: (h, 0, i))
    shp = jax.ShapeDtypeStruct((PEER_HEADS, PEER_NKEYS, n), F32)
    return pl.pallas_call(
        _peer_route_kernel, grid=(n // tn, PEER_HEADS),
        in_specs=[pl.BlockSpec((tn, D_MODEL), lambda i, h: (i, 0)), wq, wq, const(k1h), const(k1l), const(k2h),
                  const(k2l)],
        out_specs=[keyed, keyed, keyed, keyed],
        out_shape=[shp, shp, shp, shp],
        scratch_shapes=[pltpu.VMEM((tn, D_MODEL), BF16), pltpu.VMEM((tn, D_MODEL), BF16)],
        compiler_params=_params(("parallel", "arbitrary")),
        name="peer_route",
    )(hf, wq_hi, wq_lo, k1h, k1l, k2h, k2l)


_PEER_JROWS = 32
_PEER_SLABS = 4


def _gelu_tanh(x):
    k = math.sqrt(2.0 / math.pi)
    hx = 0.5 * x
    return hx + hx * jnp.tanh(x * (k + (k * 0.044715) * (x * x)))


def _peer_main_kernel(hf_ref, y2_ref, u_ref, vt_ref, c1_ref, e1_ref, s2_ref, e2_ref, gn_ref, out_ref,
                      acc_sc, w_sc, act_sc, hfb_sc, *, tn, n_slab):
    e = pl.program_id(1)

    @pl.when(e == 0)
    def _():
        acc_sc[...] = jnp.zeros_like(acc_sc)
        hfb_sc[...] = hf_ref[...].astype(BF16)

    act_sc[...] = _gelu_tanh(_dot_nt(u_ref[...], hfb_sc[...]))

    for lc in range(tn // LANE):
        ln = slice(lc * LANE, (lc + 1) * LANE)

        def rows(jq, carry, ln=ln):
            j0 = pl.multiple_of(jq * _PEER_JROWS, _PEER_JROWS)
            for i0 in range(0, n_slab, _PEER_SLABS):
                slabs = range(i0, i0 + _PEER_SLABS)
                w = {ii: jnp.zeros((_PEER_JROWS, LANE), F32) for ii in slabs}
                for h in range(PEER_HEADS):
                    s2 = s2_ref[h, pl.ds(j0, _PEER_JROWS), ln]
                    e2 = e2_ref[h, pl.ds(j0, _PEER_JROWS), ln]
                    for ii in slabs:
                        w[ii] = w[ii] + e1_ref[h, ii:ii + 1, ln] * jnp.where(s2 >= c1_ref[h, ii:ii + 1, ln], e2, 0.0)
                for ii in slabs:
                    rs = pl.ds(ii * PEER_NKEYS + j0, _PEER_JROWS)
                    w_sc[rs, ln] = (w[ii] * act_sc[rs, ln]).astype(BF16)
            return carry

        lax.fori_loop(0, PEER_NKEYS // _PEER_JROWS, rows, 0)

    acc_sc[...] += _dot(vt_ref[...], w_sc[...])

    @pl.when(e == pl.num_programs(1) - 1)
    def _():
        out_ref[...] = _rms(y2_ref[...] + acc_sc[...].T, gn_ref[...])


def _peer_main(hf, y2, u_tab, vt_tab, c1, e1, s2, e2, gn, tn, n_slab=16):
    n = hf.shape[0]
    ec = n_slab * PEER_NKEYS
    n_exp = u_tab.shape[0]
    tok = pl.BlockSpec((tn, D_MODEL), lambda i, e: (i, 0))
    slab = pl.BlockSpec((PEER_HEADS, n_slab, tn), lambda i, e: (0, e, i))
    keyed = pl.BlockSpec((PEER_HEADS, PEER_NKEYS, tn), lambda i, e: (0, 0, i))
    return pl.pallas_call(
        functools.partial(_peer_main_kernel, tn=tn, n_slab=n_slab),
        grid=(n // tn, n_exp // ec),
        in_specs=[tok, tok, pl.BlockSpec((ec, D_MODEL), lambda i, e: (e, 0)),
                  pl.BlockSpec((D_MODEL, ec), lambda i, e: (0, e)), slab, slab, keyed, keyed,
                  pl.BlockSpec(gn.shape, lambda i, e: (0, 0))],
        out_specs=tok,
        out_shape=jax.ShapeDtypeStruct((n, D_MODEL), F32),
        scratch_shapes=[pltpu.VMEM((D_MODEL, tn), F32), pltpu.VMEM((ec, tn), BF16), pltpu.VMEM((ec, tn), F32),
                        pltpu.VMEM((tn, D_MODEL), BF16)],
        compiler_params=_params(("parallel", "arbitrary")),
        name="peer_main",
    )(hf, y2, u_tab, vt_tab, c1, e1, s2, e2, gn)


def _peer(hf, y2, wq_hi, wq_lo, k1, k2, u_tab, vt_tab, gn, tn):
    c1, e1, s2, e2 = _peer_route(hf, wq_hi, wq_lo, k1, k2, tn)
    return _peer_main(hf, y2, u_tab, vt_tab, c1, e1, s2, e2, gn, tn)


def _rope_tables(pos):
    half = HEAD_DIM // 2
    inv = ROPE_THETA ** (-jnp.arange(half, dtype=F32) / half)
    ang = pos.astype(F32)[:, None] * inv[None, :]
    cos, sin = jnp.cos(ang), jnp.sin(ang)
    return jnp.tile(jnp.concatenate([cos, cos], -1), (1, 2)), jnp.tile(jnp.concatenate([-sin, sin], -1), (1, 2))


def _proj_weight(w):
    gates = jnp.pad(w[:, 1280:1304], ((0, 0), (0, LANE - 3 * N_HEADS)))
    return jnp.concatenate([w[:, :1280], gates, w[:, 1304:1816], w[:, 1816:]], axis=1)


def _compress_weights(pe, w1, w2):
    r = CMP_LEN // CMP_STRIDE
    eye = jnp.eye(N_KV, dtype=F32)
    w1r = w1.reshape(r, CMP_STRIDE, HEAD_DIM, w1.shape[-1])
    wc = jnp.einsum('jsdh,gk->sgdjkh', w1r, eye).reshape(CMP_STRIDE * KV_WIDTH, r * N_KV * w1.shape[-1])
    pe2 = jnp.broadcast_to(pe.reshape(r, CMP_STRIDE, 1, HEAD_DIM), (r, CMP_STRIDE, N_KV, HEAD_DIM))
    pe2 = jnp.pad(pe2.reshape(r, CMP_STRIDE * KV_WIDTH), ((0, 8 - r), (0, 0)))
    w2bd = jnp.einsum('hd,gk->ghkd', w2, eye).reshape(N_KV * w2.shape[0], KV_WIDTH)
    return wc.astype(BF16), pe2, w2bd.astype(BF16)


def kernel(x_prompt, x_sample, cache_k_cmp, cache_v_cmp, cache_k_sel, cache_v_sel, cache_k_win, cache_v_win,
           state_ssm_re, state_ssm_im, page_table, norm_mix, w_in, cmp_pe_k, cmp_w1_k, cmp_w2_k, cmp_pe_v, cmp_w1_v,
           cmp_w2_v, ssm_a_re, ssm_a_im, ssm_log_dt, ssm_b_re, ssm_b_im, ssm_c_re, ssm_c_im, ssm_d, w_att_proj,
           w_ssm_glu, w_out, norm_ffn, peer_w_q, peer_keys1, peer_keys2, peer_u, peer_v, norm_final):
    assert w_in.shape[0] == 1, "single layer"
    B, T, _ = x_prompt.shape
    DB = x_sample.shape[0]
    n_pages = page_table.shape[1]
    past = n_pages * PAGE_SIZE
    n_pool = cache_k_cmp.shape[1]

    w_hi, w_lo = _split(_proj_weight(w_in[0]))
    g_mix = norm_mix[0].reshape(1, D_MODEL)
    cw = _compress_weights(cmp_pe_k[0], cmp_w1_k[0], cmp_w2_k[0]) + _compress_weights(cmp_pe_v[0], cmp_w1_v[0],
                                                                                       cmp_w2_v[0])
    sp = _ssm_params(ssm_a_re[0], ssm_a_im[0], ssm_log_dt[0], ssm_b_re[0], ssm_b_im[0], ssm_c_re[0], ssm_c_im[0],
                     ssm_d[0])
    wa, ws, wo = w_att_proj[0].astype(BF16), w_ssm_glu[0].astype(BF16), w_out[0].astype(BF16)
    g_ffn = norm_ffn[0].reshape(1, D_MODEL)
    g_fin = norm_final.reshape(1, D_MODEL)
    wq_hi, wq_lo = _split(peer_w_q[0])
    u_tab = peer_u[0].astype(BF16)
    vt_tab = peer_v[0].T.astype(BF16)

    cos_p, sin_p = _rope_tables(jnp.arange(T, dtype=jnp.int32))
    q, kc, vc, ks, vs, kw, vw, gt, u, gm = _project(x_prompt, g_mix, cos_p, sin_p, [w_hi], 512, False)
    chunks = lambda a: a.reshape(B, T // CMP_STRIDE, CMP_STRIDE * KV_WIDTH)
    ck, cv = _compress_prompt(chunks(kc), chunks(vc), cw)
    o_att = _attn_prompt(q, gt, ck, cv, ks, vs, kw, vw)
    y_ssm, hp_re, hp_im = _ssm_prompt(u, sp)
    n_p = B * T
    flat = lambda a: a.reshape(n_p, a.shape[-1])
    y2_p, hf_p = _merge(flat(x_prompt), flat(o_att), flat(y_ssm), flat(gm), wa, ws, wo, g_ffn, 512)
    y_prompt = _peer(hf_p, y2_p, wq_hi, wq_lo, peer_keys1[0], peer_keys2[0], u_tab, vt_tab, g_fin, 512)

    cos_s, sin_s = _rope_tables(jnp.full((DB,), past, jnp.int32))
    xs = x_sample.reshape(1, DB, D_MODEL)
    qs, kcs, vcs, kss, vss, kws, vws, gts, us, gms = [a[0] for a in
                                                      _project(xs, g_mix, cos_s, sin_s, [w_hi, w_lo], DB, True)]
    row3 = lambda a: a.reshape(DB, 1, KV_WIDTH)
    native = lambda c: jnp.transpose(c[0], (0, 2, 3, 1)).reshape(c.shape[1], KV_WIDTH, c.shape[2])
    cks, cvs = _compress_sample(page_table, native(cache_k_cmp), native(cache_v_cmp), row3(kcs), row3(vcs), cw)
    q5 = qs.reshape(DB, N_KV, Q_PER_KV, 1, HEAD_DIM) * jnp.eye(N_KV, dtype=F32).reshape(1, N_KV, 1, N_KV, 1)
    qbd = q5.reshape(DB, N_HEADS, KV_WIDTH)
    n_blocks = -(-(past + 1) // SEL_BLOCK)
    ns_pad = -(-n_blocks // LANE) * LANE
    oc, sel = _attn_sample_cmp(qbd, cks, cvs, past, n_blocks, ns_pad)
    gt3 = gts[:, :3 * N_HEADS].reshape(DB, N_HEADS, 3)
    o_s = _attn_sample(page_table, qbd, sel, oc, gt3, row3(kss), row3(vss), native(cache_k_win), native(cache_v_win),
                       row3(kws), row3(vws), native(cache_k_sel), native(cache_v_sel), past)
    h0r = state_ssm_re[0].reshape(DB, SSM_GROUPS * SSM_STATE)
    h0i = state_ssm_im[0].reshape(DB, SSM_GROUPS * SSM_STATE)
    ys_ssm, hs_re, hs_im = _ssm_sample(us, h0r, h0i, sp)
    y2_s, hf_s = _merge(x_sample.reshape(DB, D_MODEL), o_s.reshape(DB, ATT_WIDTH), ys_ssm, gms, wa, ws, wo, g_ffn, DB)
    y_sample = _peer(hf_s, y2_s, wq_hi, wq_lo, peer_keys1[0], peer_keys2[0], u_tab, vt_tab, g_fin, DB)

    kv5 = lambda a, n: a.reshape(1, n, -1, N_KV, HEAD_DIM)
    wb = min(WINDOW, T)
    st = lambda a, n: a.reshape(1, n, SSM_GROUPS, SSM_STATE)
    wbuf = cache_k_win.shape[2]
    nw = min(WINDOW, wbuf + 1)
    win_s = lambda old, new: jnp.concatenate([old[0].reshape(DB, wbuf, KV_WIDTH), new.reshape(DB, 1, KV_WIDTH)],
                                             axis=1)[:, wbuf + 1 - nw:]
    return (y_prompt.reshape(B, T, D_MODEL), y_sample.reshape(DB, 1, D_MODEL),
            kv5(kc, B), kv5(vc, B), kv5(ks, B), kv5(vs, B), kv5(kw[:, T - wb:], B), kv5(vw[:, T - wb:], B),
            st(hp_re, B), st(hp_im, B),
            kv5(kcs, DB), kv5(vcs, DB), kv5(kss, DB), kv5(vss, DB),
            kv5(win_s(cache_k_win, kws), DB), kv5(win_s(cache_v_win, vws), DB),
            st(hs_re, DB), st(hs_im, DB))
```

```python
import functools
import math

import jax
import jax.numpy as jnp
from jax import lax
from jax.experimental import pallas as pl
from jax.experimental.pallas import tpu as pltpu

F32 = jnp.float32
BF16 = jnp.bfloat16

D_MODEL = 1024
HEAD_DIM = 64
N_HEADS = 8
N_KV = 2
Q_PER_KV = 4
ATT_WIDTH = 512
KV_WIDTH = 128
CMP_LEN = 32
CMP_STRIDE = 16
SEL_BLOCK = 64
SEL_TOPN = 16
WINDOW = 512
ROPE_THETA = 10000.0
PAGE_SIZE = 128
SSM_GROUP = 16
SSM_WIDTH = 512
SSM_GROUPS = 32
SSM_STATE = 64
PEER_HEADS = 8
PEER_NKEYS = 128
PEER_QDIM = 256
PEER_TOPK = 16
RMS_EPS = 1e-6
NEG = -1e30
BIG = 1e9
TINY = 1e-30
LOWEST = -3.0e38

LANE = 128
VMEM_LIMIT = 56 * 1024 * 1024

_NT = (((1,), (1,)), ((), ()))


def _params(sem, vmem=VMEM_LIMIT):
    return pltpu.CompilerParams(dimension_semantics=sem, vmem_limit_bytes=vmem)


def _split(x):
    hi = x.astype(BF16)
    lo = (x - hi.astype(F32)).astype(BF16)
    return hi, lo


def _dot(a, b):
    return jnp.dot(a, b, preferred_element_type=F32)


def _dot_nt(a, b):
    return lax.dot_general(a, b, _NT, preferred_element_type=F32)


def _dot3(a_hi, a_lo, b_hi, b_lo):
    return _dot(a_hi, b_hi) + (_dot(a_hi, b_lo) + _dot(a_lo, b_hi))


def _dot3_nt(a_hi, a_lo, b_hi, b_lo):
    return _dot_nt(a_hi, b_hi) + (_dot_nt(a_hi, b_lo) + _dot_nt(a_lo, b_hi))


def _rms(x, g):
    return x * lax.rsqrt(jnp.mean(x * x, axis=-1, keepdims=True) + RMS_EPS) * g


def _softmax_rows(s, mask):
    s = jnp.where(mask, s, NEG)
    e = jnp.where(mask, jnp.exp(s - jnp.max(s, axis=-1, keepdims=True)), 0.0)
    return e / jnp.maximum(jnp.sum(e, axis=-1, keepdims=True), TINY)


_SEG = {'q': (0, 512), 'kc': (512, 640), 'vc': (640, 768), 'ks': (768, 896), 'vs': (896, 1024),
        'kw': (1024, 1152), 'vw': (1152, 1280), 'gt': (1280, 1408), 'u': (1408, 1920), 'gm': (1920, 3968)}
_PROJ_COLS = 3968


def _proj_kernel(*refs, precise):
    x_ref, g_ref, cos_ref, sin_ref = refs[:4]
    n_w = 2 if precise else 1
    w_refs = refs[4:4 + n_w]
    q_ref, kc_ref, vc_ref, ks_ref, vs_ref, kw_ref, vw_ref, gt_ref, u_ref, gm_ref = refs[4 + n_w:]
    h = _rms(x_ref[0], g_ref[...])
    h_hi = h.astype(BF16)
    h_lo = (h - h_hi.astype(F32)).astype(BF16) if precise else None

    def mm(c0, c1):
        z = _dot(h_hi, w_refs[0][:, c0:c1])
        if precise:
            z = z + (_dot(h_hi, w_refs[1][:, c0:c1]) + _dot(h_lo, w_refs[0][:, c0:c1]))
        return z

    cos = cos_ref[...]
    sin = sin_ref[...]
    first = (lax.broadcasted_iota(jnp.int32, (1, LANE), 1) % HEAD_DIM) < (HEAD_DIM // 2)

    def rope(z):
        rot = jnp.where(first, pltpu.roll(z, LANE - HEAD_DIM // 2, 1), pltpu.roll(z, HEAD_DIM // 2, 1))
        return z * cos + rot * sin

    for i in range(4):
        q_ref[0, :, i * LANE:(i + 1) * LANE] = rope(mm(i * LANE, (i + 1) * LANE))
    kc_ref[0] = rope(mm(*_SEG['kc']))
    vc_ref[0] = mm(*_SEG['vc'])
    ks_ref[0] = rope(mm(*_SEG['ks']))
    vs_ref[0] = mm(*_SEG['vs'])
    kw_ref[0] = rope(mm(*_SEG['kw']))
    vw_ref[0] = mm(*_SEG['vw'])
    gt_ref[0] = mm(*_SEG['gt'])
    u_ref[0] = mm(*_SEG['u'])
    for i in range(4):
        c0 = _SEG['gm'][0] + i * 512
        gm_ref[0, :, i * 512:(i + 1) * 512] = mm(c0, c0 + 512)


def _project(x, gain, cos, sin, w_list, tm, precise):
    B, T, D = x.shape
    widths = [512, 128, 128, 128, 128, 128, 128, 128, 512, 2048]
    tok = lambda w: pl.BlockSpec((1, tm, w), lambda b, t: (b, t, 0))
    const = lambda a: pl.BlockSpec(a.shape, lambda b, t: (0,) * a.ndim)
    return pl.pallas_call(
        functools.partial(_proj_kernel, precise=precise),
        grid=(B, T // tm),
        in_specs=[tok(D), const(gain), pl.BlockSpec((tm, LANE), lambda b, t: (t, 0)),
                  pl.BlockSpec((tm, LANE), lambda b, t: (t, 0))] + [const(w) for w in w_list],
        out_specs=[tok(w) for w in widths],
        out_shape=[jax.ShapeDtypeStruct((B, T, w), F32) for w in widths],
        compiler_params=_params(("parallel", "arbitrary")),
        name="proj",
    )(x, gain, cos, sin, *w_list)


def _compress_rows(x, last_p1, wc_ref, pe_ref, w2_ref):
    C = x.shape[0]
    p = _dot(x.astype(BF16), wc_ref[...])
    pb = _dot(pe_ref[...].astype(BF16), wc_ref[...])
    bias = pb[0:1, :LANE] + pb[1:2, LANE:]
    p1 = pltpu.roll(p[:, LANE:], C - 1, 0)
    if last_p1 is not None:
        row = lax.broadcasted_iota(jnp.int32, (C, 1), 0)
        p1 = jnp.where(row == C - 1, last_p1, p1)
    hid = p[:, :LANE] + p1 + bias
    return _dot(jax.nn.gelu(hid).astype(BF16), w2_ref[...])


def _compress_prompt_kernel(xk_ref, xv_ref, wck_ref, pek_ref, w2k_ref, wcv_ref, pev_ref, w2v_ref, ck_ref, cv_ref):
    ck_ref[0] = _compress_rows(xk_ref[0], None, wck_ref, pek_ref, w2k_ref)
    cv_ref[0] = _compress_rows(xv_ref[0], None, wcv_ref, pev_ref, w2v_ref)


def _compress_prompt(xk, xv, cw):
    B, C, W = xk.shape
    seq = pl.BlockSpec((1, C, W), lambda b: (b, 0, 0))
    const = lambda a: pl.BlockSpec(a.shape, lambda b: (0,) * a.ndim)
    out = pl.BlockSpec((1, C, LANE), lambda b: (b, 0, 0))
    return pl.pallas_call(
        _compress_prompt_kernel,
        grid=(B,),
        in_specs=[seq, seq] + [const(a) for a in cw],
        out_specs=[out, out],
        out_shape=[jax.ShapeDtypeStruct((B, C, LANE), F32)] * 2,
        compiler_params=_params(("arbitrary",)),
        name="compress_prompt",
    )(xk, xv, *cw)


def _paged_fetch(pt_ref, pools, bufs, sems, n_pages, dst):
    b = pl.program_id(0)
    nb = pl.num_programs(0)
    slot = b % 2

    def copies(bb, sl):
        return [pltpu.make_async_copy(pool.at[pt_ref[bb, p]], dst(buf, sl, p), sem.at[sl])
                for pool, buf, sem in zip(pools, bufs, sems) for p in range(n_pages)]

    @pl.when(b == 0)
    def _():
        for cp in copies(b, slot):
            cp.start()

    @pl.when(b + 1 < nb)
    def _():
        for cp in copies(b + 1, 1 - slot):
            cp.start()

    for cp in copies(b, slot):
        cp.wait()
    return slot


def _compress_sample_kernel(pt_ref, kn_ref, vn_ref, perm_ref, wck_ref, pek_ref, w2k_ref, wcv_ref, pev_ref, w2v_ref,
                            kpool_ref, vpool_ref, ck_ref, cv_ref, kbuf, vbuf, ksem, vsem, taps_sc, *, n_pages):
    slot = _paged_fetch(pt_ref, (kpool_ref, vpool_ref), (kbuf, vbuf), (ksem, vsem), n_pages,
                        lambda buf, sl, p: buf.at[sl, p])
    cpp = PAGE_SIZE // CMP_STRIDE

    def one(buf, new_ref, wc_ref, pe_ref, w2_ref, out_ref):
        def flip(pp, carry):
            pair = buf[slot, pl.ds(pl.multiple_of(pp * 2, 2), 2)].reshape(2 * KV_WIDTH, PAGE_SIZE)
            rows = _dot_nt(perm_ref[...], pair.astype(BF16))
            c0 = pl.multiple_of(pp * 2 * cpp, 2 * cpp)
            for s in range(CMP_STRIDE):
                taps_sc[s, pl.ds(c0, 2 * cpp), :] = jnp.concatenate(
                    [rows[s * cpp:(s + 1) * cpp, :KV_WIDTH], rows[s * cpp:(s + 1) * cpp, KV_WIDTH:]], axis=0)
            return carry

        lax.fori_loop(0, n_pages // 2, flip, 0, unroll=16)
        x = jnp.concatenate([taps_sc[s] for s in range(CMP_STRIDE)], axis=1)
        new = jnp.broadcast_to(new_ref[0], (8, LANE)).astype(BF16)
        last_p1 = _dot(new, wc_ref[0:LANE, LANE:])[0:1]
        out_ref[0] = _compress_rows(x, last_p1, wc_ref, pe_ref, w2_ref)

    one(kbuf, kn_ref, wck_ref, pek_ref, w2k_ref, ck_ref)
    one(vbuf, vn_ref, wcv_ref, pev_ref, w2v_ref, cv_ref)


def _compress_sample(page_table, kpool, vpool, k_new, v_new, cw):
    DB, n_pages = page_table.shape
    C = n_pages * PAGE_SIZE // CMP_STRIDE
    new = pl.BlockSpec((1, 1, LANE), lambda b, pt: (b, 0, 0))
    const = lambda a: pl.BlockSpec(a.shape, lambda b, pt: (0,) * a.ndim)
    hbm = pl.BlockSpec(memory_space=pl.ANY)
    out = pl.BlockSpec((1, C, LANE), lambda b, pt: (b, 0, 0))
    page_buf = pltpu.VMEM((2, n_pages, KV_WIDTH, PAGE_SIZE), F32)
    cpp = PAGE_SIZE // CMP_STRIDE
    tok = jnp.arange(PAGE_SIZE)
    perm = ((tok[:, None] % cpp) * CMP_STRIDE + tok[:, None] // cpp == tok[None, :]).astype(BF16)
    return pl.pallas_call(
        functools.partial(_compress_sample_kernel, n_pages=n_pages),
        grid_spec=pltpu.PrefetchScalarGridSpec(
            num_scalar_prefetch=1, grid=(DB,),
            in_specs=[new, new, const(perm)] + [const(a) for a in cw] + [hbm, hbm],
            out_specs=[out, out],
            scratch_shapes=[page_buf, page_buf, pltpu.SemaphoreType.DMA((2,)), pltpu.SemaphoreType.DMA((2,)),
                            pltpu.VMEM((CMP_STRIDE, C, KV_WIDTH), F32)]),
        out_shape=[jax.ShapeDtypeStruct((DB, C, LANE), F32)] * 2,
        compiler_params=_params(("arbitrary",)),
        name="compress_sample",
    )(page_table, k_new, v_new, perm, *cw, kpool, vpool)


def _select_blocks(imp, qpos, n_real, axis=1):
    j_shape = (1, imp.shape[1]) if axis == 1 else (imp.shape[0], 1)
    j = lax.broadcasted_iota(jnp.int32, j_shape, axis)
    valid = j * SEL_BLOCK <= qpos
    cur = qpos // SEL_BLOCK
    force = (j == 0) | (j == cur) | (j == cur - 1)
    score = jnp.where(valid & force, BIG, jnp.where(valid, imp, -BIG))
    rank = jnp.zeros(imp.shape, F32)
    for jp in range(n_real):
        other = score[:, jp:jp + 1] if axis == 1 else score[jp:jp + 1, :]
        earlier = jnp.where(j > jp, 1.0, 0.0)
        rank = rank + jnp.where(other > score, 1.0, jnp.where(other == score, earlier, 0.0))
    return jnp.where(rank < SEL_TOPN, 1.0, 0.0)


def _attn_prompt_kernel(q_ref, gt_ref, ck_ref, cv_ref, ks_ref, vs_ref, kw_ref, vw_ref, msel_ref, exp_ref, o_ref,
                        m_sc, acc_sc, s_sc, *, tq, kc, wb):
    s0 = pl.program_id(1) * tq
    R = Q_PER_KV * tq
    ncb = ck_ref.shape[1]
    qpos_r = s0 + lax.broadcasted_iota(jnp.int32, (R, 1), 0) % tq
    qpos_t = s0 + lax.broadcasted_iota(jnp.int32, (tq, 1), 0)
    qpos_l = s0 + lax.broadcasted_iota(jnp.int32, (1, tq), 1)
    sig = jax.nn.sigmoid(gt_ref[0])
    stack = lambda f: jnp.concatenate([f(r) for r in range(Q_PER_KV)], axis=0)

    def rows_of(bias, r0, n):
        if n >= tq:
            return jnp.concatenate([bias] * (n // tq), axis=0)
        return bias[r0 % tq:r0 % tq + n]

    n_chunks = (s0 + tq + kc - 1) // kc
    group_lanes = lambda g: slice(g * HEAD_DIM, (g + 1) * HEAD_DIM)
    ones_lane = lambda g: slice((1 - g) * HEAD_DIM, (1 - g) * HEAD_DIM + 1)

    def with_ones(v, g):
        own = (lax.broadcasted_iota(jnp.int32, (1, KV_WIDTH), 1) // HEAD_DIM) == g
        return jnp.where(own, v, 1.0).astype(BF16)

    pre = []
    for g in range(N_KV):
        gl = group_lanes(g)
        qg = stack(lambda r: q_ref[0, :, (g * Q_PER_KV + r) * HEAD_DIM:(g * Q_PER_KV + r + 1) * HEAD_DIM])
        qg = qg * (HEAD_DIM ** -0.5)
        q_hi = qg.astype(BF16)

        s_c = _dot_nt(q_hi, ck_ref[0, :, gl].astype(BF16))
        cend = lax.broadcasted_iota(jnp.int32, (1, ncb), 1) * CMP_STRIDE + (CMP_LEN - 1)
        p_c = _softmax_rows(s_c, cend <= qpos_r)
        o_c = _dot(p_c.astype(BF16), cv_ref[0, :, gl].astype(BF16))
        p_sum = p_c[0:tq]
        for r in range(1, Q_PER_KV):
            p_sum = p_sum + p_c[r * tq:(r + 1) * tq]
        ps_hi, ps_lo = _split(p_sum)
        msel_t = msel_ref[...]
        imp_t = _dot_nt(msel_t, ps_hi) + _dot_nt(msel_t, ps_lo)
        sel = _select_blocks(imp_t, qpos_l, imp_t.shape[0], axis=0).T
        unpicked = ((sel - 1.0) * -NEG).astype(BF16)

        wl = WINDOW + tq
        w0 = pl.multiple_of(jnp.maximum(s0 - WINDOW, 0), tq)
        kpos = w0 + lax.broadcasted_iota(jnp.int32, (1, wl), 1)
        dlt = qpos_t - kpos
        bias_w = jnp.where((dlt >= 0) & (dlt < WINDOW), 0.0, NEG)
        k_w = kw_ref[0, pl.ds(w0, wl), gl].astype(BF16)
        v_w = with_ones(vw_ref[0, pl.ds(w0, wl), :], g)
        o_w = []
        for r0 in range(0, R, wb):
            s_w = _dot_nt(q_hi[r0:r0 + wb], k_w) + rows_of(bias_w, r0, wb)
            p_w = jnp.exp(s_w - jnp.max(s_w, axis=-1, keepdims=True))
            ov = _dot(p_w.astype(BF16), v_w)
            o_w.append(ov[:, gl] / jnp.maximum(ov[:, ones_lane(g)], TINY))
        pre.append((q_hi, unpicked, o_c, jnp.concatenate(o_w, axis=0)))

    heads = []
    for g in range(N_KV):
        gl = group_lanes(g)
        q_hi, unpicked, o_c, o_w = pre[g]
        m_sc[...] = jnp.full((R, LANE), NEG, F32)

        def sweep_max(c, carry):
            off = pl.multiple_of(c * kc, kc)
            k = ks_ref[0, pl.ds(off, kc), gl].astype(BF16)
            kpos = off + lax.broadcasted_iota(jnp.int32, (1, kc), 1)
            bias = _dot(unpicked, exp_ref[c]) + jnp.where(kpos <= qpos_t, 0.0, NEG)
            s = _dot_nt(q_hi, k) + rows_of(bias, 0, R)
            s_sc[c] = s
            mx = s[:, 0:LANE]
            for t in range(1, kc // LANE):
                mx = jnp.maximum(mx, s[:, t * LANE:(t + 1) * LANE])
            m_sc[...] = jnp.maximum(m_sc[...], mx)
            return carry

        lax.fori_loop(0, n_chunks, sweep_max, 0)
        m_sc[...] = jnp.broadcast_to(jnp.max(m_sc[...], axis=-1, keepdims=True), (R, LANE))
        acc_sc[...] = jnp.zeros((R, KV_WIDTH), F32)

        def sweep_pv(c, carry):
            off = pl.multiple_of(c * kc, kc)
            p = jnp.exp(s_sc[c] - jnp.concatenate([m_sc[...]] * (kc // LANE), axis=1))
            acc_sc[...] += _dot(p.astype(BF16), with_ones(vs_ref[0, pl.ds(off, kc), :], g))
            return carry

        lax.fori_loop(0, n_chunks, sweep_pv, 0)
        o_s = acc_sc[:, gl] / jnp.maximum(acc_sc[:, ones_lane(g)], TINY)

        gate = lambda i: stack(lambda r: sig[:, (g * Q_PER_KV + r) * 3 + i:(g * Q_PER_KV + r) * 3 + i + 1])
        og = gate(0) * o_c + gate(1) * o_s + gate(2) * o_w
        heads += [og[r * tq:(r + 1) * tq] for r in range(Q_PER_KV)]
    o_ref[0] = jnp.concatenate(heads, axis=1)


def _attn_prompt(q, gt, ck, cv, ks, vs, kw, vw, tq=128, kc=512, wb=256):
    B, T, _ = q.shape
    assert T % kc == 0 and T % tq == 0 and T >= WINDOW + tq and (tq % wb == 0 or wb % tq == 0)
    ncb = ck.shape[1]
    ns = T // SEL_BLOCK
    i = jnp.arange(ncb)[None, :]
    j = jnp.arange(ns)[:, None]
    msel = ((i * CMP_STRIDE <= j * SEL_BLOCK + SEL_BLOCK - 1)
            & (i * CMP_STRIDE + CMP_LEN - 1 >= j * SEL_BLOCK)).astype(BF16)
    key_blk = (jnp.arange(T) // SEL_BLOCK).reshape(T // kc, 1, kc)
    expand = (key_blk == jnp.arange(ns)[None, :, None]).astype(BF16)
    tok = lambda w: pl.BlockSpec((1, tq, w), lambda b, t: (b, t, 0))
    seq = lambda a: pl.BlockSpec((1,) + a.shape[1:], lambda b, t: (b, 0, 0))
    const = lambda a: pl.BlockSpec(a.shape, lambda b, t: (0,) * a.ndim)
    R = Q_PER_KV * tq
    return pl.pallas_call(
        functools.partial(_attn_prompt_kernel, tq=tq, kc=kc, wb=wb),
        grid=(B, T // tq),
        in_specs=[tok(ATT_WIDTH), tok(LANE), seq(ck), seq(cv), seq(ks), seq(vs), seq(kw), seq(vw),
                  const(msel), const(expand)],
        out_specs=tok(ATT_WIDTH),
        out_shape=jax.ShapeDtypeStruct((B, T, ATT_WIDTH), F32),
        scratch_shapes=[pltpu.VMEM((R, LANE), F32), pltpu.VMEM((R, KV_WIDTH), F32), pltpu.VMEM((T // kc, R, kc), F32)],
        compiler_params=_params(("parallel", "arbitrary")),
        name="attn_prompt",
    )(q, gt, ck, cv, ks, vs, kw, vw, msel, expand)


def _group_lanes():
    row = lax.broadcasted_iota(jnp.int32, (N_HEADS, LANE), 0) // Q_PER_KV
    lane = lax.broadcasted_iota(jnp.int32, (N_HEADS, LANE), 1) // HEAD_DIM
    return row == lane


def _attn_sample_cmp_kernel(q_ref, ck_ref, cv_ref, msel_ref, oc_ref, sel_ref, *, qpos, n_blocks):
    q_hi, q_lo = _split(q_ref[0] * (HEAD_DIM ** -0.5))
    k_hi, k_lo = _split(ck_ref[0])
    s_c = _dot3_nt(q_hi, q_lo, k_hi, k_lo)
    ncb = s_c.shape[1]
    cend = lax.broadcasted_iota(jnp.int32, (1, ncb), 1) * CMP_STRIDE + (CMP_LEN - 1)
    p_c = _softmax_rows(s_c, cend <= qpos)
    o_c = _dot(p_c.astype(BF16), cv_ref[0].astype(BF16))
    oc_ref[0] = jnp.where(_group_lanes(), o_c, 0.0)
    top = jnp.sum(p_c[0:Q_PER_KV], axis=0, keepdims=True)
    bot = jnp.sum(p_c[Q_PER_KV:], axis=0, keepdims=True)
    row = lax.broadcasted_iota(jnp.int32, (N_HEADS, 1), 0)
    p_sum = jnp.where(row < Q_PER_KV, top, bot)
    ps_hi, ps_lo = _split(p_sum)
    msel = msel_ref[...]
    imp = _dot(ps_hi, msel) + _dot(ps_lo, msel)
    sel_ref[0] = _select_blocks(imp, jnp.full((N_HEADS, 1), qpos, jnp.int32), n_blocks)


def _attn_sample_cmp(qbd, ck, cv, qpos, n_blocks, ns_pad):
    DB, ncb, _ = ck.shape
    i = jnp.arange(ncb)[:, None]
    j = jnp.arange(ns_pad)[None, :]
    msel = ((i * CMP_STRIDE <= j * SEL_BLOCK + SEL_BLOCK - 1) & (i * CMP_STRIDE + CMP_LEN - 1 >= j * SEL_BLOCK)
            & (j < n_blocks)).astype(BF16)
    row = lambda a: pl.BlockSpec((1,) + a.shape[1:], lambda b: (b, 0, 0))
    return pl.pallas_call(
        functools.partial(_attn_sample_cmp_kernel, qpos=qpos, n_blocks=n_blocks),
        grid=(DB,),
        in_specs=[row(qbd), row(ck), row(cv), pl.BlockSpec(msel.shape, lambda b: (0, 0))],
        out_specs=[pl.BlockSpec((1, N_HEADS, LANE), lambda b: (b, 0, 0)),
                   pl.BlockSpec((1, N_HEADS, ns_pad), lambda b: (b, 0, 0))],
        out_shape=[jax.ShapeDtypeStruct((DB, N_HEADS, LANE), F32), jax.ShapeDtypeStruct((DB, N_HEADS, ns_pad), F32)],
        compiler_params=_params(("arbitrary",)),
        name="attn_sample_cmp",
    )(qbd, ck, cv, msel)


def _attn_sample_kernel(pt_ref, q_ref, sel_ref, oc_ref, gt_ref, ksn_ref, vsn_ref, kwb_ref, vwb_ref, kwn_ref, vwn_ref,
                        kpool_ref, vpool_ref, o_ref, kbuf, vbuf, ksem, vsem, *, n_pages, qpos):
    slot = _paged_fetch(pt_ref, (kpool_ref, vpool_ref), (kbuf, vbuf), (ksem, vsem), n_pages,
                        lambda buf, sl, p: buf.at[sl, :, pl.ds(p * PAGE_SIZE, PAGE_SIZE)])
    past = n_pages * PAGE_SIZE
    q = q_ref[0] * (HEAD_DIM ** -0.5)
    q_b = q.astype(BF16)
    sel = sel_ref[0]

    def attend(kt_old, vt_old, mask_old, k_new, v_new, mask_new):
        s = jnp.where(mask_old, _dot(q_b, kt_old.astype(BF16)), NEG)
        s_n = jnp.where(mask_new, jnp.sum(q * k_new, axis=-1, keepdims=True), NEG)
        m = jnp.maximum(jnp.max(s, axis=-1, keepdims=True), s_n)
        e = jnp.where(mask_old, jnp.exp(s - m), 0.0)
        e_n = jnp.where(mask_new, jnp.exp(s_n - m), 0.0)
        den = jnp.maximum(jnp.sum(e, axis=-1, keepdims=True) + e_n, TINY)
        return (_dot_nt(e.astype(BF16), vt_old.astype(BF16)) + e_n * v_new) / den

    assert LANE == 2 * SEL_BLOCK
    first = lax.broadcasted_iota(jnp.int32, (1, LANE), 1) < SEL_BLOCK
    picked = jnp.concatenate([jnp.where(first, sel[:, 2 * t:2 * t + 1], sel[:, 2 * t + 1:2 * t + 2])
                              for t in range(past // LANE)], axis=1)
    kpos = lax.broadcasted_iota(jnp.int32, (1, past), 1)
    nb_new = past // SEL_BLOCK
    o_s = attend(kbuf[slot], vbuf[slot], (picked > 0.5) & (kpos <= qpos), ksn_ref[0], vsn_ref[0],
                 (sel[:, nb_new:nb_new + 1] > 0.5) & (past <= qpos))
    wbuf = kwb_ref.shape[2]
    dlt = qpos - (past - wbuf + lax.broadcasted_iota(jnp.int32, (1, wbuf), 1))
    o_w = attend(kwb_ref[0], vwb_ref[0], (dlt >= 0) & (dlt < WINDOW), kwn_ref[0], vwn_ref[0],
                 jnp.full((N_HEADS, 1), (qpos - past >= 0) & (qpos - past < WINDOW)))
    sig = jax.nn.sigmoid(gt_ref[0])
    o = sig[:, 0:1] * oc_ref[0] + sig[:, 1:2] * o_s + sig[:, 2:3] * o_w
    o = jnp.where(_group_lanes(), o, 0.0)
    o_ref[0] = o[:, :HEAD_DIM] + o[:, HEAD_DIM:]


def _attn_sample(page_table, qbd, sel, oc, gt3, ksn, vsn, kwb, vwb, kwn, vwn, kpool, vpool, qpos):
    DB, n_pages = page_table.shape
    past = n_pages * PAGE_SIZE
    row = lambda a: pl.BlockSpec((1,) + a.shape[1:], lambda b, pt: (b, 0, 0))
    hbm = pl.BlockSpec(memory_space=pl.ANY)
    ins = [qbd, sel, oc, gt3, ksn, vsn, kwb, vwb, kwn, vwn]
    return pl.pallas_call(
        functools.partial(_attn_sample_kernel, n_pages=n_pages, qpos=qpos),
        grid_spec=pltpu.PrefetchScalarGridSpec(
            num_scalar_prefetch=1, grid=(DB,),
            in_specs=[row(a) for a in ins] + [hbm, hbm],
            out_specs=pl.BlockSpec((1, N_HEADS, HEAD_DIM), lambda b, pt: (b, 0, 0)),
            scratch_shapes=[pltpu.VMEM((2, KV_WIDTH, past), F32), pltpu.VMEM((2, KV_WIDTH, past), F32),
                            pltpu.SemaphoreType.DMA((2,)), pltpu.SemaphoreType.DMA((2,))]),
        out_shape=jax.ShapeDtypeStruct((DB, N_HEADS, HEAD_DIM), F32),
        compiler_params=_params(("arbitrary",)),
        name="attn_sample",
    )(page_table, *ins, kpool, vpool)


_SSM_HALF = SSM_WIDTH // 2
_SSM_ROW = SSM_GROUPS // 2 * SSM_STATE


def _ssm_params(a_re, a_im, log_dt, b_re, b_im, c_re, c_im, d):
    lam = lax.complex(a_re, a_im)
    step = jnp.exp(log_dt)[:, None]
    a_bar = jnp.exp(lam * step)
    b_bar = ((a_bar - 1.0) / lam)[..., None] * lax.complex(b_re, b_im)
    eye = jnp.eye(SSM_GROUPS // 2, dtype=F32)

    def b_mat(x):
        x = x.reshape(2, SSM_GROUPS // 2, SSM_STATE, SSM_GROUP)
        return jnp.einsum('hgpc,gk->hgckp', x, eye).reshape(2, _SSM_HALF, _SSM_ROW)

    def c_mat(x):
        x = x.reshape(2, SSM_GROUPS // 2, SSM_GROUP, SSM_STATE)
        return jnp.einsum('hgcp,gk->hgpkc', x, eye).reshape(2, _SSM_ROW, _SSM_HALF)

    bm = jnp.concatenate([b_mat(jnp.real(b_bar)), b_mat(jnp.imag(b_bar))], axis=2)
    cm = jnp.concatenate([c_mat(c_re), -c_mat(c_im)], axis=1)
    ar = jnp.real(a_bar).reshape(2, _SSM_ROW)
    ai = jnp.imag(a_bar).reshape(2, _SSM_ROW)
    bm_hi, bm_lo = _split(bm)
    cm_hi, cm_lo = _split(cm)
    return ar, ai, bm_hi, bm_lo, cm_hi, cm_lo, d.reshape(1, SSM_WIDTH)


def _ssm_prompt_kernel(u_ref, ar_ref, ai_ref, bh_ref, ch_ref, d_ref, y_ref, hr_ref, hi_ref,
                       sr_sc, si_sc, st_re, st_im, *, tc, nb):
    t = pl.program_id(0)
    rows = 2 * nb
    nlb = _SSM_ROW // LANE

    @pl.when(t == 0)
    def _():
        st_re[...] = jnp.zeros_like(st_re)
        st_im[...] = jnp.zeros_like(st_im)

    for b in range(nb):
        for hf in range(2):
            u_hi, u_lo = _split(u_ref[b, :, hf * _SSM_HALF:(hf + 1) * _SSM_HALF])
            bu = _dot(u_hi, bh_ref[hf]) + _dot(u_lo, bh_ref[hf])
            for k in range(nlb):
                sr_sc[k, pl.ds(b * 2 + hf, tc, stride=rows), :] = bu[:, k * LANE:(k + 1) * LANE]
                si_sc[k, pl.ds(b * 2 + hf, tc, stride=rows), :] = bu[:, _SSM_ROW + k * LANE:_SSM_ROW + (k + 1) * LANE]

    ar = ar_ref[...]
    ai = ai_ref[...]

    def step(i, carry):
        h_re, h_im = carry
        r0 = pl.multiple_of(i * rows, rows)
        n_re = ar * h_re - ai * h_im + sr_sc[:, pl.ds(r0, rows), :]
        n_im = ar * h_im + ai * h_re + si_sc[:, pl.ds(r0, rows), :]
        sr_sc[:, pl.ds(r0, rows), :] = n_re
        si_sc[:, pl.ds(r0, rows), :] = n_im
        return n_re, n_im

    h_re, h_im = lax.fori_loop(0, tc, step, (st_re[...], st_im[...]), unroll=4)
    st_re[...] = h_re
    st_im[...] = h_im
    hr_ref[...] = h_re
    hi_ref[...] = h_im

    for b in range(nb):
        for hf in range(2):
            gather = lambda sc: jnp.concatenate(
                [sc[k, pl.ds(b * 2 + hf, tc, stride=rows), :] for k in range(nlb)], axis=1)
            hs = jnp.concatenate([gather(sr_sc).astype(BF16), gather(si_sc).astype(BF16)], axis=1)
            cols = slice(hf * _SSM_HALF, (hf + 1) * _SSM_HALF)
            y_ref[b, :, cols] = _dot(hs, ch_ref[hf]) + d_ref[:, cols] * u_ref[b, :, cols]


def _ssm_prompt(u, sp, tc=128):
    B, T, W = u.shape
    ar, ai, bh, bl, ch, cl, d = sp
    rows = 2 * B
    nlb = _SSM_ROW // LANE
    tiles = lambda a: jnp.tile(a, (B, 1)).reshape(rows, nlb, LANE).transpose(1, 0, 2)
    const = lambda a: pl.BlockSpec(a.shape, lambda t: (0,) * a.ndim)
    blk = pl.BlockSpec((B, tc, W), lambda t: (0, t, 0))
    st = pl.BlockSpec((nlb, rows, LANE), lambda t: (0, 0, 0))
    ins = [tiles(ar), tiles(ai), bh, ch, d]
    st_shape = jax.ShapeDtypeStruct((nlb, rows, LANE), F32)
    y, h_re, h_im = pl.pallas_call(
        functools.partial(_ssm_prompt_kernel, tc=tc, nb=B),
        grid=(T // tc,),
        in_specs=[blk] + [const(a) for a in ins],
        out_specs=[blk, st, st],
        out_shape=[jax.ShapeDtypeStruct((B, T, W), F32), st_shape, st_shape],
        scratch_shapes=[pltpu.VMEM((nlb, tc * rows, LANE), F32), pltpu.VMEM((nlb, tc * rows, LANE), F32),
                        pltpu.VMEM((nlb, rows, LANE), F32), pltpu.VMEM((nlb, rows, LANE), F32)],
        compiler_params=_params(("arbitrary",)),
        name="ssm_prompt",
    )(u, *ins)
    rows_major = lambda a: a.transpose(1, 0, 2).reshape(rows, _SSM_ROW)
    return y, rows_major(h_re), rows_major(h_im)


def _ssm_sample_kernel(u_ref, h0r_ref, h0i_ref, ar_ref, ai_ref, bh_ref, bl_ref, ch_ref, cl_ref, d_ref,
                       y_ref, hr_ref, hi_ref):
    for hf in range(2):
        cols = slice(hf * _SSM_HALF, (hf + 1) * _SSM_HALF)
        lanes = slice(hf * _SSM_ROW, (hf + 1) * _SSM_ROW)
        u = u_ref[:, cols]
        u_hi, u_lo = _split(u)
        bu = _dot3(u_hi, u_lo, bh_ref[hf], bl_ref[hf])
        ar = ar_ref[hf:hf + 1, :]
        ai = ai_ref[hf:hf + 1, :]
        h_re = ar * h0r_ref[:, lanes] - ai * h0i_ref[:, lanes] + bu[:, :_SSM_ROW]
        h_im = ar * h0i_ref[:, lanes] + ai * h0r_ref[:, lanes] + bu[:, _SSM_ROW:]
        hr_ref[:, lanes] = h_re
        hi_ref[:, lanes] = h_im
        r_hi, r_lo = _split(h_re)
        i_hi, i_lo = _split(h_im)
        y = (_dot3(r_hi, r_lo, ch_ref[hf, :_SSM_ROW], cl_ref[hf, :_SSM_ROW])
             + _dot3(i_hi, i_lo, ch_ref[hf, _SSM_ROW:], cl_ref[hf, _SSM_ROW:]))
        y_ref[:, cols] = y + d_ref[:, cols] * u


def _ssm_sample(u, h0r, h0i, sp):
    n = u.shape[0]
    ins = [u, h0r, h0i, *sp]
    full = lambda a: pl.BlockSpec(a.shape, lambda i: (0,) * a.ndim)
    outs = [jax.ShapeDtypeStruct((n, SSM_WIDTH), F32), jax.ShapeDtypeStruct(h0r.shape, F32),
            jax.ShapeDtypeStruct(h0r.shape, F32)]
    return pl.pallas_call(
        _ssm_sample_kernel, grid=(1,),
        in_specs=[full(a) for a in ins], out_specs=[full(a) for a in outs], out_shape=outs,
        compiler_params=_params(("arbitrary",)),
        name="ssm_sample",
    )(*ins)


def _merge_kernel(x_ref, o_ref, y_ref, gm_ref, wa_ref, ws_ref, wo_ref, gf_ref, y2_ref, hf_ref):
    a = _dot(o_ref[...].astype(BF16), wa_ref[...])
    gl = _dot(jax.nn.gelu(y_ref[...]).astype(BF16), ws_ref[...])
    s = gl[:, :D_MODEL] * jax.nn.sigmoid(gl[:, D_MODEL:])
    gm = gm_ref[...]
    m = jax.nn.sigmoid(gm[:, :D_MODEL]) * a + jax.nn.sigmoid(gm[:, D_MODEL:]) * s
    y2 = x_ref[...] + _dot(m.astype(BF16), wo_ref[...])
    y2_ref[...] = y2
    hf_ref[...] = _rms(y2, gf_ref[...])


def _merge(x, o, y, gm, wa, ws, wo, gf, tm):
    n = x.shape[0]
    tok = lambda w: pl.BlockSpec((tm, w), lambda i: (i, 0))
    const = lambda a: pl.BlockSpec(a.shape, lambda i: (0,) * a.ndim)
    return pl.pallas_call(
        _merge_kernel, grid=(n // tm,),
        in_specs=[tok(D_MODEL), tok(ATT_WIDTH), tok(SSM_WIDTH), tok(2 * D_MODEL), const(wa), const(ws), const(wo),
                  const(gf)],
        out_specs=[tok(D_MODEL), tok(D_MODEL)],
        out_shape=[jax.ShapeDtypeStruct((n, D_MODEL), F32)] * 2,
        compiler_params=_params(("parallel",)),
        name="merge",
    )(x, o, y, gm, wa, ws, wo, gf)


def _cand_pairs():
    return [(a, b) for a in range(PEER_TOPK) for b in range(PEER_TOPK) if (a + 1) * (b + 1) <= PEER_TOPK]


def _top_values(s, k):
    vals = []
    for _ in range(k):
        mx = jnp.max(s, axis=0, keepdims=True)
        vals.append(mx)
        s = jnp.where(s == mx, LOWEST, s)
    return vals


def _peer_route_kernel(hf_ref, wqh_ref, wql_ref, k1h_ref, k1l_ref, k2h_ref, k2l_ref,
                       c1_ref, e1_ref, s2_ref, e2_ref, hh_sc, hl_sc):
    half = PEER_QDIM // 2

    @pl.when(pl.program_id(1) == 0)
    def _():
        hh_sc[...], hl_sc[...] = _split(hf_ref[...])

    q = _dot3(hh_sc[...], hl_sc[...], wqh_ref[...], wql_ref[...])
    q1h, q1l = _split(q[:, :half])
    q2h, q2l = _split(q[:, half:])
    s1 = _dot3_nt(k1h_ref[...], k1l_ref[...], q1h, q1l)
    s2 = _dot3_nt(k2h_ref[...], k2l_ref[...], q2h, q2l)
    v1 = _top_values(s1, PEER_TOPK)
    v2 = _top_values(s2, PEER_TOPK)
    cand = jnp.concatenate([v1[a] + v2[b] for a, b in _cand_pairs()], axis=0)
    thr = _top_values(cand, PEER_TOPK)[-1]
    top = v1[0] + v2[0]
    z = jnp.sum(jnp.where(cand >= thr, jnp.exp(cand - top), 0.0), axis=0, keepdims=True)
    c1 = jnp.full(s1.shape, -LOWEST, F32)
    for v in v2:
        c1 = jnp.where(s1 + v >= thr, v, c1)
    c1_ref[0] = c1
    s2_ref[0] = s2
    e1_ref[0] = jnp.exp(s1 - v1[0])
    e2_ref[0] = jnp.exp(s2 - v2[0]) / z


def _peer_route(hf, wq_hi, wq_lo, k1, k2, tn):
    n = hf.shape[0]
    k1h, k1l = _split(k1)
    k2h, k2l = _split(k2)
    const = lambda a: pl.BlockSpec(a.shape, lambda i, h: (0, 0))
    wq = pl.BlockSpec((D_MODEL, PEER_QDIM), lambda i, h: (0, h))
    keyed = pl.BlockSpec((1, PEER_NKEYS, tn), lambda i, h: (h, 0, i))
    shp = jax.ShapeDtypeStruct((PEER_HEADS, PEER_NKEYS, n), F32)
    return pl.pallas_call(
        _peer_route_kernel, grid=(n // tn, PEER_HEADS),
        in_specs=[pl.BlockSpec((tn, D_MODEL), lambda i, h: (i, 0)), wq, wq, const(k1h), const(k1l), const(k2h),
                  const(k2l)],
        out_specs=[keyed, keyed, keyed, keyed],
        out_shape=[shp, shp, shp, shp],
        scratch_shapes=[pltpu.VMEM((tn, D_MODEL), BF16), pltpu.VMEM((tn, D_MODEL), BF16)],
        compiler_params=_params(("parallel", "arbitrary")),
        name="peer_route",
    )(hf, wq_hi, wq_lo, k1h, k1l, k2h, k2l)


_PEER_JROWS = 32
_PEER_SLABS = 4


def _gelu_tanh(x):
    k = math.sqrt(2.0 / math.pi)
    hx = 0.5 * x
    return hx + hx * jnp.tanh(x * (k + (k * 0.044715) * (x * x)))


def _peer_main_kernel(hf_ref, y2_ref, u_ref, vt_ref, c1_ref, e1_ref, s2_ref, e2_ref, gn_ref, out_ref,
                      acc_sc, w_sc, act_sc, hfb_sc, *, tn, n_slab):
    e = pl.program_id(1)

    @pl.when(e == 0)
    def _():
        acc_sc[...] = jnp.zeros_like(acc_sc)
        hfb_sc[...] = hf_ref[...].astype(BF16)

    act_sc[...] = _gelu_tanh(_dot_nt(u_ref[...], hfb_sc[...]))

    for lc in range(tn // LANE):
        ln = slice(lc * LANE, (lc + 1) * LANE)

        def rows(jq, carry, ln=ln):
            j0 = pl.multiple_of(jq * _PEER_JROWS, _PEER_JROWS)
            for i0 in range(0, n_slab, _PEER_SLABS):
                slabs = range(i0, i0 + _PEER_SLABS)
                w = {ii: jnp.zeros((_PEER_JROWS, LANE), F32) for ii in slabs}
                for h in range(PEER_HEADS):
                    s2 = s2_ref[h, pl.ds(j0, _PEER_JROWS), ln]
                    e2 = e2_ref[h, pl.ds(j0, _PEER_JROWS), ln]
                    for ii in slabs:
                        w[ii] = w[ii] + e1_ref[h, ii:ii + 1, ln] * jnp.where(s2 >= c1_ref[h, ii:ii + 1, ln], e2, 0.0)
                for ii in slabs:
                    rs = pl.ds(ii * PEER_NKEYS + j0, _PEER_JROWS)
                    w_sc[rs, ln] = (w[ii] * act_sc[rs, ln]).astype(BF16)
            return carry

        lax.fori_loop(0, PEER_NKEYS // _PEER_JROWS, rows, 0)

    acc_sc[...] += _dot(vt_ref[...], w_sc[...])

    @pl.when(e == pl.num_programs(1) - 1)
    def _():
        out_ref[...] = _rms(y2_ref[...] + acc_sc[...].T, gn_ref[...])


def _peer_main(hf, y2, u_tab, vt_tab, c1, e1, s2, e2, gn, tn, n_slab=16):
    n = hf.shape[0]
    ec = n_slab * PEER_NKEYS
    n_exp = u_tab.shape[0]
    tok = pl.BlockSpec((tn, D_MODEL), lambda i, e: (i, 0))
    slab = pl.BlockSpec((PEER_HEADS, n_slab, tn), lambda i, e: (0, e, i))
    keyed = pl.BlockSpec((PEER_HEADS, PEER_NKEYS, tn), lambda i, e: (0, 0, i))
    return pl.pallas_call(
        functools.partial(_peer_main_kernel, tn=tn, n_slab=n_slab),
        grid=(n // tn, n_exp // ec),
        in_specs=[tok, tok, pl.BlockSpec((ec, D_MODEL), lambda i, e: (e, 0)),
                  pl.BlockSpec((D_MODEL, ec), lambda i, e: (0, e)), slab, slab, keyed, keyed,
                  pl.BlockSpec(gn.shape, lambda i, e: (0, 0))],
        out_specs=tok,
        out_shape=jax.ShapeDtypeStruct((n, D_MODEL), F32),
        scratch_shapes=[pltpu.VMEM((D_MODEL, tn), F32), pltpu.VMEM((ec, tn), BF16), pltpu.VMEM((ec, tn), F32),
                        pltpu.VMEM((tn, D_MODEL), BF16)],
        compiler_params=_params(("parallel", "arbitrary")),
        name="peer_main",
    )(hf, y2, u_tab, vt_tab, c1, e1, s2, e2, gn)


def _peer(hf, y2, wq_hi, wq_lo, k1, k2, u_tab, vt_tab, gn, tn):
    c1, e1, s2, e2 = _peer_route(hf, wq_hi, wq_lo, k1, k2, tn)
    return _peer_main(hf, y2, u_tab, vt_tab, c1, e1, s2, e2, gn, tn)


def _rope_tables(pos):
    half = HEAD_DIM // 2
    inv = ROPE_THETA ** (-jnp.arange(half, dtype=F32) / half)
    ang = pos.astype(F32)[:, None] * inv[None, :]
    cos, sin = jnp.cos(ang), jnp.sin(ang)
    return jnp.tile(jnp.concatenate([cos, cos], -1), (1, 2)), jnp.tile(jnp.concatenate([-sin, sin], -1), (1, 2))


def _proj_weight(w):
    gates = jnp.pad(w[:, 1280:1304], ((0, 0), (0, LANE - 3 * N_HEADS)))
    return jnp.concatenate([w[:, :1280], gates, w[:, 1304:1816], w[:, 1816:]], axis=1)


def _compress_weights(pe, w1, w2):
    r = CMP_LEN // CMP_STRIDE
    eye = jnp.eye(N_KV, dtype=F32)
    w1r = w1.reshape(r, CMP_STRIDE, HEAD_DIM, w1.shape[-1])
    wc = jnp.einsum('jsdh,gk->sgdjkh', w1r, eye).reshape(CMP_STRIDE * KV_WIDTH, r * N_KV * w1.shape[-1])
    pe2 = jnp.broadcast_to(pe.reshape(r, CMP_STRIDE, 1, HEAD_DIM), (r, CMP_STRIDE, N_KV, HEAD_DIM))
    pe2 = jnp.pad(pe2.reshape(r, CMP_STRIDE * KV_WIDTH), ((0, 8 - r), (0, 0)))
    w2bd = jnp.einsum('hd,gk->ghkd', w2, eye).reshape(N_KV * w2.shape[0], KV_WIDTH)
    return wc.astype(BF16), pe2, w2bd.astype(BF16)


def kernel(x_prompt, x_sample, cache_k_cmp, cache_v_cmp, cache_k_sel, cache_v_sel, cache_k_win, cache_v_win,
           state_ssm_re, state_ssm_im, page_table, norm_mix, w_in, cmp_pe_k, cmp_w1_k, cmp_w2_k, cmp_pe_v, cmp_w1_v,
           cmp_w2_v, ssm_a_re, ssm_a_im, ssm_log_dt, ssm_b_re, ssm_b_im, ssm_c_re, ssm_c_im, ssm_d, w_att_proj,
           w_ssm_glu, w_out, norm_ffn, peer_w_q, peer_keys1, peer_keys2, peer_u, peer_v, norm_final):
    assert w_in.shape[0] == 1, "single layer"
    B, T, _ = x_prompt.shape
    DB = x_sample.shape[0]
    n_pages = page_table.shape[1]
    past = n_pages * PAGE_SIZE
    n_pool = cache_k_cmp.shape[1]

    w_hi, w_lo = _split(_proj_weight(w_in[0]))
    g_mix = norm_mix[0].reshape(1, D_MODEL)
    cw = _compress_weights(cmp_pe_k[0], cmp_w1_k[0], cmp_w2_k[0]) + _compress_weights(cmp_pe_v[0], cmp_w1_v[0],
                                                                                       cmp_w2_v[0])
    sp = _ssm_params(ssm_a_re[0], ssm_a_im[0], ssm_log_dt[0], ssm_b_re[0], ssm_b_im[0], ssm_c_re[0], ssm_c_im[0],
                     ssm_d[0])
    wa, ws, wo = w_att_proj[0].astype(BF16), w_ssm_glu[0].astype(BF16), w_out[0].astype(BF16)
    g_ffn = norm_ffn[0].reshape(1, D_MODEL)
    g_fin = norm_final.reshape(1, D_MODEL)
    wq_hi, wq_lo = _split(peer_w_q[0])
    u_tab = peer_u[0].astype(BF16)
    vt_tab = peer_v[0].T.astype(BF16)

    cos_p, sin_p = _rope_tables(jnp.arange(T, dtype=jnp.int32))
    q, kc, vc, ks, vs, kw, vw, gt, u, gm = _project(x_prompt, g_mix, cos_p, sin_p, [w_hi], 512, False)
    chunks = lambda a: a.reshape(B, T // CMP_STRIDE, CMP_STRIDE * KV_WIDTH)
    ck, cv = _compress_prompt(chunks(kc), chunks(vc), cw)
    o_att = _attn_prompt(q, gt, ck, cv, ks, vs, kw, vw)
    y_ssm, hp_re, hp_im = _ssm_prompt(u, sp)
    n_p = B * T
    flat = lambda a: a.reshape(n_p, a.shape[-1])
    y2_p, hf_p = _merge(flat(x_prompt), flat(o_att), flat(y_ssm), flat(gm), wa, ws, wo, g_ffn, 512)
    y_prompt = _peer(hf_p, y2_p, wq_hi, wq_lo, peer_keys1[0], peer_keys2[0], u_tab, vt_tab, g_fin, 512)

    cos_s, sin_s = _rope_tables(jnp.full((DB,), past, jnp.int32))
    xs = x_sample.reshape(1, DB, D_MODEL)
    qs, kcs, vcs, kss, vss, kws, vws, gts, us, gms = [a[0] for a in
                                                      _project(xs, g_mix, cos_s, sin_s, [w_hi, w_lo], DB, True)]
    row3 = lambda a: a.reshape(DB, 1, KV_WIDTH)
    native = lambda c: jnp.transpose(c[0], (0, 2, 3, 1)).reshape(c.shape[1], KV_WIDTH, c.shape[2])
    cks, cvs = _compress_sample(page_table, native(cache_k_cmp), native(cache_v_cmp), row3(kcs), row3(vcs), cw)
    q5 = qs.reshape(DB, N_KV, Q_PER_KV, 1, HEAD_DIM) * jnp.eye(N_KV, dtype=F32).reshape(1, N_KV, 1, N_KV, 1)
    qbd = q5.reshape(DB, N_HEADS, KV_WIDTH)
    n_blocks = -(-(past + 1) // SEL_BLOCK)
    ns_pad = -(-n_blocks // LANE) * LANE
    oc, sel = _attn_sample_cmp(qbd, cks, cvs, past, n_blocks, ns_pad)
    gt3 = gts[:, :3 * N_HEADS].reshape(DB, N_HEADS, 3)
    o_s = _attn_sample(page_table, qbd, sel, oc, gt3, row3(kss), row3(vss), native(cache_k_win), native(cache_v_win),
                       row3(kws), row3(vws), native(cache_k_sel), native(cache_v_sel), past)
    h0r = state_ssm_re[0].reshape(DB, SSM_GROUPS * SSM_STATE)
    h0i = state_ssm_im[0].reshape(DB, SSM_GROUPS * SSM_STATE)
    ys_ssm, hs_re, hs_im = _ssm_sample(us, h0r, h0i, sp)
    y2_s, hf_s = _merge(x_sample.reshape(DB, D_MODEL), o_s.reshape(DB, ATT_WIDTH), ys_ssm, gms, wa, ws, wo, g_ffn, DB)
    y_sample = _peer(hf_s, y2_s, wq_hi, wq_lo, peer_keys1[0], peer_keys2[0], u_tab, vt_tab, g_fin, DB)

    kv5 = lambda a, n: a.reshape(1, n, -1, N_KV, HEAD_DIM)
    wb = min(WINDOW, T)
    st = lambda a, n: a.reshape(1, n, SSM_GROUPS, SSM_STATE)
    wbuf = cache_k_win.shape[2]
    nw = min(WINDOW, wbuf + 1)
    win_s = lambda old, new: jnp.concatenate([old[0].reshape(DB, wbuf, KV_WIDTH), new.reshape(DB, 1, KV_WIDTH)],
                                             axis=1)[:, wbuf + 1 - nw:]
    return (y_prompt.reshape(B, T, D_MODEL), y_sample.reshape(DB, 1, D_MODEL),
            kv5(kc, B), kv5(vc, B), kv5(ks, B), kv5(vs, B), kv5(kw[:, T - wb:], B), kv5(vw[:, T - wb:], B),
            st(hp_re, B), st(hp_im, B),
            kv5(kcs, DB), kv5(vcs, DB), kv5(kss, DB), kv5(vss, DB),
            kv5(win_s(cache_k_win, kws), DB), kv5(win_s(cache_v_win, vws), DB),
            st(hs_re, DB), st(hs_im, DB))
```

```python
import functools
import math

import jax
import jax.numpy as jnp
from jax import lax
from jax.experimental import pallas as pl
from jax.experimental.pallas import tpu as pltpu

F32 = jnp.float32
BF16 = jnp.bfloat16

D_MODEL = 1024
HEAD_DIM = 64
N_HEADS = 8
N_KV = 2
Q_PER_KV = 4
ATT_WIDTH = 512
KV_WIDTH = 128
CMP_LEN = 32
CMP_STRIDE = 16
SEL_BLOCK = 64
SEL_TOPN = 16
WINDOW = 512
ROPE_THETA = 10000.0
PAGE_SIZE = 128
SSM_GROUP = 16
SSM_WIDTH = 512
SSM_GROUPS = 32
SSM_STATE = 64
PEER_HEADS = 8
PEER_NKEYS = 128
PEER_QDIM = 256
PEER_TOPK = 16
RMS_EPS = 1e-6
NEG = -1e30
BIG = 1e9
TINY = 1e-30
LOWEST = -3.0e38

LANE = 128
VMEM_LIMIT = 56 * 1024 * 1024

_NT = (((1,), (1,)), ((), ()))


def _params(sem, vmem=VMEM_LIMIT):
    return pltpu.CompilerParams(dimension_semantics=sem, vmem_limit_bytes=vmem)


def _split(x):
    hi = x.astype(BF16)
    lo = (x - hi.astype(F32)).astype(BF16)
    return hi, lo


def _dot(a, b):
    return jnp.dot(a, b, preferred_element_type=F32)


def _dot_nt(a, b):
    return lax.dot_general(a, b, _NT, preferred_element_type=F32)


def _dot3(a_hi, a_lo, b_hi, b_lo):
    return _dot(a_hi, b_hi) + (_dot(a_hi, b_lo) + _dot(a_lo, b_hi))


def _dot3_nt(a_hi, a_lo, b_hi, b_lo):
    return _dot_nt(a_hi, b_hi) + (_dot_nt(a_hi, b_lo) + _dot_nt(a_lo, b_hi))


def _rms(x, g):
    return x * lax.rsqrt(jnp.mean(x * x, axis=-1, keepdims=True) + RMS_EPS) * g


def _softmax_rows(s, mask):
    s = jnp.where(mask, s, NEG)
    e = jnp.where(mask, jnp.exp(s - jnp.max(s, axis=-1, keepdims=True)), 0.0)
    return e / jnp.maximum(jnp.sum(e, axis=-1, keepdims=True), TINY)


_SEG = {'q': (0, 512), 'kc': (512, 640), 'vc': (640, 768), 'ks': (768, 896), 'vs': (896, 1024),
        'kw': (1024, 1152), 'vw': (1152, 1280), 'gt': (1280, 1408), 'u': (1408, 1920), 'gm': (1920, 3968)}
_PROJ_COLS = 3968


def _proj_kernel(*refs, precise):
    x_ref, g_ref, cos_ref, sin_ref = refs[:4]
    n_w = 2 if precise else 1
    w_refs = refs[4:4 + n_w]
    q_ref, kc_ref, vc_ref, ks_ref, vs_ref, kw_ref, vw_ref, gt_ref, u_ref, gm_ref = refs[4 + n_w:]
    h = _rms(x_ref[0], g_ref[...])
    h_hi = h.astype(BF16)
    h_lo = (h - h_hi.astype(F32)).astype(BF16) if precise else None

    def mm(c0, c1):
        z = _dot(h_hi, w_refs[0][:, c0:c1])
        if precise:
            z = z + (_dot(h_hi, w_refs[1][:, c0:c1]) + _dot(h_lo, w_refs[0][:, c0:c1]))
        return z

    cos = cos_ref[...]
    sin = sin_ref[...]
    first = (lax.broadcasted_iota(jnp.int32, (1, LANE), 1) % HEAD_DIM) < (HEAD_DIM // 2)

    def rope(z):
        rot = jnp.where(first, pltpu.roll(z, LANE - HEAD_DIM // 2, 1), pltpu.roll(z, HEAD_DIM // 2, 1))
        return z * cos + rot * sin

    for i in range(4):
        q_ref[0, :, i * LANE:(i + 1) * LANE] = rope(mm(i * LANE, (i + 1) * LANE))
    kc_ref[0] = rope(mm(*_SEG['kc']))
    vc_ref[0] = mm(*_SEG['vc'])
    ks_ref[0] = rope(mm(*_SEG['ks']))
    vs_ref[0] = mm(*_SEG['vs'])
    kw_ref[0] = rope(mm(*_SEG['kw']))
    vw_ref[0] = mm(*_SEG['vw'])
    gt_ref[0] = mm(*_SEG['gt'])
    u_ref[0] = mm(*_SEG['u'])
    for i in range(4):
        c0 = _SEG['gm'][0] + i * 512
        gm_ref[0, :, i * 512:(i + 1) * 512] = mm(c0, c0 + 512)


def _project(x, gain, cos, sin, w_list, tm, precise):
    B, T, D = x.shape
    widths = [512, 128, 128, 128, 128, 128, 128, 128, 512, 2048]
    tok = lambda w: pl.BlockSpec((1, tm, w), lambda b, t: (b, t, 0))
    const = lambda a: pl.BlockSpec(a.shape, lambda b, t: (0,) * a.ndim)
    return pl.pallas_call(
        functools.partial(_proj_kernel, precise=precise),
        grid=(B, T // tm),
        in_specs=[tok(D), const(gain), pl.BlockSpec((tm, LANE), lambda b, t: (t, 0)),
                  pl.BlockSpec((tm, LANE), lambda b, t: (t, 0))] + [const(w) for w in w_list],
        out_specs=[tok(w) for w in widths],
        out_shape=[jax.ShapeDtypeStruct((B, T, w), F32) for w in widths],
        compiler_params=_params(("parallel", "arbitrary")),
        name="proj",
    )(x, gain, cos, sin, *w_list)


def _compress_rows(x, last_p1, wc_ref, pe_ref, w2_ref):
    C = x.shape[0]
    p = _dot(x.astype(BF16), wc_ref[...])
    pb = _dot(pe_ref[...].astype(BF16), wc_ref[...])
    bias = pb[0:1, :LANE] + pb[1:2, LANE:]
    p1 = pltpu.roll(p[:, LANE:], C - 1, 0)
    if last_p1 is not None:
        row = lax.broadcasted_iota(jnp.int32, (C, 1), 0)
        p1 = jnp.where(row == C - 1, last_p1, p1)
    hid = p[:, :LANE] + p1 + bias
    return _dot(jax.nn.gelu(hid).astype(BF16), w2_ref[...])


def _compress_prompt_kernel(xk_ref, xv_ref, wck_ref, pek_ref, w2k_ref, wcv_ref, pev_ref, w2v_ref, ck_ref, cv_ref):
    ck_ref[0] = _compress_rows(xk_ref[0], None, wck_ref, pek_ref, w2k_ref)
    cv_ref[0] = _compress_rows(xv_ref[0], None, wcv_ref, pev_ref, w2v_ref)


def _compress_prompt(xk, xv, cw):
    B, C, W = xk.shape
    seq = pl.BlockSpec((1, C, W), lambda b: (b, 0, 0))
    const = lambda a: pl.BlockSpec(a.shape, lambda b: (0,) * a.ndim)
    out = pl.BlockSpec((1, C, LANE), lambda b: (b, 0, 0))
    return pl.pallas_call(
        _compress_prompt_kernel,
        grid=(B,),
        in_specs=[seq, seq] + [const(a) for a in cw],
        out_specs=[out, out],
        out_shape=[jax.ShapeDtypeStruct((B, C, LANE), F32)] * 2,
        compiler_params=_params(("arbitrary",)),
        name="compress_prompt",
    )(xk, xv, *cw)


def _paged_fetch(pt_ref, pools, bufs, sems, n_pages, dst):
    b = pl.program_id(0)
    nb = pl.num_programs(0)
    slot = b % 2

    def copies(bb, sl):
        return [pltpu.make_async_copy(pool.at[pt_ref[bb, p]], dst(buf, sl, p), sem.at[sl])
                for pool, buf, sem in zip(pools, bufs, sems) for p in range(n_pages)]

    @pl.when(b == 0)
    def _():
        for cp in copies(b, slot):
            cp.start()

    @pl.when(b + 1 < nb)
    def _():
        for cp in copies(b + 1, 1 - slot):
            cp.start()

    for cp in copies(b, slot):
        cp.wait()
    return slot


def _compress_sample_kernel(pt_ref, kn_ref, vn_ref, perm_ref, wck_ref, pek_ref, w2k_ref, wcv_ref, pev_ref, w2v_ref,
                            kpool_ref, vpool_ref, ck_ref, cv_ref, kbuf, vbuf, ksem, vsem, taps_sc, *, n_pages):
    slot = _paged_fetch(pt_ref, (kpool_ref, vpool_ref), (kbuf, vbuf), (ksem, vsem), n_pages,
                        lambda buf, sl, p: buf.at[sl, p])
    cpp = PAGE_SIZE // CMP_STRIDE

    def one(buf, new_ref, wc_ref, pe_ref, w2_ref, out_ref):
        def flip(pp, carry):
            pair = buf[slot, pl.ds(pl.multiple_of(pp * 2, 2), 2)].reshape(2 * KV_WIDTH, PAGE_SIZE)
            rows = _dot_nt(perm_ref[...], pair.astype(BF16))
            c0 = pl.multiple_of(pp * 2 * cpp, 2 * cpp)
            for s in range(CMP_STRIDE):
                taps_sc[s, pl.ds(c0, 2 * cpp), :] = jnp.concatenate(
                    [rows[s * cpp:(s + 1) * cpp, :KV_WIDTH], rows[s * cpp:(s + 1) * cpp, KV_WIDTH:]], axis=0)
            return carry

        lax.fori_loop(0, n_pages // 2, flip, 0, unroll=16)
        x = jnp.concatenate([taps_sc[s] for s in range(CMP_STRIDE)], axis=1)
        new = jnp.broadcast_to(new_ref[0], (8, LANE)).astype(BF16)
        last_p1 = _dot(new, wc_ref[0:LANE, LANE:])[0:1]
        out_ref[0] = _compress_rows(x, last_p1, wc_ref, pe_ref, w2_ref)

    one(kbuf, kn_ref, wck_ref, pek_ref, w2k_ref, ck_ref)
    one(vbuf, vn_ref, wcv_ref, pev_ref, w2v_ref, cv_ref)


def _compress_sample(page_table, kpool, vpool, k_new, v_new, cw):
    DB, n_pages = page_table.shape
    C = n_pages * PAGE_SIZE // CMP_STRIDE
    new = pl.BlockSpec((1, 1, LANE), lambda b, pt: (b, 0, 0))
    const = lambda a: pl.BlockSpec(a.shape, lambda b, pt: (0,) * a.ndim)
    hbm = pl.BlockSpec(memory_space=pl.ANY)
    out = pl.BlockSpec((1, C, LANE), lambda b, pt: (b, 0, 0))
    page_buf = pltpu.VMEM((2, n_pages, KV_WIDTH, PAGE_SIZE), F32)
    cpp = PAGE_SIZE // CMP_STRIDE
    tok = jnp.arange(PAGE_SIZE)
    perm = ((tok[:, None] % cpp) * CMP_STRIDE + tok[:, None] // cpp == tok[None, :]).astype(BF16)
    return pl.pallas_call(
        functools.partial(_compress_sample_kernel, n_pages=n_pages),
        grid_spec=pltpu.PrefetchScalarGridSpec(
            num_scalar_prefetch=1, grid=(DB,),
            in_specs=[new, new, const(perm)] + [const(a) for a in cw] + [hbm, hbm],
            out_specs=[out, out],
            scratch_shapes=[page_buf, page_buf, pltpu.SemaphoreType.DMA((2,)), pltpu.SemaphoreType.DMA((2,)),
                            pltpu.VMEM((CMP_STRIDE, C, KV_WIDTH), F32)]),
        out_shape=[jax.ShapeDtypeStruct((DB, C, LANE), F32)] * 2,
        compiler_params=_params(("arbitrary",)),
        name="compress_sample",
    )(page_table, k_new, v_new, perm, *cw, kpool, vpool)


def _select_blocks(imp, qpos, n_real, axis=1):
    j_shape = (1, imp.shape[1]) if axis == 1 else (imp.shape[0], 1)
    j = lax.broadcasted_iota(jnp.int32, j_shape, axis)
    valid = j * SEL_BLOCK <= qpos
    cur = qpos // SEL_BLOCK
    force = (j == 0) | (j == cur) | (j == cur - 1)
    score = jnp.where(valid & force, BIG, jnp.where(valid, imp, -BIG))
    rank = jnp.zeros(imp.shape, F32)
    for jp in range(n_real):
        other = score[:, jp:jp + 1] if axis == 1 else score[jp:jp + 1, :]
        earlier = jnp.where(j > jp, 1.0, 0.0)
        rank = rank + jnp.where(other > score, 1.0, jnp.where(other == score, earlier, 0.0))
    return jnp.where(rank < SEL_TOPN, 1.0, 0.0)


def _attn_prompt_kernel(q_ref, gt_ref, ck_ref, cv_ref, ks_ref, vs_ref, kw_ref, vw_ref, msel_ref, exp_ref, o_ref,
                        m_sc, acc_sc, s_sc, *, tq, kc, wb):
    s0 = pl.program_id(1) * tq
    R = Q_PER_KV * tq
    ncb = ck_ref.shape[1]
    qpos_r = s0 + lax.broadcasted_iota(jnp.int32, (R, 1), 0) % tq
    qpos_t = s0 + lax.broadcasted_iota(jnp.int32, (tq, 1), 0)
    qpos_l = s0 + lax.broadcasted_iota(jnp.int32, (1, tq), 1)
    sig = jax.nn.sigmoid(gt_ref[0])
    stack = lambda f: jnp.concatenate([f(r) for r in range(Q_PER_KV)], axis=0)

    def rows_of(bias, r0, n):
        if n >= tq:
            return jnp.concatenate([bias] * (n // tq), axis=0)
        return bias[r0 % tq:r0 % tq + n]

    n_chunks = (s0 + tq + kc - 1) // kc
    group_lanes = lambda g: slice(g * HEAD_DIM, (g + 1) * HEAD_DIM)
    ones_lane = lambda g: slice((1 - g) * HEAD_DIM, (1 - g) * HEAD_DIM + 1)

    def with_ones(v, g):
        own = (lax.broadcasted_iota(jnp.int32, (1, KV_WIDTH), 1) // HEAD_DIM) == g
        return jnp.where(own, v, 1.0).astype(BF16)

    pre = []
    for g in range(N_KV):
        gl = group_lanes(g)
        qg = stack(lambda r: q_ref[0, :, (g * Q_PER_KV + r) * HEAD_DIM:(g * Q_PER_KV + r + 1) * HEAD_DIM])
        qg = qg * (HEAD_DIM ** -0.5)
        q_hi = qg.astype(BF16)

        s_c = _dot_nt(q_hi, ck_ref[0, :, gl].astype(BF16))
        cend = lax.broadcasted_iota(jnp.int32, (1, ncb), 1) * CMP_STRIDE + (CMP_LEN - 1)
        p_c = _softmax_rows(s_c, cend <= qpos_r)
        o_c = _dot(p_c.astype(BF16), cv_ref[0, :, gl].astype(BF16))
        p_sum = p_c[0:tq]
        for r in range(1, Q_PER_KV):
            p_sum = p_sum + p_c[r * tq:(r + 1) * tq]
        ps_hi, ps_lo = _split(p_sum)
        msel_t = msel_ref[...]
        imp_t = _dot_nt(msel_t, ps_hi) + _dot_nt(msel_t, ps_lo)
        sel = _select_blocks(imp_t, qpos_l, imp_t.shape[0], axis=0).T
        unpicked = ((sel - 1.0) * -NEG).astype(BF16)

        wl = WINDOW + tq
        w0 = pl.multiple_of(jnp.maximum(s0 - WINDOW, 0), tq)
        kpos = w0 + lax.broadcasted_iota(jnp.int32, (1, wl), 1)
        dlt = qpos_t - kpos
        bias_w = jnp.where((dlt >= 0) & (dlt < WINDOW), 0.0, NEG)
        k_w = kw_ref[0, pl.ds(w0, wl), gl].astype(BF16)
        v_w = with_ones(vw_ref[0, pl.ds(w0, wl), :], g)
        o_w = []
        for r0 in range(0, R, wb):
            s_w = _dot_nt(q_hi[r0:r0 + wb], k_w) + rows_of(bias_w, r0, wb)
            p_w = jnp.exp(s_w - jnp.max(s_w, axis=-1, keepdims=True))
            ov = _dot(p_w.astype(BF16), v_w)
            o_w.append(ov[:, gl] / jnp.maximum(ov[:, ones_lane(g)], TINY))
        pre.append((q_hi, unpicked, o_c, jnp.concatenate(o_w, axis=0)))

    heads = []
    for g in range(N_KV):
        gl = group_lanes(g)
        q_hi, unpicked, o_c, o_w = pre[g]
        m_sc[...] = jnp.full((R, LANE), NEG, F32)

        def sweep_max(c, carry):
            off = pl.multiple_of(c * kc, kc)
            k = ks_ref[0, pl.ds(off, kc), gl].astype(BF16)
            kpos = off + lax.broadcasted_iota(jnp.int32, (1, kc), 1)
            bias = _dot(unpicked, exp_ref[c]) + jnp.where(kpos <= qpos_t, 0.0, NEG)
            s = _dot_nt(q_hi, k) + rows_of(bias, 0, R)
            s_sc[c] = s
            mx = s[:, 0:LANE]
            for t in range(1, kc // LANE):
                mx = jnp.maximum(mx, s[:, t * LANE:(t + 1) * LANE])
            m_sc[...] = jnp.maximum(m_sc[...], mx)
            return carry

        lax.fori_loop(0, n_chunks, sweep_max, 0)
        m_sc[...] = jnp.broadcast_to(jnp.max(m_sc[...], axis=-1, keepdims=True), (R, LANE))
        acc_sc[...] = jnp.zeros((R, KV_WIDTH), F32)

        def sweep_pv(c, carry):
            off = pl.multiple_of(c * kc, kc)
            p = jnp.exp(s_sc[c] - jnp.concatenate([m_sc[...]] * (kc // LANE), axis=1))
            acc_sc[...] += _dot(p.astype(BF16), with_ones(vs_ref[0, pl.ds(off, kc), :], g))
            return carry

        lax.fori_loop(0, n_chunks, sweep_pv, 0)
        o_s = acc_sc[:, gl] / jnp.maximum(acc_sc[:, ones_lane(g)], TINY)

        gate = lambda i: stack(lambda r: sig[:, (g * Q_PER_KV + r) * 3 + i:(g * Q_PER_KV + r) * 3 + i + 1])
        og = gate(0) * o_c + gate(1) * o_s + gate(2) * o_w
        heads += [og[r * tq:(r + 1) * tq] for r in range(Q_PER_KV)]
    o_ref[0] = jnp.concatenate(heads, axis=1)


def _attn_prompt(q, gt, ck, cv, ks, vs, kw, vw, tq=128, kc=1024, wb=256):
    B, T, _ = q.shape
    assert T % kc == 0 and T % tq == 0 and T >= WINDOW + tq and (tq % wb == 0 or wb % tq == 0)
    ncb = ck.shape[1]
    ns = T // SEL_BLOCK
    i = jnp.arange(ncb)[None, :]
    j = jnp.arange(ns)[:, None]
    msel = ((i * CMP_STRIDE <= j * SEL_BLOCK + SEL_BLOCK - 1)
            & (i * CMP_STRIDE + CMP_LEN - 1 >= j * SEL_BLOCK)).astype(BF16)
    key_blk = (jnp.arange(T) // SEL_BLOCK).reshape(T // kc, 1, kc)
    expand = (key_blk == jnp.arange(ns)[None, :, None]).astype(BF16)
    tok = lambda w: pl.BlockSpec((1, tq, w), lambda b, t: (b, t, 0))
    seq = lambda a: pl.BlockSpec((1,) + a.shape[1:], lambda b, t: (b, 0, 0))
    const = lambda a: pl.BlockSpec(a.shape, lambda b, t: (0,) * a.ndim)
    R = Q_PER_KV * tq
    return pl.pallas_call(
        functools.partial(_attn_prompt_kernel, tq=tq, kc=kc, wb=wb),
        grid=(B, T // tq),
        in_specs=[tok(ATT_WIDTH), tok(LANE), seq(ck), seq(cv), seq(ks), seq(vs), seq(kw), seq(vw),
                  const(msel), const(expand)],
        out_specs=tok(ATT_WIDTH),
        out_shape=jax.ShapeDtypeStruct((B, T, ATT_WIDTH), F32),
        scratch_shapes=[pltpu.VMEM((R, LANE), F32), pltpu.VMEM((R, KV_WIDTH), F32), pltpu.VMEM((T // kc, R, kc), F32)],
        compiler_params=_params(("parallel", "arbitrary")),
        name="attn_prompt",
    )(q, gt, ck, cv, ks, vs, kw, vw, msel, expand)


def _group_lanes():
    row = lax.broadcasted_iota(jnp.int32, (N_HEADS, LANE), 0) // Q_PER_KV
    lane = lax.broadcasted_iota(jnp.int32, (N_HEADS, LANE), 1) // HEAD_DIM
    return row == lane


def _attn_sample_cmp_kernel(q_ref, ck_ref, cv_ref, msel_ref, oc_ref, sel_ref, *, qpos, n_blocks):
    q_hi, q_lo = _split(q_ref[0] * (HEAD_DIM ** -0.5))
    k_hi, k_lo = _split(ck_ref[0])
    s_c = _dot3_nt(q_hi, q_lo, k_hi, k_lo)
    ncb = s_c.shape[1]
    cend = lax.broadcasted_iota(jnp.int32, (1, ncb), 1) * CMP_STRIDE + (CMP_LEN - 1)
    p_c = _softmax_rows(s_c, cend <= qpos)
    o_c = _dot(p_c.astype(BF16), cv_ref[0].astype(BF16))
    oc_ref[0] = jnp.where(_group_lanes(), o_c, 0.0)
    top = jnp.sum(p_c[0:Q_PER_KV], axis=0, keepdims=True)
    bot = jnp.sum(p_c[Q_PER_KV:], axis=0, keepdims=True)
    row = lax.broadcasted_iota(jnp.int32, (N_HEADS, 1), 0)
    p_sum = jnp.where(row < Q_PER_KV, top, bot)
    ps_hi, ps_lo = _split(p_sum)
    msel = msel_ref[...]
    imp = _dot(ps_hi, msel) + _dot(ps_lo, msel)
    sel_ref[0] = _select_blocks(imp, jnp.full((N_HEADS, 1), qpos, jnp.int32), n_blocks)


def _attn_sample_cmp(qbd, ck, cv, qpos, n_blocks, ns_pad):
    DB, ncb, _ = ck.shape
    i = jnp.arange(ncb)[:, None]
    j = jnp.arange(ns_pad)[None, :]
    msel = ((i * CMP_STRIDE <= j * SEL_BLOCK + SEL_BLOCK - 1) & (i * CMP_STRIDE + CMP_LEN - 1 >= j * SEL_BLOCK)
            & (j < n_blocks)).astype(BF16)
    row = lambda a: pl.BlockSpec((1,) + a.shape[1:], lambda b: (b, 0, 0))
    return pl.pallas_call(
        functools.partial(_attn_sample_cmp_kernel, qpos=qpos, n_blocks=n_blocks),
        grid=(DB,),
        in_specs=[row(qbd), row(ck), row(cv), pl.BlockSpec(msel.shape, lambda b: (0, 0))],
        out_specs=[pl.BlockSpec((1, N_HEADS, LANE), lambda b: (b, 0, 0)),
                   pl.BlockSpec((1, N_HEADS, ns_pad), lambda b: (b, 0, 0))],
        out_shape=[jax.ShapeDtypeStruct((DB, N_HEADS, LANE), F32), jax.ShapeDtypeStruct((DB, N_HEADS, ns_pad), F32)],
        compiler_params=_params(("arbitrary",)),
        name="attn_sample_cmp",
    )(qbd, ck, cv, msel)


def _attn_sample_kernel(pt_ref, q_ref, sel_ref, oc_ref, gt_ref, ksn_ref, vsn_ref, kwb_ref, vwb_ref, kwn_ref, vwn_ref,
                        kpool_ref, vpool_ref, o_ref, kbuf, vbuf, ksem, vsem, *, n_pages, qpos):
    slot = _paged_fetch(pt_ref, (kpool_ref, vpool_ref), (kbuf, vbuf), (ksem, vsem), n_pages,
                        lambda buf, sl, p: buf.at[sl, :, pl.ds(p * PAGE_SIZE, PAGE_SIZE)])
    past = n_pages * PAGE_SIZE
    q = q_ref[0] * (HEAD_DIM ** -0.5)
    q_b = q.astype(BF16)
    sel = sel_ref[0]

    def attend(kt_old, vt_old, mask_old, k_new, v_new, mask_new):
        s = jnp.where(mask_old, _dot(q_b, kt_old.astype(BF16)), NEG)
        s_n = jnp.where(mask_new, jnp.sum(q * k_new, axis=-1, keepdims=True), NEG)
        m = jnp.maximum(jnp.max(s, axis=-1, keepdims=True), s_n)
        e = jnp.where(mask_old, jnp.exp(s - m), 0.0)
        e_n = jnp.where(mask_new, jnp.exp(s_n - m), 0.0)
        den = jnp.maximum(jnp.sum(e, axis=-1, keepdims=True) + e_n, TINY)
        return (_dot_nt(e.astype(BF16), vt_old.astype(BF16)) + e_n * v_new) / den

    assert LANE == 2 * SEL_BLOCK
    first = lax.broadcasted_iota(jnp.int32, (1, LANE), 1) < SEL_BLOCK
    picked = jnp.concatenate([jnp.where(first, sel[:, 2 * t:2 * t + 1], sel[:, 2 * t + 1:2 * t + 2])
                              for t in range(past // LANE)], axis=1)
    kpos = lax.broadcasted_iota(jnp.int32, (1, past), 1)
    nb_new = past // SEL_BLOCK
    o_s = attend(kbuf[slot], vbuf[slot], (picked > 0.5) & (kpos <= qpos), ksn_ref[0], vsn_ref[0],
                 (sel[:, nb_new:nb_new + 1] > 0.5) & (past <= qpos))
    wbuf = kwb_ref.shape[2]
    dlt = qpos - (past - wbuf + lax.broadcasted_iota(jnp.int32, (1, wbuf), 1))
    o_w = attend(kwb_ref[0], vwb_ref[0], (dlt >= 0) & (dlt < WINDOW), kwn_ref[0], vwn_ref[0],
                 jnp.full((N_HEADS, 1), (qpos - past >= 0) & (qpos - past < WINDOW)))
    sig = jax.nn.sigmoid(gt_ref[0])
    o = sig[:, 0:1] * oc_ref[0] + sig[:, 1:2] * o_s + sig[:, 2:3] * o_w
    o = jnp.where(_group_lanes(), o, 0.0)
    o_ref[0] = o[:, :HEAD_DIM] + o[:, HEAD_DIM:]


def _attn_sample(page_table, qbd, sel, oc, gt3, ksn, vsn, kwb, vwb, kwn, vwn, kpool, vpool, qpos):
    DB, n_pages = page_table.shape
    past = n_pages * PAGE_SIZE
    row = lambda a: pl.BlockSpec((1,) + a.shape[1:], lambda b, pt: (b, 0, 0))
    hbm = pl.BlockSpec(memory_space=pl.ANY)
    ins = [qbd, sel, oc, gt3, ksn, vsn, kwb, vwb, kwn, vwn]
    return pl.pallas_call(
        functools.partial(_attn_sample_kernel, n_pages=n_pages, qpos=qpos),
        grid_spec=pltpu.PrefetchScalarGridSpec(
            num_scalar_prefetch=1, grid=(DB,),
            in_specs=[row(a) for a in ins] + [hbm, hbm],
            out_specs=pl.BlockSpec((1, N_HEADS, HEAD_DIM), lambda b, pt: (b, 0, 0)),
            scratch_shapes=[pltpu.VMEM((2, KV_WIDTH, past), F32), pltpu.VMEM((2, KV_WIDTH, past), F32),
                            pltpu.SemaphoreType.DMA((2,)), pltpu.SemaphoreType.DMA((2,))]),
        out_shape=jax.ShapeDtypeStruct((DB, N_HEADS, HEAD_DIM), F32),
        compiler_params=_params(("arbitrary",)),
        name="attn_sample",
    )(page_table, *ins, kpool, vpool)


_SSM_HALF = SSM_WIDTH // 2
_SSM_ROW = SSM_GROUPS // 2 * SSM_STATE


def _ssm_params(a_re, a_im, log_dt, b_re, b_im, c_re, c_im, d):
    lam = lax.complex(a_re, a_im)
    step = jnp.exp(log_dt)[:, None]
    a_bar = jnp.exp(lam * step)
    b_bar = ((a_bar - 1.0) / lam)[..., None] * lax.complex(b_re, b_im)
    eye = jnp.eye(SSM_GROUPS // 2, dtype=F32)

    def b_mat(x):
        x = x.reshape(2, SSM_GROUPS // 2, SSM_STATE, SSM_GROUP)
        return jnp.einsum('hgpc,gk->hgckp', x, eye).reshape(2, _SSM_HALF, _SSM_ROW)

    def c_mat(x):
        x = x.reshape(2, SSM_GROUPS // 2, SSM_GROUP, SSM_STATE)
        return jnp.einsum('hgcp,gk->hgpkc', x, eye).reshape(2, _SSM_ROW, _SSM_HALF)

    bm = jnp.concatenate([b_mat(jnp.real(b_bar)), b_mat(jnp.imag(b_bar))], axis=2)
    cm = jnp.concatenate([c_mat(c_re), -c_mat(c_im)], axis=1)
    ar = jnp.real(a_bar).reshape(2, _SSM_ROW)
    ai = jnp.imag(a_bar).reshape(2, _SSM_ROW)
    bm_hi, bm_lo = _split(bm)
    cm_hi, cm_lo = _split(cm)
    return ar, ai, bm_hi, bm_lo, cm_hi, cm_lo, d.reshape(1, SSM_WIDTH)


def _ssm_prompt_kernel(u_ref, ar_ref, ai_ref, bh_ref, ch_ref, d_ref, y_ref, hr_ref, hi_ref,
                       sr_sc, si_sc, st_re, st_im, *, tc, nb):
    t = pl.program_id(0)
    rows = 2 * nb
    nlb = _SSM_ROW // LANE

    @pl.when(t == 0)
    def _():
        st_re[...] = jnp.zeros_like(st_re)
        st_im[...] = jnp.zeros_like(st_im)

    for b in range(nb):
        for hf in range(2):
            u_hi, u_lo = _split(u_ref[b, :, hf * _SSM_HALF:(hf + 1) * _SSM_HALF])
            bu = _dot(u_hi, bh_ref[hf]) + _dot(u_lo, bh_ref[hf])
            for k in range(nlb):
                sr_sc[k, pl.ds(b * 2 + hf, tc, stride=rows), :] = bu[:, k * LANE:(k + 1) * LANE]
                si_sc[k, pl.ds(b * 2 + hf, tc, stride=rows), :] = bu[:, _SSM_ROW + k * LANE:_SSM_ROW + (k + 1) * LANE]

    ar = ar_ref[...]
    ai = ai_ref[...]

    def step(i, carry):
        h_re, h_im = carry
        r0 = pl.multiple_of(i * rows, rows)
        n_re = ar * h_re - ai * h_im + sr_sc[:, pl.ds(r0, rows), :]
        n_im = ar * h_im + ai * h_re + si_sc[:, pl.ds(r0, rows), :]
        sr_sc[:, pl.ds(r0, rows), :] = n_re
        si_sc[:, pl.ds(r0, rows), :] = n_im
        return n_re, n_im

    h_re, h_im = lax.fori_loop(0, tc, step, (st_re[...], st_im[...]), unroll=4)
    st_re[...] = h_re
    st_im[...] = h_im
    hr_ref[...] = h_re
    hi_ref[...] = h_im

    for b in range(nb):
        for hf in range(2):
            gather = lambda sc: jnp.concatenate(
                [sc[k, pl.ds(b * 2 + hf, tc, stride=rows), :] for k in range(nlb)], axis=1)
            hs = jnp.concatenate([gather(sr_sc).astype(BF16), gather(si_sc).astype(BF16)], axis=1)
            cols = slice(hf * _SSM_HALF, (hf + 1) * _SSM_HALF)
            y_ref[b, :, cols] = _dot(hs, ch_ref[hf]) + d_ref[:, cols] * u_ref[b, :, cols]


def _ssm_prompt(u, sp, tc=128):
    B, T, W = u.shape
    ar, ai, bh, bl, ch, cl, d = sp
    rows = 2 * B
    nlb = _SSM_ROW // LANE
    tiles = lambda a: jnp.tile(a, (B, 1)).reshape(rows, nlb, LANE).transpose(1, 0, 2)
    const = lambda a: pl.BlockSpec(a.shape, lambda t: (0,) * a.ndim)
    blk = pl.BlockSpec((B, tc, W), lambda t: (0, t, 0))
    st = pl.BlockSpec((nlb, rows, LANE), lambda t: (0, 0, 0))
    ins = [tiles(ar), tiles(ai), bh, ch, d]
    st_shape = jax.ShapeDtypeStruct((nlb, rows, LANE), F32)
    y, h_re, h_im = pl.pallas_call(
        functools.partial(_ssm_prompt_kernel, tc=tc, nb=B),
        grid=(T // tc,),
        in_specs=[blk] + [const(a) for a in ins],
        out_specs=[blk, st, st],
        out_shape=[jax.ShapeDtypeStruct((B, T, W), F32), st_shape, st_shape],
        scratch_shapes=[pltpu.VMEM((nlb, tc * rows, LANE), F32), pltpu.VMEM((nlb, tc * rows, LANE), F32),
                        pltpu.VMEM((nlb, rows, LANE), F32), pltpu.VMEM((nlb, rows, LANE), F32)],
        compiler_params=_params(("arbitrary",)),
        name="ssm_prompt",
    )(u, *ins)
    rows_major = lambda a: a.transpose(1, 0, 2).reshape(rows, _SSM_ROW)
    return y, rows_major(h_re), rows_major(h_im)


def _ssm_sample_kernel(u_ref, h0r_ref, h0i_ref, ar_ref, ai_ref, bh_ref, bl_ref, ch_ref, cl_ref, d_ref,
                       y_ref, hr_ref, hi_ref):
    for hf in range(2):
        cols = slice(hf * _SSM_HALF, (hf + 1) * _SSM_HALF)
        lanes = slice(hf * _SSM_ROW, (hf + 1) * _SSM_ROW)
        u = u_ref[:, cols]
        u_hi, u_lo = _split(u)
        bu = _dot3(u_hi, u_lo, bh_ref[hf], bl_ref[hf])
        ar = ar_ref[hf:hf + 1, :]
        ai = ai_ref[hf:hf + 1, :]
        h_re = ar * h0r_ref[:, lanes] - ai * h0i_ref[:, lanes] + bu[:, :_SSM_ROW]
        h_im = ar * h0i_ref[:, lanes] + ai * h0r_ref[:, lanes] + bu[:, _SSM_ROW:]
        hr_ref[:, lanes] = h_re
        hi_ref[:, lanes] = h_im
        r_hi, r_lo = _split(h_re)
        i_hi, i_lo = _split(h_im)
        y = (_dot3(r_hi, r_lo, ch_ref[hf, :_SSM_ROW], cl_ref[hf, :_SSM_ROW])
             + _dot3(i_hi, i_lo, ch_ref[hf, _SSM_ROW:], cl_ref[hf, _SSM_ROW:]))
        y_ref[:, cols] = y + d_ref[:, cols] * u


def _ssm_sample(u, h0r, h0i, sp):
    n = u.shape[0]
    ins = [u, h0r, h0i, *sp]
    full = lambda a: pl.BlockSpec(a.shape, lambda i: (0,) * a.ndim)
    outs = [jax.ShapeDtypeStruct((n, SSM_WIDTH), F32), jax.ShapeDtypeStruct(h0r.shape, F32),
            jax.ShapeDtypeStruct(h0r.shape, F32)]
    return pl.pallas_call(
        _ssm_sample_kernel, grid=(1,),
        in_specs=[full(a) for a in ins], out_specs=[full(a) for a in outs], out_shape=outs,
        compiler_params=_params(("arbitrary",)),
        name="ssm_sample",
    )(*ins)


def _merge_kernel(x_ref, o_ref, y_ref, gm_ref, wa_ref, ws_ref, wo_ref, gf_ref, y2_ref, hf_ref):
    a = _dot(o_ref[...].astype(BF16), wa_ref[...])
    gl = _dot(jax.nn.gelu(y_ref[...]).astype(BF16), ws_ref[...])
    s = gl[:, :D_MODEL] * jax.nn.sigmoid(gl[:, D_MODEL:])
    gm = gm_ref[...]
    m = jax.nn.sigmoid(gm[:, :D_MODEL]) * a + jax.nn.sigmoid(gm[:, D_MODEL:]) * s
    y2 = x_ref[...] + _dot(m.astype(BF16), wo_ref[...])
    y2_ref[...] = y2
    hf_ref[...] = _rms(y2, gf_ref[...])


def _merge(x, o, y, gm, wa, ws, wo, gf, tm):
    n = x.shape[0]
    tok = lambda w: pl.BlockSpec((tm, w), lambda i: (i, 0))
    const = lambda a: pl.BlockSpec(a.shape, lambda i: (0,) * a.ndim)
    return pl.pallas_call(
        _merge_kernel, grid=(n // tm,),
        in_specs=[tok(D_MODEL), tok(ATT_WIDTH), tok(SSM_WIDTH), tok(2 * D_MODEL), const(wa), const(ws), const(wo),
                  const(gf)],
        out_specs=[tok(D_MODEL), tok(D_MODEL)],
        out_shape=[jax.ShapeDtypeStruct((n, D_MODEL), F32)] * 2,
        compiler_params=_params(("parallel",)),
        name="merge",
    )(x, o, y, gm, wa, ws, wo, gf)


def _cand_pairs():
    return [(a, b) for a in range(PEER_TOPK) for b in range(PEER_TOPK) if (a + 1) * (b + 1) <= PEER_TOPK]


def _top_values(s, k):
    vals = []
    for _ in range(k):
        mx = jnp.max(s, axis=0, keepdims=True)
        vals.append(mx)
        s = jnp.where(s == mx, LOWEST, s)
    return vals


def _peer_route_kernel(hf_ref, wqh_ref, wql_ref, k1h_ref, k1l_ref, k2h_ref, k2l_ref,
                       c1_ref, e1_ref, s2_ref, e2_ref, hh_sc, hl_sc):
    half = PEER_QDIM // 2

    @pl.when(pl.program_id(1) == 0)
    def _():
        hh_sc[...], hl_sc[...] = _split(hf_ref[...])

    q = _dot3(hh_sc[...], hl_sc[...], wqh_ref[...], wql_ref[...])
    q1h, q1l = _split(q[:, :half])
    q2h, q2l = _split(q[:, half:])
    s1 = _dot3_nt(k1h_ref[...], k1l_ref[...], q1h, q1l)
    s2 = _dot3_nt(k2h_ref[...], k2l_ref[...], q2h, q2l)
    v1 = _top_values(s1, PEER_TOPK)
    v2 = _top_values(s2, PEER_TOPK)
    cand = jnp.concatenate([v1[a] + v2[b] for a, b in _cand_pairs()], axis=0)
    thr = _top_values(cand, PEER_TOPK)[-1]
    top = v1[0] + v2[0]
    z = jnp.sum(jnp.where(cand >= thr, jnp.exp(cand - top), 0.0), axis=0, keepdims=True)
    c1 = jnp.full(s1.shape, -LOWEST, F32)
    for v in v2:
        c1 = jnp.where(s1 + v >= thr, v, c1)
    c1_ref[0] = c1
    s2_ref[0] = s2
    e1_ref[0] = jnp.exp(s1 - v1[0])
    e2_ref[0] = jnp.exp(s2 - v2[0]) / z


def _peer_route(hf, wq_hi, wq_lo, k1, k2, tn):
    n = hf.shape[0]
    k1h, k1l = _split(k1)
    k2h, k2l = _split(k2)
    const = lambda a: pl.BlockSpec(a.shape, lambda i, h: (0, 0))
    wq = pl.BlockSpec((D_MODEL, PEER_QDIM), lambda i, h: (0, h))
    keyed = pl.BlockSpec((1, PEER_NKEYS, tn), lambda i, h: (h, 0, i))
    shp = jax.ShapeDtypeStruct((PEER_HEADS, PEER_NKEYS, n), F32)
    return pl.pallas_call(
        _peer_route_kernel, grid=(n // tn, PEER_HEADS),
        in_specs=[pl.BlockSpec((tn, D_MODEL), lambda i, h: (i, 0)), wq, wq, const(k1h), const(k1l), const(k2h),
                  const(k2l)],
        out_specs=[keyed, keyed, keyed, keyed],
        out_shape=[shp, shp, shp, shp],
        scratch_shapes=[pltpu.VMEM((tn, D_MODEL), BF16), pltpu.VMEM((tn, D_MODEL), BF16)],
        compiler_params=_params(("parallel", "arbitrary")),
        name="peer_route",
    )(hf, wq_hi, wq_lo, k1h, k1l, k2h, k2l)


_PEER_JROWS = 32
_PEER_SLABS = 4


def _gelu_tanh(x):
    k = math.sqrt(2.0 / math.pi)
    hx = 0.5 * x
    return hx + hx * jnp.tanh(x * (k + (k * 0.044715) * (x * x)))


def _peer_main_kernel(hf_ref, y2_ref, u_ref, vt_ref, c1_ref, e1_ref, s2_ref, e2_ref, gn_ref, out_ref,
                      acc_sc, w_sc, act_sc, hfb_sc, *, tn, n_slab):
    e = pl.program_id(1)

    @pl.when(e == 0)
    def _():
        acc_sc[...] = jnp.zeros_like(acc_sc)
        hfb_sc[...] = hf_ref[...].astype(BF16)

    act_sc[...] = _gelu_tanh(_dot_nt(u_ref[...], hfb_sc[...]))

    for lc in range(tn // LANE):
        ln = slice(lc * LANE, (lc + 1) * LANE)

        def rows(jq, carry, ln=ln):
            j0 = pl.multiple_of(jq * _PEER_JROWS, _PEER_JROWS)
            for i0 in range(0, n_slab, _PEER_SLABS):
                slabs = range(i0, i0 + _PEER_SLABS)
                w = {ii: jnp.zeros((_PEER_JROWS, LANE), F32) for ii in slabs}
                for h in range(PEER_HEADS):
                    s2 = s2_ref[h, pl.ds(j0, _PEER_JROWS), ln]
                    e2 = e2_ref[h, pl.ds(j0, _PEER_JROWS), ln]
                    for ii in slabs:
                        w[ii] = w[ii] + e1_ref[h, ii:ii + 1, ln] * jnp.where(s2 >= c1_ref[h, ii:ii + 1, ln], e2, 0.0)
                for ii in slabs:
                    rs = pl.ds(ii * PEER_NKEYS + j0, _PEER_JROWS)
                    w_sc[rs, ln] = (w[ii] * act_sc[rs, ln]).astype(BF16)
            return carry

        lax.fori_loop(0, PEER_NKEYS // _PEER_JROWS, rows, 0)

    acc_sc[...] += _dot(vt_ref[...], w_sc[...])

    @pl.when(e == pl.num_programs(1) - 1)
    def _():
        out_ref[...] = _rms(y2_ref[...] + acc_sc[...].T, gn_ref[...])


def _peer_main(hf, y2, u_tab, vt_tab, c1, e1, s2, e2, gn, tn, n_slab=16):
    n = hf.shape[0]
    ec = n_slab * PEER_NKEYS
    n_exp = u_tab.shape[0]
    tok = pl.BlockSpec((tn, D_MODEL), lambda i, e: (i, 0))
    slab = pl.BlockSpec((PEER_HEADS, n_slab, tn), lambda i, e: (0, e, i))
    keyed = pl.BlockSpec((PEER_HEADS, PEER_NKEYS, tn), lambda i, e: (0, 0, i))
    return pl.pallas_call(
        functools.partial(_peer_main_kernel, tn=tn, n_slab=n_slab),
        grid=(n // tn, n_exp // ec),
        in_specs=[tok, tok, pl.BlockSpec((ec, D_MODEL), lambda i, e: (e, 0)),
                  pl.BlockSpec((D_MODEL, ec), lambda i, e: (0, e)), slab, slab, keyed, keyed,
                  pl.BlockSpec(gn.shape, lambda i, e: (0, 0))],
        out_specs=tok,
        out_shape=jax.ShapeDtypeStruct((n, D_MODEL), F32),
        scratch_shapes=[pltpu.VMEM((D_MODEL, tn), F32), pltpu.VMEM((ec, tn), BF16), pltpu.VMEM((ec, tn), F32),
                        pltpu.VMEM((tn, D_MODEL), BF16)],
        compiler_params=_params(("parallel", "arbitrary")),
        name="peer_main",
    )(hf, y2, u_tab, vt_tab, c1, e1, s2, e2, gn)


def _peer(hf, y2, wq_hi, wq_lo, k1, k2, u_tab, vt_tab, gn, tn):
    c1, e1, s2, e2 = _peer_route(hf, wq_hi, wq_lo, k1, k2, tn)
    return _peer_main(hf, y2, u_tab, vt_tab, c1, e1, s2, e2, gn, tn)


def _rope_tables(pos):
    half = HEAD_DIM // 2
    inv = ROPE_THETA ** (-jnp.arange(half, dtype=F32) / half)
    ang = pos.astype(F32)[:, None] * inv[None, :]
    cos, sin = jnp.cos(ang), jnp.sin(ang)
    return jnp.tile(jnp.concatenate([cos, cos], -1), (1, 2)), jnp.tile(jnp.concatenate([-sin, sin], -1), (1, 2))


def _proj_weight(w):
    gates = jnp.pad(w[:, 1280:1304], ((0, 0), (0, LANE - 3 * N_HEADS)))
    return jnp.concatenate([w[:, :1280], gates, w[:, 1304:1816], w[:, 1816:]], axis=1)


def _compress_weights(pe, w1, w2):
    r = CMP_LEN // CMP_STRIDE
    eye = jnp.eye(N_KV, dtype=F32)
    w1r = w1.reshape(r, CMP_STRIDE, HEAD_DIM, w1.shape[-1])
    wc = jnp.einsum('jsdh,gk->sgdjkh', w1r, eye).reshape(CMP_STRIDE * KV_WIDTH, r * N_KV * w1.shape[-1])
    pe2 = jnp.broadcast_to(pe.reshape(r, CMP_STRIDE, 1, HEAD_DIM), (r, CMP_STRIDE, N_KV, HEAD_DIM))
    pe2 = jnp.pad(pe2.reshape(r, CMP_STRIDE * KV_WIDTH), ((0, 8 - r), (0, 0)))
    w2bd = jnp.einsum('hd,gk->ghkd', w2, eye).reshape(N_KV * w2.shape[0], KV_WIDTH)
    return wc.astype(BF16), pe2, w2bd.astype(BF16)


def kernel(x_prompt, x_sample, cache_k_cmp, cache_v_cmp, cache_k_sel, cache_v_sel, cache_k_win, cache_v_win,
           state_ssm_re, state_ssm_im, page_table, norm_mix, w_in, cmp_pe_k, cmp_w1_k, cmp_w2_k, cmp_pe_v, cmp_w1_v,
           cmp_w2_v, ssm_a_re, ssm_a_im, ssm_log_dt, ssm_b_re, ssm_b_im, ssm_c_re, ssm_c_im, ssm_d, w_att_proj,
           w_ssm_glu, w_out, norm_ffn, peer_w_q, peer_keys1, peer_keys2, peer_u, peer_v, norm_final):
    assert w_in.shape[0] == 1, "single layer"
    B, T, _ = x_prompt.shape
    DB = x_sample.shape[0]
    n_pages = page_table.shape[1]
    past = n_pages * PAGE_SIZE
    n_pool = cache_k_cmp.shape[1]

    w_hi, w_lo = _split(_proj_weight(w_in[0]))
    g_mix = norm_mix[0].reshape(1, D_MODEL)
    cw = _compress_weights(cmp_pe_k[0], cmp_w1_k[0], cmp_w2_k[0]) + _compress_weights(cmp_pe_v[0], cmp_w1_v[0],
                                                                                       cmp_w2_v[0])
    sp = _ssm_params(ssm_a_re[0], ssm_a_im[0], ssm_log_dt[0], ssm_b_re[0], ssm_b_im[0], ssm_c_re[0], ssm_c_im[0],
                     ssm_d[0])
    wa, ws, wo = w_att_proj[0].astype(BF16), w_ssm_glu[0].astype(BF16), w_out[0].astype(BF16)
    g_ffn = norm_ffn[0].reshape(1, D_MODEL)
    g_fin = norm_final.reshape(1, D_MODEL)
    wq_hi, wq_lo = _split(peer_w_q[0])
    u_tab = peer_u[0].astype(BF16)
    vt_tab = peer_v[0].T.astype(BF16)

    cos_p, sin_p = _rope_tables(jnp.arange(T, dtype=jnp.int32))
    q, kc, vc, ks, vs, kw, vw, gt, u, gm = _project(x_prompt, g_mix, cos_p, sin_p, [w_hi], 512, False)
    chunks = lambda a: a.reshape(B, T // CMP_STRIDE, CMP_STRIDE * KV_WIDTH)
    ck, cv = _compress_prompt(chunks(kc), chunks(vc), cw)
    o_att = _attn_prompt(q, gt, ck, cv, ks, vs, kw, vw)
    y_ssm, hp_re, hp_im = _ssm_prompt(u, sp)
    n_p = B * T
    flat = lambda a: a.reshape(n_p, a.shape[-1])
    y2_p, hf_p = _merge(flat(x_prompt), flat(o_att), flat(y_ssm), flat(gm), wa, ws, wo, g_ffn, 512)
    y_prompt = _peer(hf_p, y2_p, wq_hi, wq_lo, peer_keys1[0], peer_keys2[0], u_tab, vt_tab, g_fin, 512)

    cos_s, sin_s = _rope_tables(jnp.full((DB,), past, jnp.int32))
    xs = x_sample.reshape(1, DB, D_MODEL)
    qs, kcs, vcs, kss, vss, kws, vws, gts, us, gms = [a[0] for a in
                                                      _project(xs, g_mix, cos_s, sin_s, [w_hi, w_lo], DB, True)]
    row3 = lambda a: a.reshape(DB, 1, KV_WIDTH)
    native = lambda c: jnp.transpose(c[0], (0, 2, 3, 1)).reshape(c.shape[1], KV_WIDTH, c.shape[2])
    cks, cvs = _compress_sample(page_table, native(cache_k_cmp), native(cache_v_cmp), row3(kcs), row3(vcs), cw)
    q5 = qs.reshape(DB, N_KV, Q_PER_KV, 1, HEAD_DIM) * jnp.eye(N_KV, dtype=F32).reshape(1, N_KV, 1, N_KV, 1)
    qbd = q5.reshape(DB, N_HEADS, KV_WIDTH)
    n_blocks = -(-(past + 1) // SEL_BLOCK)
    ns_pad = -(-n_blocks // LANE) * LANE
    oc, sel = _attn_sample_cmp(qbd, cks, cvs, past, n_blocks, ns_pad)
    gt3 = gts[:, :3 * N_HEADS].reshape(DB, N_HEADS, 3)
    o_s = _attn_sample(page_table, qbd, sel, oc, gt3, row3(kss), row3(vss), native(cache_k_win), native(cache_v_win),
                       row3(kws), row3(vws), native(cache_k_sel), native(cache_v_sel), past)
    h0r = state_ssm_re[0].reshape(DB, SSM_GROUPS * SSM_STATE)
    h0i = state_ssm_im[0].reshape(DB, SSM_GROUPS * SSM_STATE)
    ys_ssm, hs_re, hs_im = _ssm_sample(us, h0r, h0i, sp)
    y2_s, hf_s = _merge(x_sample.reshape(DB, D_MODEL), o_s.reshape(DB, ATT_WIDTH), ys_ssm, gms, wa, ws, wo, g_ffn, DB)
    y_sample = _peer(hf_s, y2_s, wq_hi, wq_lo, peer_keys1[0], peer_keys2[0], u_tab, vt_tab, g_fin, DB)

    kv5 = lambda a, n: a.reshape(1, n, -1, N_KV, HEAD_DIM)
    wb = min(WINDOW, T)
    st = lambda a, n: a.reshape(1, n, SSM_GROUPS, SSM_STATE)
    wbuf = cache_k_win.shape[2]
    nw = min(WINDOW, wbuf + 1)
    win_s = lambda old, new: jnp.concatenate([old[0].reshape(DB, wbuf, KV_WIDTH), new.reshape(DB, 1, KV_WIDTH)],
                                             axis=1)[:, wbuf + 1 - nw:]
    return (y_prompt.reshape(B, T, D_MODEL), y_sample.reshape(DB, 1, D_MODEL),
            kv5(kc, B), kv5(vc, B), kv5(ks, B), kv5(vs, B), kv5(kw[:, T - wb:], B), kv5(vw[:, T - wb:], B),
            st(hp_re, B), st(hp_im, B),
            kv5(kcs, DB), kv5(vcs, DB), kv5(kss, DB), kv5(vss, DB),
            kv5(win_s(cache_k_win, kws), DB), kv5(win_s(cache_v_win, vws), DB),
            st(hs_re, DB), st(hs_im, DB))
```
